```python
import math
import jax
import jax.numpy as jnp
from jax import lax
import numpy as np

D_MODEL = 1024
BATCH = 8
SEQ = 2048
DEPTH = 4

GRID_W = 64
CTX_LEN = 256
N_MIXERS = 3
N_MOD = 9
FFN_DIM = 2816
EPS = 1e-6

HYENA_ORDER = 2
HYENA_SHORT_K = 3
HYENA_EMB = 33
HYENA_BANDS = (HYENA_EMB - 1) // 2
HYENA_FILTER_HIDDEN = 64
HYENA_FAST_DECAY = 0.3
HYENA_SLOW_DECAY = 1.5
HYENA_DECAY_TARGET = 1e-2

SSM_INNER = 2 * D_MODEL
SSM_HEAD_DIM = 64
SSM_HEADS = SSM_INNER // SSM_HEAD_DIM
SSM_GROUPS = 8
HEADS_PER_GROUP = SSM_HEADS // SSM_GROUPS
SSM_STATE = 128
SSM_CONV_K = 3
SSM_CHUNK = 128
SSM_BC_DIM = SSM_GROUPS * SSM_STATE
SSM_CONV_DIM = SSM_INNER + 2 * SSM_BC_DIM
SSM_IN_DIM = SSM_INNER + SSM_CONV_DIM + 2 * SSM_HEADS

MLA_HEADS = 16
MLA_NOPE = 64
MLA_ROPE = 32
MLA_V = 64
MLA_Q_RANK = 768
MLA_KV_RANK = 256
MLA_QK = MLA_NOPE + MLA_ROPE
MLA_SCALE = MLA_QK ** -0.5
MLA_DOWN = MLA_Q_RANK + MLA_KV_RANK + MLA_ROPE
ROPE_AXIS = MLA_ROPE // 2
ROPE_BASE = 10000.0
Q_BLOCK = 128

N_HYENA = (DEPTH + N_MIXERS - 1) // N_MIXERS
N_MAMBA = (DEPTH + N_MIXERS - 2) // N_MIXERS
N_MLA = DEPTH // N_MIXERS

kernel_name = 'hybrid_hyena_ssd_mla_macaron_prefix'


def rms_norm(x, eps=EPS):
    xf = x.astype(jnp.float32)
    return (xf * lax.rsqrt(jnp.mean(xf * xf, axis=-1, keepdims=True) + eps)).astype(x.dtype)


def adaln(x, mod, k):
    return rms_norm(x) * (1 + mod[:, :, k + 1]) + mod[:, :, k]


def swiglu(h, w_in, w_out):
    g, u = jnp.split(h @ w_in, 2, axis=-1)
    return (jax.nn.silu(g) * u) @ w_out


def dwconv_centered(x, w, b):
    k = w.shape[0]
    y = lax.conv_general_dilated(x, w.astype(x.dtype)[:, None, :], window_strides=(1,),
                                 padding=[(k // 2, k // 2)],
                                 dimension_numbers=('NWC', 'WIO', 'NWC'),
                                 feature_group_count=x.shape[-1])
    return y + b


def hyena_filters(L, w1, b1, freq, w2, b2, w3):
    f32 = jnp.float32
    pos = jnp.arange(L, dtype=f32)
    t = jnp.linspace(0.0, 1.0, L, dtype=f32)
    bands = jnp.linspace(1e-4, HYENA_BANDS - 1, HYENA_BANDS, dtype=f32)
    ang = (2.0 * math.pi / L) * pos[:, None] * bands[None, :]
    z = jnp.concatenate([t[:, None], jnp.cos(ang), -jnp.sin(ang)], axis=-1)
    fr = freq.astype(f32)
    hid = jnp.sin(fr * (z @ w1.astype(f32) + b1.astype(f32)))
    hid = jnp.sin(fr * (hid @ w2.astype(f32) + b2.astype(f32)))
    h = (hid @ w3.astype(f32)).reshape(L, HYENA_ORDER, 2, D_MODEL)
    max_decay = math.log(HYENA_DECAY_TARGET) / HYENA_FAST_DECAY
    min_decay = math.log(HYENA_DECAY_TARGET) / HYENA_SLOW_DECAY
    deltas = jnp.abs(jnp.linspace(min_decay, max_decay, D_MODEL, dtype=f32))
    return h * jnp.exp(-t[:, None] * deltas[None, :])[:, None, None, :]


def two_sided(h_fwd, h_bwd):
    return jnp.concatenate([h_fwd, jnp.zeros_like(h_fwd[:1]), jnp.flip(h_bwd[1:], axis=0)], axis=0)


def long_conv(u, k):
    n = 2 * u.shape[1]
    uf = jnp.fft.rfft(u.astype(jnp.float32), n=n, axis=1)
    kf = jnp.fft.rfft(k, n=n, axis=0)
    return jnp.fft.irfft(uf * kf[None], n=n, axis=1)[:, :u.shape[1]].astype(u.dtype)


def hyena_mixer(a, in_w, in_b, conv_w, conv_b, w1, b1, freq, w2, b2, w3, bias, out_w, out_b):
    L = a.shape[1]
    u = dwconv_centered(a @ in_w + in_b, conv_w, conv_b)
    x1, x2, v = jnp.split(u, 3, axis=-1)
    filt = hyena_filters(L, w1, b1, freq, w2, b2, w3)
    z = x1 * (long_conv(v, two_sided(filt[:, 0, 0], filt[:, 0, 1])) + v * bias[0])
    y = x2 * (long_conv(z, two_sided(filt[:, 1, 0], filt[:, 1, 1])) + z * bias[1])
    return y @ out_w + out_b


def segsum(a):
    T = a.shape[-1]
    cs = jnp.cumsum(a, axis=-1)
    diff = cs[..., :, None] - cs[..., None, :]
    return jnp.where(jnp.tril(jnp.ones((T, T), dtype=bool)), diff, -jnp.inf)


def ssd_chunked(xdt, da, bm, cm, init):
    b, L, G, R, P = xdt.shape
    N = bm.shape[-1]
    nc = L // SSM_CHUNK
    x = xdt.reshape(b, nc, SSM_CHUNK, G, R, P)
    bc = bm.reshape(b, nc, SSM_CHUNK, G, N)
    cc = cm.reshape(b, nc, SSM_CHUNK, G, N)
    a = da.reshape(b, nc, SSM_CHUNK, G, R).transpose(0, 3, 4, 1, 2)
    a_cs = jnp.cumsum(a, axis=-1)
    decay_in = jnp.exp(segsum(a))
    cb = jnp.einsum('bclgn,bcsgn->bgcls', cc, bc)
    y_diag = jnp.einsum('bgcls,bgrcls,bcsgrp->bclgrp', cb, decay_in, x)
    decay_to_end = jnp.exp(a_cs[..., -1:] - a_cs)
    chunk_states = jnp.einsum('bclgn,bgrcl,bclgrp->bcgrpn', bc, decay_to_end, x)
    states = jnp.concatenate([init[:, None].astype(chunk_states.dtype), chunk_states], axis=1)
    chunk_tot = jnp.pad(a_cs[..., -1], [(0, 0), (0, 0), (0, 0), (1, 0)])
    decay_chunk = jnp.exp(segsum(chunk_tot))
    states = jnp.einsum('bgrzc,bcgrpn->bzgrpn', decay_chunk, states)
    prev_states, final = states[:, :-1], states[:, -1]
    y_off = jnp.einsum('bclgn,bcgrpn,bgrcl->bclgrp', cc, prev_states, jnp.exp(a_cs))
    return (y_diag + y_off).reshape(b, L, G, R, P), final


def mamba2_mixer(a, ac, in_w, conv_w, conv_b, dt_bias, a_log, d_skip, norm_w, out_w, want_ctx):
    f32 = jnp.float32
    A = -jnp.exp(a_log.astype(f32))

    def project(h):
        b, L, _ = h.shape
        zxbcdt = h @ in_w
        z = zxbcdt[..., :SSM_INNER]
        xbc = jax.nn.silu(dwconv_centered(zxbcdt[..., SSM_INNER:SSM_INNER + SSM_CONV_DIM], conv_w, conv_b))
        xs = xbc[..., :SSM_INNER].reshape(b, L, SSM_GROUPS, HEADS_PER_GROUP, SSM_HEAD_DIM)
        bm = xbc[..., SSM_INNER:SSM_INNER + SSM_BC_DIM].reshape(b, L, SSM_GROUPS, SSM_STATE)
        cm = xbc[..., SSM_INNER + SSM_BC_DIM:].reshape(b, L, SSM_GROUPS, SSM_STATE)
        dt = zxbcdt[..., SSM_INNER + SSM_CONV_DIM:].astype(f32).reshape(b, L, 2, SSM_HEADS)
        dt = jax.nn.softplus(dt + dt_bias.astype(f32))
        return z, xs, bm, cm, dt

    def scan(xs, bm, cm, dt, d, init):
        dt_d = dt[:, :, d].reshape(dt.shape[0], dt.shape[1], SSM_GROUPS, HEADS_PER_GROUP)
        return ssd_chunked(xs * dt_d[..., None], dt_d * A[d].reshape(SSM_GROUPS, HEADS_PER_GROUP), bm, cm, init)

    def finish(y, xs, z):
        b, L = z.shape[:2]
        y = (y + d_skip.astype(f32).reshape(SSM_GROUPS, HEADS_PER_GROUP, 1) * xs).reshape(b, L, SSM_INNER)
        y = y * jax.nn.silu(z.astype(f32))
        yg = y.reshape(b, L, SSM_GROUPS, SSM_INNER // SSM_GROUPS)
        yg = yg * lax.rsqrt(jnp.mean(yg * yg, axis=-1, keepdims=True) + EPS)
        return (yg.reshape(b, L, SSM_INNER) * norm_w).astype(z.dtype) @ out_w

    flip = lambda t: jnp.flip(t, axis=1)
    zc, xc, bcx, ccx, dtc = project(ac)
    zl, xl, bl, cl, dtl = project(a)
    zero = jnp.zeros((a.shape[0], SSM_GROUPS, HEADS_PER_GROUP, SSM_HEAD_DIM, SSM_STATE), f32)
    yc_f, s_f = scan(xc, bcx, ccx, dtc, 0, zero)
    yc_b, s_b = scan(flip(xc), flip(bcx), flip(ccx), flip(dtc), 1, zero)
    yl_f, _ = scan(xl, bl, cl, dtl, 0, s_f)
    yl_b, _ = scan(flip(xl), flip(bl), flip(cl), flip(dtl), 1, s_b)
    y = finish(yl_f + flip(yl_b), xl, zl)
    yc = finish(yc_f + flip(yc_b), xc, zc) if want_ctx else None
    return y, yc


def axial_rope(row, col):
    inv = 1.0 / (ROPE_BASE ** (jnp.arange(0, ROPE_AXIS, 2, dtype=jnp.float32) / ROPE_AXIS))
    ar = row.astype(jnp.float32)[:, None] * inv[None, :]
    ac = col.astype(jnp.float32)[:, None] * inv[None, :]
    ang = jnp.concatenate([ar, ar, ac, ac], axis=-1)
    return jnp.cos(ang), jnp.sin(ang)


def rope_2d(x, cos, sin):
    def rot(v):
        v1, v2 = jnp.split(v, 2, axis=-1)
        return jnp.concatenate([-v2, v1], axis=-1)
    xr, xc = jnp.split(x, 2, axis=-1)
    return x * cos + jnp.concatenate([rot(xr), rot(xc)], axis=-1) * sin


def mla_mixer(a, ac, w_down, q_norm, w_uq, kv_norm, w_ukv, w_o, cos, sin, want_ctx):
    def down(h):
        d = h @ w_down
        cq = rms_norm(d[..., :MLA_Q_RANK]) * q_norm
        ckv = rms_norm(d[..., MLA_Q_RANK:MLA_Q_RANK + MLA_KV_RANK]) * kv_norm
        return cq, ckv, d[..., MLA_Q_RANK + MLA_KV_RANK:]

    def up_q(cq):
        q = (cq @ w_uq).reshape(cq.shape[0], cq.shape[1], MLA_HEADS, MLA_QK)
        return q[..., :MLA_NOPE], q[..., MLA_NOPE:]

    def up_kv(ckv):
        kv = (ckv @ w_ukv).reshape(ckv.shape[0], ckv.shape[1], MLA_HEADS, MLA_NOPE + MLA_V)
        return kv[..., :MLA_NOPE], kv[..., MLA_NOPE:]

    cq_c, ckv_c, kc_pe = down(ac)
    kc_nope, vc = up_kv(ckv_c)
    n_ctx = kc_nope.shape[1]
    cq_l, ckv_l, k_pe = down(a)
    k_nope, v = up_kv(ckv_l)
    k_rot = rope_2d(k_pe, cos, sin)
    q_nope, q_pe = up_q(cq_l)
    q_rot = rope_2d(q_pe, cos[:, None], sin[:, None])

    b, L = a.shape[:2]
    nb = L // Q_BLOCK
    to_blocks = lambda t: jnp.swapaxes(t.reshape(b, nb, Q_BLOCK, *t.shape[2:]), 0, 1)

    def attend(blk):
        qn, qr, qp = blk
        s_ctx = (jnp.einsum('bqhd,bkhd->bhqk', qn, kc_nope)
                 + jnp.einsum('bqhr,bkr->bhqk', qp, kc_pe))
        s_lat = (jnp.einsum('bqhd,bkhd->bhqk', qn, k_nope)
                 + jnp.einsum('bqhr,bkr->bhqk', qr, k_rot))
        p = jax.nn.softmax(jnp.concatenate([s_ctx, s_lat], axis=-1).astype(jnp.float32) * MLA_SCALE,
                           axis=-1).astype(v.dtype)
        return (jnp.einsum('bhqk,bkhd->bqhd', p[..., :n_ctx], vc)
                + jnp.einsum('bhqk,bkhd->bqhd', p[..., n_ctx:], v))

    o = lax.map(attend, (to_blocks(q_nope), to_blocks(q_rot), to_blocks(q_pe)))
    y = jnp.swapaxes(o, 0, 1).reshape(b, L, MLA_HEADS * MLA_V) @ w_o
    yc = None
    if want_ctx:
        qcn, qcp = up_q(cq_c)
        s = jnp.einsum('bqhd,bkhd->bhqk', qcn, kc_nope) + jnp.einsum('bqhr,bkr->bhqk', qcp, kc_pe)
        p = jax.nn.softmax(s.astype(jnp.float32) * MLA_SCALE, axis=-1).astype(vc.dtype)
        yc = jnp.einsum('bhqk,bkhd->bqhd', p, vc).reshape(b, n_ctx, MLA_HEADS * MLA_V) @ w_o
    return y, yc


def setup_inputs(seed: int = 0) -> dict:
    key = jax.random.key(seed)
    ks = jax.random.split(key, 40)
    f32 = jnp.float32

    def nrm(i, shape, scale):
        return jax.random.normal(ks[i], shape, f32) * scale

    def gain(i, shape):
        return 1.0 + nrm(i, shape, 0.02)

    dt0 = jnp.exp(jax.random.uniform(ks[24], (N_MAMBA, 2, SSM_HEADS), f32, math.log(1e-3), math.log(1e-1)))
    return {
        'x': nrm(0, (BATCH, SEQ, D_MODEL), 1.0),
        'c': nrm(1, (BATCH, D_MODEL), 1.0),
        'ctx': nrm(2, (BATCH, CTX_LEN, D_MODEL), 1.0),
        'c_ctx': nrm(3, (D_MODEL,), 1.0),
        'ada_w': nrm(4, (DEPTH, D_MODEL, N_MOD * D_MODEL), 0.5 * D_MODEL ** -0.5),
        'ada_b': nrm(5, (DEPTH, N_MOD * D_MODEL), 0.02),
        'ffn_in': nrm(6, (DEPTH, 2, D_MODEL, 2 * FFN_DIM), D_MODEL ** -0.5),
        'ffn_out': nrm(7, (DEPTH, 2, FFN_DIM, D_MODEL), FFN_DIM ** -0.5),
        'hy_in_w': nrm(8, (N_HYENA, D_MODEL, 3 * D_MODEL), D_MODEL ** -0.5),
        'hy_in_b': nrm(9, (N_HYENA, 3 * D_MODEL), 0.02),
        'hy_conv_w': nrm(10, (N_HYENA, HYENA_SHORT_K, 3 * D_MODEL), HYENA_SHORT_K ** -0.5),
        'hy_conv_b': nrm(11, (N_HYENA, 3 * D_MODEL), 0.02),
        'hy_pos_w1': nrm(12, (N_HYENA, HYENA_EMB, HYENA_FILTER_HIDDEN), HYENA_EMB ** -0.5),
        'hy_pos_b1': nrm(13, (N_HYENA, HYENA_FILTER_HIDDEN), 0.1),
        'hy_freq': 1.0 + nrm(14, (N_HYENA, HYENA_FILTER_HIDDEN), 0.1),
        'hy_pos_w2': nrm(15, (N_HYENA, HYENA_FILTER_HIDDEN, HYENA_FILTER_HIDDEN), HYENA_FILTER_HIDDEN ** -0.5),
        'hy_pos_b2': nrm(16, (N_HYENA, HYENA_FILTER_HIDDEN), 0.1),
        'hy_pos_w3': nrm(17, (N_HYENA, HYENA_FILTER_HIDDEN, HYENA_ORDER * 2 * D_MODEL), 0.05 * HYENA_FILTER_HIDDEN ** -0.5),
        'hy_bias': nrm(18, (N_HYENA, HYENA_ORDER, D_MODEL), 0.5),
        'hy_out_w': nrm(19, (N_HYENA, D_MODEL, D_MODEL), D_MODEL ** -0.5),
        'hy_out_b': nrm(20, (N_HYENA, D_MODEL), 0.02),
        'mb_in_w': nrm(21, (N_MAMBA, D_MODEL, SSM_IN_DIM), D_MODEL ** -0.5),
        'mb_conv_w': nrm(22, (N_MAMBA, SSM_CONV_K, SSM_CONV_DIM), SSM_CONV_K ** -0.5),
        'mb_conv_b': nrm(23, (N_MAMBA, SSM_CONV_DIM), 0.02),
        'mb_dt_bias': dt0 + jnp.log(-jnp.expm1(-dt0)),
        'mb_A_log': jnp.log(jax.random.uniform(ks[25], (N_MAMBA, 2, SSM_HEADS), f32, 1.0, 16.0)),
        'mb_D': 1.0 + nrm(26, (N_MAMBA, SSM_HEADS), 0.1),
        'mb_norm_w': gain(27, (N_MAMBA, SSM_INNER)),
        'mb_out_w': nrm(28, (N_MAMBA, SSM_INNER, D_MODEL), SSM_INNER ** -0.5),
        'mla_w_down': nrm(29, (N_MLA, D_MODEL, MLA_DOWN), D_MODEL ** -0.5),
        'mla_q_norm': gain(30, (N_MLA, MLA_Q_RANK)),
        'mla_w_uq': nrm(31, (N_MLA, MLA_Q_RANK, MLA_HEADS * MLA_QK), MLA_Q_RANK ** -0.5),
        'mla_kv_norm': gain(32, (N_MLA, MLA_KV_RANK)),
        'mla_w_ukv': nrm(33, (N_MLA, MLA_KV_RANK, MLA_HEADS * (MLA_NOPE + MLA_V)), MLA_KV_RANK ** -0.5),
        'mla_w_o': nrm(34, (N_MLA, MLA_HEADS * MLA_V, D_MODEL), (MLA_HEADS * MLA_V) ** -0.5),
        'final_norm_w': gain(35, (D_MODEL,)),
    }


def reference(x, c, ctx, c_ctx, ada_w, ada_b, ffn_in, ffn_out,
              hy_in_w, hy_in_b, hy_conv_w, hy_conv_b, hy_pos_w1, hy_pos_b1, hy_freq,
              hy_pos_w2, hy_pos_b2, hy_pos_w3, hy_bias, hy_out_w, hy_out_b,
              mb_in_w, mb_conv_w, mb_conv_b, mb_dt_bias, mb_A_log, mb_D, mb_norm_w, mb_out_w,
              mla_w_down, mla_q_norm, mla_w_uq, mla_kv_norm, mla_w_ukv, mla_w_o,
              final_norm_w):
    n_lat = x.shape[1]
    rows = n_lat // GRID_W
    row = jnp.repeat(jnp.arange(rows), GRID_W)
    col = jnp.tile(jnp.arange(GRID_W), rows)
    cos, sin = axial_rope(row, col)

    sc = jax.nn.silu(c)
    scc = jax.nn.silu(c_ctx)
    h, hc = x, ctx
    for i in range(DEPTH):
        kind, j, last = i % N_MIXERS, i // N_MIXERS, i == DEPTH - 1
        mod = (sc @ ada_w[i] + ada_b[i]).reshape(-1, 1, N_MOD, D_MODEL)
        modc = (scc @ ada_w[i] + ada_b[i]).reshape(1, 1, N_MOD, D_MODEL)
        ctx_needed = not (last and kind == 0)
        ctx_out = not last

        h = h + 0.5 * mod[:, :, 2] * swiglu(adaln(h, mod, 0), ffn_in[i, 0], ffn_out[i, 0])
        if ctx_needed:
            hc = hc + 0.5 * modc[:, :, 2] * swiglu(adaln(hc, modc, 0), ffn_in[i, 0], ffn_out[i, 0])

        a = adaln(h, mod, 3)
        ac = adaln(hc, modc, 3) if ctx_needed else None
        if kind == 0:
            hp = (hy_in_w[j], hy_in_b[j], hy_conv_w[j], hy_conv_b[j], hy_pos_w1[j], hy_pos_b1[j],
                  hy_freq[j], hy_pos_w2[j], hy_pos_b2[j], hy_pos_w3[j], hy_bias[j], hy_out_w[j], hy_out_b[j])
            y = hyena_mixer(a, *hp)
            yc = hyena_mixer(ac, *hp) if ctx_out else None
        elif kind == 1:
            y, yc = mamba2_mixer(a, ac, mb_in_w[j], mb_conv_w[j], mb_conv_b[j], mb_dt_bias[j],
                                 mb_A_log[j], mb_D[j], mb_norm_w[j], mb_out_w[j], ctx_out)
        else:
            y, yc = mla_mixer(a, ac, mla_w_down[j], mla_q_norm[j], mla_w_uq[j], mla_kv_norm[j],
                              mla_w_ukv[j], mla_w_o[j], cos, sin, ctx_out)
        h = h + mod[:, :, 5] * y.astype(h.dtype)

        h = h + 0.5 * mod[:, :, 8] * swiglu(adaln(h, mod, 6), ffn_in[i, 1], ffn_out[i, 1])
        if ctx_out:
            hc = hc + modc[:, :, 5] * yc.astype(hc.dtype)
            hc = hc + 0.5 * modc[:, :, 8] * swiglu(adaln(hc, modc, 6), ffn_in[i, 1], ffn_out[i, 1])

    return rms_norm(h) * final_norm_w
```

```python
import functools
import math

import jax
import jax.numpy as jnp
from jax import lax
from jax.experimental import pallas as pl
from jax.experimental.pallas import tpu as pltpu

F32 = jnp.float32
BF16 = jnp.bfloat16
HIGHEST = lax.Precision.HIGHEST

D_MODEL = 1024
DEPTH = 4
GRID_W = 64
CTX_LEN = 256
N_MIXERS = 3
N_MOD = 9
FFN_DIM = 2816
EPS = 1e-6

HYENA_EMB = 33
HYENA_BANDS = (HYENA_EMB - 1) // 2
HYENA_FILTER_HIDDEN = 64
HYENA_FAST_DECAY = 0.3
HYENA_SLOW_DECAY = 1.5
HYENA_DECAY_TARGET = 1e-2

SSM_INNER = 2 * D_MODEL
SSM_HEAD_DIM = 64
SSM_HEADS = SSM_INNER // SSM_HEAD_DIM
SSM_GROUPS = 8
HEADS_PER_GROUP = SSM_HEADS // SSM_GROUPS
SSM_STATE = 128
SSM_CHUNK = 128
SSM_BC_DIM = SSM_GROUPS * SSM_STATE
SSM_CONV_DIM = SSM_INNER + 2 * SSM_BC_DIM
SSM_IN_DIM = SSM_INNER + SSM_CONV_DIM + 2 * SSM_HEADS
SSM_IN_PAD = 6272

MLA_HEADS = 16
MLA_NOPE = 64
MLA_ROPE = 32
MLA_V = 64
MLA_Q_RANK = 768
MLA_KV_RANK = 256
MLA_QK = MLA_NOPE + MLA_ROPE
MLA_SCALE = MLA_QK ** -0.5
MLA_DOWN_PAD = 1152
ROPE_AXIS = MLA_ROPE // 2
ROPE_BASE = 10000.0

MOD_ROWS = 2048
VMEM_LIMIT_BYTES = 56 * 1024 * 1024


def _params(*sem):
    return pltpu.CompilerParams(dimension_semantics=sem, vmem_limit_bytes=VMEM_LIMIT_BYTES)


def _rms(x):
    return x * lax.rsqrt(jnp.mean(x * x, axis=-1, keepdims=True) + EPS)


def _silu(x):
    return x * jax.nn.sigmoid(x)


def _dot(a, b, **kw):
    return jnp.dot(a, b, preferred_element_type=F32, **kw)


def _dot_nt(a, b):
    return lax.dot_general(a, b, (((1,), (1,)), ((), ())), preferred_element_type=F32)


def _mod_kernel(x_ref, w_ref, b_ref, o_ref):
    a = _silu(x_ref[...]).astype(BF16)
    o_ref[...] = _dot(a, w_ref[...].astype(BF16)) + b_ref[...]


def _mod_all(cc, ada_w, ada_b):
    n = N_MOD * D_MODEL
    tn = 1024
    return pl.pallas_call(
        _mod_kernel,
        grid=(DEPTH, n // tn),
        in_specs=[pl.BlockSpec((16, D_MODEL), lambda l, j: (0, 0)),
                  pl.BlockSpec((None, D_MODEL, tn), lambda l, j: (l, 0, j)),
                  pl.BlockSpec((None, 1, tn), lambda l, j: (l, 0, j))],
        out_specs=pl.BlockSpec((None, 16, tn), lambda l, j: (l, 0, j)),
        out_shape=jax.ShapeDtypeStruct((DEPTH, 16, n), F32),
        compiler_params=_params("parallel", "parallel"),
        name="adaln_mod",
    )(cc, ada_w, ada_b.reshape(DEPTH, 1, n))


def _mm_kernel(*refs, pro, epi, has_bias, slot, gslot):
    it = iter(refs)
    x_ref, w_ref = next(it), next(it)
    mod_ref = next(it) if (pro == "adaln" or epi == "gres") else None
    nw_ref = next(it) if pro == "rms" else None
    b_ref = next(it) if has_bias else None
    res_ref = next(it) if epi == "gres" else None
    o_ref, xa_ref = next(it), next(it)

    @pl.when(pl.program_id(1) == 0)
    def _():
        x = x_ref[...].astype(F32)
        if pro == "adaln":
            x = _rms(x) * (1.0 + mod_ref[0, slot + 1:slot + 2, :]) + mod_ref[0, slot:slot + 1, :]
        elif pro == "rms":
            x = _rms(x) * nw_ref[...]
        xa_ref[...] = x.astype(BF16)

    acc = _dot(xa_ref[...], w_ref[...])
    if has_bias:
        acc = acc + b_ref[...]
    if epi == "gres":
        acc = res_ref[...] + mod_ref[0, gslot:gslot + 1, :] * acc
    o_ref[...] = acc.astype(o_ref.dtype)


def _mm(x, w, *, rows, tn, tm=512, x_colblock=0, pro="none", epi="none", mod=None, slot=0, gslot=0,
        nw=None, bias=None, res=None, out_dtype=F32, name="mm"):
    k, n = w.shape
    assert rows % tm == 0 and n % tn == 0 and MOD_ROWS % tm == 0
    if epi == "gres":
        assert tn == n == D_MODEL
    per = MOD_ROWS // tm
    ins = [x, w]
    specs = [pl.BlockSpec((tm, k), lambda i, j: (i, x_colblock)),
             pl.BlockSpec((k, tn), lambda i, j: (0, j))]
    if pro == "adaln" or epi == "gres":
        ins.append(mod)
        specs.append(pl.BlockSpec((1, N_MOD, D_MODEL), lambda i, j: (i // per, 0, 0)))
    if pro == "rms":
        ins.append(nw.reshape(1, k))
        specs.append(pl.BlockSpec((1, k), lambda i, j: (0, 0)))
    if bias is not None:
        ins.append(bias.reshape(1, n))
        specs.append(pl.BlockSpec((1, tn), lambda i, j: (0, j)))
    if epi == "gres":
        ins.append(res)
        specs.append(pl.BlockSpec((tm, tn), lambda i, j: (i, j)))
    return pl.pallas_call(
        functools.partial(_mm_kernel, pro=pro, epi=epi, has_bias=bias is not None, slot=slot, gslot=gslot),
        grid=(rows // tm, n // tn),
        in_specs=specs,
        out_specs=pl.BlockSpec((tm, tn), lambda i, j: (i, j)),
        out_shape=jax.ShapeDtypeStruct((rows, n), out_dtype),
        scratch_shapes=[pltpu.VMEM((tm, k), BF16)],
        compiler_params=_params("parallel", "arbitrary"),
        name=name,
    )(*ins)


def _ffn_kernel(*refs, slot, final):
    if final:
        x_ref, mod_ref, wg_ref, wu_ref, wo_ref, fw_ref, o_ref, xa_ref, acc_ref = refs
    else:
        x_ref, mod_ref, wg_ref, wu_ref, wo_ref, o_ref, xa_ref, acc_ref = refs
    j = pl.program_id(1)

    @pl.when(j == 0)
    def _():
        x = x_ref[...]
        a = _rms(x) * (1.0 + mod_ref[0, slot + 1:slot + 2, :]) + mod_ref[0, slot:slot + 1, :]
        xa_ref[...] = a.astype(BF16)
        acc_ref[...] = jnp.zeros_like(acc_ref)

    xa = xa_ref[...]
    g = _dot(xa, wg_ref[...])
    u = _dot(xa, wu_ref[...])
    acc_ref[...] += _dot((_silu(g) * u).astype(BF16), wo_ref[...])

    @pl.when(j == pl.num_programs(1) - 1)
    def _():
        out = x_ref[...] + (0.5 * mod_ref[0, slot + 2:slot + 3, :]) * acc_ref[...]
        if final:
            out = _rms(out) * fw_ref[...]
        o_ref[...] = out


def _ffn(h, mod, w_in, w_out, *, rows, slot, final_w=None, tm=1024, tf=256):
    nf = FFN_DIM // tf
    per = MOD_ROWS // tm
    final = final_w is not None
    ins = [h, mod, w_in, w_in, w_out]
    specs = [pl.BlockSpec((tm, D_MODEL), lambda i, j: (i, 0)),
             pl.BlockSpec((1, N_MOD, D_MODEL), lambda i, j: (i // per, 0, 0)),
             pl.BlockSpec((D_MODEL, tf), lambda i, j: (0, j)),
             pl.BlockSpec((D_MODEL, tf), lambda i, j: (0, j + nf)),
             pl.BlockSpec((tf, D_MODEL), lambda i, j: (j, 0))]
    if final:
        ins.append(final_w.reshape(1, D_MODEL))
        specs.append(pl.BlockSpec((1, D_MODEL), lambda i, j: (0, 0)))
    return pl.pallas_call(
        functools.partial(_ffn_kernel, slot=slot, final=final),
        grid=(rows // tm, nf),
        in_specs=specs,
        out_specs=pl.BlockSpec((tm, D_MODEL), lambda i, j: (i, 0)),
        out_shape=jax.ShapeDtypeStruct((rows, D_MODEL), F32),
        scratch_shapes=[pltpu.VMEM((tm, D_MODEL), BF16), pltpu.VMEM((tm, D_MODEL), F32)],
        compiler_params=_params("parallel", "arbitrary"),
        name="ffn",
    )(*ins)


def _dwconv_kernel(x_ref, w_ref, b_ref, o_ref, *, ctx_block, act):
    rows = x_ref.shape[0]
    x = x_ref[...]
    seg = jnp.where(pl.program_id(0) == ctx_block, CTX_LEN, rows)
    pos = lax.broadcasted_iota(jnp.int32, x.shape, 0) & (seg - 1)
    prev = jnp.where(pos == 0, 0.0, pltpu.roll(x, 1, 0))
    nxt = jnp.where(pos == seg - 1, 0.0, pltpu.roll(x, rows - 1, 0))
    y = w_ref[0:1, :] * prev + w_ref[1:2, :] * x + w_ref[2:3, :] * nxt + b_ref[...]
    if act:
        y = _silu(y)
    o_ref[...] = y.astype(o_ref.dtype)


def _dwconv(x, w, b, *, rows, col0, act, tc=512, name="dwconv"):
    c = w.shape[1]
    cb0 = col0 // tc
    return pl.pallas_call(
        functools.partial(_dwconv_kernel, ctx_block=8, act=act),
        grid=(rows // MOD_ROWS, c // tc),
        in_specs=[pl.BlockSpec((MOD_ROWS, tc), lambda i, j: (i, cb0 + j)),
                  pl.BlockSpec((3, tc), lambda i, j: (0, j)),
                  pl.BlockSpec((1, tc), lambda i, j: (0, j))],
        out_specs=pl.BlockSpec((MOD_ROWS, tc), lambda i, j: (i, j)),
        out_shape=jax.ShapeDtypeStruct((rows, c), F32),
        compiler_params=_params("parallel", "parallel"),
        name=name,
    )(x, w, b.reshape(1, c))


def _filter_kernel(z_ref, w1_ref, b1_ref, fr_ref, w2_ref, b2_ref, w3_ref, dl_ref, sum_ref, diff_ref):
    tl = z_ref.shape[0]
    z = z_ref[...]
    fr = fr_ref[...]
    hid = jnp.sin(fr * (_dot(z, w1_ref[...], precision=HIGHEST) + b1_ref[...]))
    hid = jnp.sin(fr * (_dot(hid, w2_ref[...], precision=HIGHEST) + b2_ref[...]))
    h = _dot(hid, w3_ref[...], precision=HIGHEST)
    dec = jnp.exp(-z[:, 0:1] * dl_ref[...])
    row = lax.broadcasted_iota(jnp.int32, dec.shape, 0) + pl.program_id(0) * tl
    for o in range(2):
        hf = h[:, (2 * o) * D_MODEL:(2 * o + 1) * D_MODEL] * dec
        hb = jnp.where(row == 0, 0.0, h[:, (2 * o + 1) * D_MODEL:(2 * o + 2) * D_MODEL] * dec)
        sum_ref[:, o * D_MODEL:(o + 1) * D_MODEL] = (hf + hb).astype(BF16)
        diff_ref[:, o * D_MODEL:(o + 1) * D_MODEL] = (hb - hf).astype(BF16)


def _hyena_filters(L, w1, b1, freq, w2, b2, w3):
    pos = jnp.arange(L, dtype=F32)
    t = jnp.linspace(0.0, 1.0, L, dtype=F32)
    bands = jnp.linspace(1e-4, HYENA_BANDS - 1, HYENA_BANDS, dtype=F32)
    ang = (2.0 * math.pi / L) * pos[:, None] * bands[None, :]
    z = jnp.concatenate([t[:, None], jnp.cos(ang), -jnp.sin(ang)], axis=-1)
    hd = 128
    ph = hd - HYENA_FILTER_HIDDEN
    z = jnp.pad(z, ((0, 0), (0, hd - HYENA_EMB)))
    w1 = jnp.pad(w1, ((0, hd - HYENA_EMB), (0, ph)))
    w2 = jnp.pad(w2, ((0, ph), (0, ph)))
    w3 = jnp.pad(w3, ((0, ph), (0, 0)))
    b1, b2, freq = (jnp.pad(t, (0, ph)) for t in (b1, b2, freq))
    max_decay = math.log(HYENA_DECAY_TARGET) / HYENA_FAST_DECAY
    min_decay = math.log(HYENA_DECAY_TARGET) / HYENA_SLOW_DECAY
    deltas = jnp.abs(jnp.linspace(min_decay, max_decay, D_MODEL, dtype=F32)).reshape(1, D_MODEL)
    tl = min(L, 512)
    full = lambda shape: pl.BlockSpec(shape, lambda i: (0, 0))
    return pl.pallas_call(
        _filter_kernel,
        grid=(L // tl,),
        in_specs=[pl.BlockSpec((tl, hd), lambda i: (i, 0)), full((hd, hd)), full((1, hd)), full((1, hd)),
                  full((hd, hd)), full((1, hd)), full((hd, 4 * D_MODEL)), full((1, D_MODEL))],
        out_specs=[pl.BlockSpec((tl, 2 * D_MODEL), lambda i: (i, 0))] * 2,
        out_shape=[jax.ShapeDtypeStruct((L, 2 * D_MODEL), BF16)] * 2,
        compiler_params=_params("parallel"),
        name="hyena_filter",
    )(z, w1, b1.reshape(1, hd), freq.reshape(1, hd), w2, b2.reshape(1, hd), w3, deltas)


def _lmm_kernel(*refs, epi, tf, scale):
    it = iter(refs)
    a_ref, x_ref = next(it), next(it)
    if epi == "spec":
        k_ref = next(it)
    elif epi == "gate":
        g_ref, v_ref, bias_ref = next(it), next(it), next(it)
    o_ref, xb_ref = next(it), next(it)

    @pl.when(pl.program_id(2) == 0)
    def _():
        xb_ref[...] = x_ref[...].astype(BF16)

    acc = _dot(a_ref[...], xb_ref[...])
    if epi == "spec":
        xr, xs = acc[:tf], acc[tf:]
        kr, ki = k_ref[0:tf, :], k_ref[tf:2 * tf, :]
        o_ref[0:tf, :] = ((xr * kr + xs * ki) * scale).astype(o_ref.dtype)
        o_ref[tf:2 * tf, :] = ((xs * kr - xr * ki) * scale).astype(o_ref.dtype)
    elif epi == "gate":
        o_ref[...] = (g_ref[...] * (acc + v_ref[...].astype(F32) * bias_ref[...])).astype(o_ref.dtype)
    else:
        o_ref[...] = acc.astype(o_ref.dtype)


def _lmm(a, x, *, nb, ncb, tm, n, x_row0=0, x_cb0=0, epi="none", tf=0, scale=1.0, kspec=None, k_cb=0,
         g=None, g_cb=0, g_row0=0, v=None, v_cb=0, v_row0=0, bias=None, out_dtype=F32, name="lmm"):
    mo, k = a.shape
    nm = mo // tm
    xr0 = x_row0 // k
    ins = [a, x]
    specs = [pl.BlockSpec((tm, k), lambda b, c, m: (m, 0)),
             pl.BlockSpec((k, n), lambda b, c, m: (xr0 + b, x_cb0 + c))]
    if epi == "spec":
        ins.append(kspec)
        specs.append(pl.BlockSpec((tm, n), lambda b, c, m: (m, k_cb)))
    elif epi == "gate":
        gr0, vr0 = g_row0 // tm, v_row0 // tm
        ins += [g, v, bias.reshape(1, n)]
        specs += [pl.BlockSpec((tm, n), lambda b, c, m: (gr0 + b * nm + m, g_cb)),
                  pl.BlockSpec((tm, n), lambda b, c, m: (vr0 + b * nm + m, v_cb)),
                  pl.BlockSpec((1, n), lambda b, c, m: (0, 0))]
    return pl.pallas_call(
        functools.partial(_lmm_kernel, epi=epi, tf=tf, scale=scale),
        grid=(nb, ncb, nm),
        in_specs=specs,
        out_specs=pl.BlockSpec((tm, n), lambda b, c, m: (b * nm + m, c)),
        out_shape=jax.ShapeDtypeStruct((nb * mo, ncb * n), out_dtype),
        scratch_shapes=[pltpu.VMEM((k, n), BF16)],
        compiler_params=_params("parallel", "parallel", "arbitrary"),
        name=name,
    )(*ins)


def _dft_mats(L, tf):
    idx = jnp.arange(L, dtype=jnp.int32)
    ph = ((2 * idx[:, None] + 1) * idx[None, :]) % (4 * L)
    ang = ph.astype(F32) * (2.0 * math.pi / (4 * L))
    c, s = jnp.cos(ang), jnp.sin(ang)
    nf = L // tf
    fwd = jnp.stack([c.reshape(nf, tf, L), s.reshape(nf, tf, L)], axis=1).reshape(2 * L, L)
    inv = jnp.stack([c.T.reshape(L, nf, tf), s.T.reshape(L, nf, tf)], axis=2).reshape(L, 2 * L)
    return c.astype(BF16), s.astype(BF16), fwd.astype(BF16), inv.astype(BF16)


def _interleave(kr, ki, tf):
    L, n = kr.shape
    return jnp.stack([kr.reshape(L // tf, tf, n), ki.reshape(L // tf, tf, n)], axis=1).reshape(2 * L, n)


def _hyena_core(u, L, nb, row0, fparams, hy_bias, name):
    tf = min(L, 256)
    tmi = min(L, 256)
    cmat, smat, fwd, inv = _dft_mats(L, tf)
    hsum, hdiff = _hyena_filters(L, *fparams)
    kr = _lmm(cmat, hsum, nb=1, ncb=2, tm=tf, n=D_MODEL, name=name + "_kr")
    ki = _lmm(smat, hdiff, nb=1, ncb=2, tm=tf, n=D_MODEL, name=name + "_ki")
    kspec = _interleave(kr, ki, tf)
    scale = 1.0 / L
    y1 = _lmm(fwd, u, nb=nb, ncb=1, tm=2 * tf, n=D_MODEL, x_row0=row0, x_cb0=2, epi="spec", tf=tf, scale=scale,
              kspec=kspec, k_cb=0, out_dtype=BF16, name=name + "_fwd1")
    z = _lmm(inv, y1, nb=nb, ncb=1, tm=tmi, n=D_MODEL, epi="gate", g=u, g_cb=0, g_row0=row0, v=u, v_cb=2,
             v_row0=row0, bias=hy_bias[0], out_dtype=F32, name=name + "_inv1")
    y2 = _lmm(fwd, z, nb=nb, ncb=1, tm=2 * tf, n=D_MODEL, epi="spec", tf=tf, scale=scale,
              kspec=kspec, k_cb=1, out_dtype=BF16, name=name + "_fwd2")
    return _lmm(inv, y2, nb=nb, ncb=1, tm=tmi, n=D_MODEL, epi="gate", g=u, g_cb=1, g_row0=row0, v=z, v_cb=0,
                bias=hy_bias[1], out_dtype=BF16, name=name + "_inv2")


def _softplus(x):
    return jnp.maximum(x, 0.0) + jnp.log(1.0 + jnp.exp(-jnp.abs(x)))


def _ssd_kernel(xs_ref, b_ref, c_ref, dt_ref, dtt_ref, dtb_ref, dtbt_ref, al_ref, alt_ref, y_ref, st_ref, xd_ref):
    T = SSM_CHUNK
    P = SSM_HEAD_DIM
    R = HEADS_PER_GROUP
    d = pl.program_id(1)

    @pl.when(pl.program_id(2) == 0)
    def _():
        st_ref[...] = jnp.zeros_like(st_ref)

    sgn = jnp.where(d == 0, 1, -1)
    ri = lax.broadcasted_iota(jnp.int32, (T, T), 0)
    ci = lax.broadcasted_iota(jnp.int32, (T, T), 1)
    mask = (ri - ci) * sgn >= 0
    tri_col = mask.astype(F32)
    tri_row = ((ci - ri) * sgn >= 0).astype(F32)

    dt = _softplus(dt_ref[...] + dtb_ref[...])
    a = dt * -jnp.exp(al_ref[...])
    at = _softplus(dtt_ref[...] + dtbt_ref[...]) * -jnp.exp(alt_ref[...])
    cs_col = _dot(tri_col, a, precision=HIGHEST)
    cs_row = _dot(at, tri_row, precision=HIGHEST)
    tot = jnp.sum(a, axis=0, keepdims=True)
    e_in = jnp.exp(cs_col)
    e_out = jnp.exp(tot - cs_col)
    e_tot = jnp.exp(tot)

    for g in range(SSM_GROUPS):
        bg = b_ref[:, g * SSM_STATE:(g + 1) * SSM_STATE]
        cg = c_ref[:, g * SSM_STATE:(g + 1) * SSM_STATE].astype(BF16)
        bt = jnp.transpose(bg).astype(BF16)
        cb = _dot(cg, bt)
        cs_prev = _dot(cg, st_ref[g].astype(BF16))
        for r in range(R):
            h = g * R + r
            lm = jnp.where(mask, jnp.exp(jnp.minimum(cs_col[:, h:h + 1] - cs_row[h:h + 1, :], 0.0)), 0.0)
            xdt = xs_ref[:, h * P:(h + 1) * P] * dt[:, h:h + 1]
            yd = _dot((cb * lm).astype(BF16), xdt.astype(BF16))
            y_ref[:, h * P:(h + 1) * P] = yd + cs_prev[:, r * P:(r + 1) * P] * e_in[:, h:h + 1]
            xd_ref[:, r * P:(r + 1) * P] = (xdt * e_out[:, h:h + 1]).astype(BF16)
        upd = _dot(bt, xd_ref[...])
        for r in range(R):
            h = g * R + r
            st_ref[g, :, r * P:(r + 1) * P] = st_ref[g, :, r * P:(r + 1) * P] * e_tot[:, h:h + 1] + upd[:, r * P:(r + 1) * P]


def _ssd(xbc, dta, dtt, dt_bias, a_log, *, nb):
    m = xbc.shape[0]
    T = SSM_CHUNK
    H = SSM_HEADS
    lat_c = MOD_ROWS // T
    ctx_c = CTX_LEN // T
    ctx0 = nb * lat_c
    steps = lat_c + ctx_c

    def rb(b, d, t):
        ctx = ctx0 + ctx_c * b + jnp.where(d == 0, t, ctx_c - 1 - t)
        lat = lat_c * b + jnp.where(d == 0, t - ctx_c, steps - 1 - t)
        return jnp.where(t < ctx_c, ctx, lat)

    return pl.pallas_call(
        _ssd_kernel,
        grid=(nb, 2, steps),
        in_specs=[pl.BlockSpec((T, SSM_INNER), lambda b, d, t: (rb(b, d, t), 0)),
                  pl.BlockSpec((T, SSM_BC_DIM), lambda b, d, t: (rb(b, d, t), 2)),
                  pl.BlockSpec((T, SSM_BC_DIM), lambda b, d, t: (rb(b, d, t), 3)),
                  pl.BlockSpec((None, T, H), lambda b, d, t: (d, rb(b, d, t), 0)),
                  pl.BlockSpec((None, H, T), lambda b, d, t: (d, 0, rb(b, d, t))),
                  pl.BlockSpec((None, 1, H), lambda b, d, t: (d, 0, 0)),
                  pl.BlockSpec((None, H, 1), lambda b, d, t: (d, 0, 0)),
                  pl.BlockSpec((None, 1, H), lambda b, d, t: (d, 0, 0)),
                  pl.BlockSpec((None, H, 1), lambda b, d, t: (d, 0, 0))],
        out_specs=pl.BlockSpec((None, T, SSM_INNER), lambda b, d, t: (d, rb(b, d, t), 0)),
        out_shape=jax.ShapeDtypeStruct((2, m, SSM_INNER), F32),
        scratch_shapes=[pltpu.VMEM((SSM_GROUPS, SSM_STATE, HEADS_PER_GROUP * SSM_HEAD_DIM), F32),
                        pltpu.VMEM((T, HEADS_PER_GROUP * SSM_HEAD_DIM), BF16)],
        compiler_params=_params("parallel", "parallel", "arbitrary"),
        name="ssd_scan",
    )(xbc, xbc, xbc, dta, dtt, dt_bias.reshape(2, 1, H), dt_bias.reshape(2, H, 1),
      a_log.reshape(2, 1, H), a_log.reshape(2, H, 1))


def _ssd_finish_kernel(yf_ref, yb_ref, xs_ref, z_ref, dsk_ref, nw_ref, o_ref):
    gw = SSM_INNER // SSM_GROUPS
    z = z_ref[...]
    y = (yf_ref[...] + yb_ref[...] + dsk_ref[...] * xs_ref[...]) * _silu(z)
    for g in range(SSM_GROUPS):
        yg = _rms(y[:, g * gw:(g + 1) * gw])
        o_ref[:, g * gw:(g + 1) * gw] = (yg * nw_ref[:, g * gw:(g + 1) * gw]).astype(o_ref.dtype)


def _ssd_finish(y2, xbc, zx, d_skip, norm_w, *, rows, tm=256):
    n = SSM_INNER
    return pl.pallas_call(
        _ssd_finish_kernel,
        grid=(rows // tm,),
        in_specs=[pl.BlockSpec((None, tm, n), lambda i: (0, i, 0)),
                  pl.BlockSpec((None, tm, n), lambda i: (1, i, 0)),
                  pl.BlockSpec((tm, n), lambda i: (i, 0)),
                  pl.BlockSpec((tm, n), lambda i: (i, 0)),
                  pl.BlockSpec((1, n), lambda i: (0, 0)),
                  pl.BlockSpec((1, n), lambda i: (0, 0))],
        out_specs=pl.BlockSpec((tm, n), lambda i: (i, 0)),
        out_shape=jax.ShapeDtypeStruct((rows, n), BF16),
        compiler_params=_params("parallel"),
        name="ssd_finish",
    )(y2, y2, xbc, zx, jnp.repeat(d_skip, SSM_HEAD_DIM).reshape(1, n), norm_w.reshape(1, n))


HEAD_W = 2 * MLA_NOPE


def _attn_kernel(*refs, has_lat):
    if has_lat:
        q_ref, kvc_ref, krc_ref, kv_ref, kr_ref, tq_ref, tk_ref, o_ref, kc_scr, k_scr = refs
    else:
        q_ref, kvc_ref, krc_ref, o_ref, kc_scr = refs
    R = MLA_ROPE

    def build_keys():
        krc = krc_ref[...].astype(F32)
        lane_c = lax.broadcasted_iota(jnp.int32, krc.shape, 1)
        pe_c = jnp.where((lane_c >= MLA_NOPE) & (lane_c < MLA_NOPE + R), pltpu.roll(krc, MLA_NOPE, 1), 0.0)
        for hh in range(2):
            kvh = kvc_ref[:, hh * HEAD_W:(hh + 1) * HEAD_W].astype(F32)
            kc_scr[hh] = jnp.where(lane_c < MLA_NOPE, kvh, pe_c).astype(BF16)
        if has_lat:
            t = kr_ref[...].astype(F32) * tk_ref[...]
            lane = lax.broadcasted_iota(jnp.int32, t.shape, 1)
            krot = jnp.where(lane < R, t + pltpu.roll(t, HEAD_W - R, 1), 0.0)
            krr = pltpu.roll(krot, MLA_NOPE, 1) + pltpu.roll(krot, MLA_NOPE + R, 1)
            for hh in range(2):
                kvh = kv_ref[:, hh * HEAD_W:(hh + 1) * HEAD_W].astype(F32)
                k_scr[hh] = jnp.where(lane < MLA_NOPE, kvh, krr).astype(BF16)

    if has_lat:
        pl.when(pl.program_id(2) == 0)(build_keys)
    else:
        build_keys()

    res = []
    for hh in range(2):
        q = q_ref[:, hh * HEAD_W:(hh + 1) * HEAD_W]
        s_c = _dot_nt(q, kc_scr[hh]) * MLA_SCALE
        mx = jnp.max(s_c, axis=-1, keepdims=True)
        if has_lat:
            ql = (q.astype(F32) * tq_ref[...]).astype(BF16)
            s_l = _dot_nt(ql, k_scr[hh]) * MLA_SCALE
            mx = jnp.maximum(mx, jnp.max(s_l, axis=-1, keepdims=True))
            p_l = jnp.exp(s_l - mx)
            den = jnp.sum(p_l, axis=-1, keepdims=True)
            acc = _dot(p_l.astype(BF16), kv_ref[:, hh * HEAD_W:(hh + 1) * HEAD_W])
        p_c = jnp.exp(s_c - mx)
        pv_c = _dot(p_c.astype(BF16), kvc_ref[:, hh * HEAD_W:(hh + 1) * HEAD_W])
        if has_lat:
            den = den + jnp.sum(p_c, axis=-1, keepdims=True)
            acc = acc + pv_c
        else:
            den = jnp.sum(p_c, axis=-1, keepdims=True)
            acc = pv_c
        res.append(acc / den)
    lane_o = lax.broadcasted_iota(jnp.int32, res[0].shape, 1)
    o_ref[...] = jnp.where(lane_o < MLA_V, pltpu.roll(res[0], MLA_V, 1), res[1]).astype(o_ref.dtype)


def _attention(q, kv, dn, cs, *, nb, tq=512):
    L = MOD_ROWS
    hp = MLA_HEADS // 2
    w = 2 * HEAD_W
    nq = L // tq
    cb = nb * (L // CTX_LEN)
    kr_cb = (MLA_Q_RANK + MLA_KV_RANK) // HEAD_W
    tab_q = jnp.concatenate([jnp.ones((L, MLA_NOPE), F32), cs], axis=1)
    tab_k = jnp.concatenate([cs, jnp.zeros((L, HEAD_W - 2 * MLA_ROPE), F32)], axis=1)
    lat = pl.pallas_call(
        functools.partial(_attn_kernel, has_lat=True),
        grid=(nb, hp, nq),
        in_specs=[pl.BlockSpec((tq, w), lambda b, p, i: (b * nq + i, p)),
                  pl.BlockSpec((CTX_LEN, w), lambda b, p, i: (cb + b, p)),
                  pl.BlockSpec((CTX_LEN, HEAD_W), lambda b, p, i: (cb + b, kr_cb)),
                  pl.BlockSpec((L, w), lambda b, p, i: (b, p)),
                  pl.BlockSpec((L, HEAD_W), lambda b, p, i: (b, kr_cb)),
                  pl.BlockSpec((tq, HEAD_W), lambda b, p, i: (i, 0)),
                  pl.BlockSpec((L, HEAD_W), lambda b, p, i: (0, 0))],
        out_specs=pl.BlockSpec((tq, 2 * MLA_V), lambda b, p, i: (b * nq + i, p)),
        out_shape=jax.ShapeDtypeStruct((nb * L, MLA_HEADS * MLA_V), BF16),
        scratch_shapes=[pltpu.VMEM((2, CTX_LEN, HEAD_W), BF16), pltpu.VMEM((2, L, HEAD_W), BF16)],
        compiler_params=_params("parallel", "parallel", "arbitrary"),
        name="mla_attn",
    )(q, kv, dn, kv, dn, tab_q, tab_k)
    ctx = pl.pallas_call(
        functools.partial(_attn_kernel, has_lat=False),
        grid=(nb, hp),
        in_specs=[pl.BlockSpec((CTX_LEN, w), lambda b, p: (cb + b, p)),
                  pl.BlockSpec((CTX_LEN, w), lambda b, p: (cb + b, p)),
                  pl.BlockSpec((CTX_LEN, HEAD_W), lambda b, p: (cb + b, kr_cb))],
        out_specs=pl.BlockSpec((CTX_LEN, 2 * MLA_V), lambda b, p: (b, p)),
        out_shape=jax.ShapeDtypeStruct((nb * CTX_LEN, MLA_HEADS * MLA_V), BF16),
        scratch_shapes=[pltpu.VMEM((2, CTX_LEN, HEAD_W), BF16)],
        compiler_params=_params("parallel", "parallel"),
        name="mla_attn_ctx",
    )(q, kv, dn)
    return jnp.concatenate([lat, ctx], axis=0)


def _rot_cols(w):
    wp = w.reshape(w.shape[:-1] + (2, 2, ROPE_AXIS // 2))
    return jnp.stack([-wp[..., 1, :], wp[..., 0, :]], axis=-2).reshape(w.shape)


def _rope_table(n_lat):
    rows = n_lat // GRID_W
    row = jnp.repeat(jnp.arange(rows), GRID_W)
    col = jnp.tile(jnp.arange(GRID_W), rows)
    inv = 1.0 / (ROPE_BASE ** (jnp.arange(0, ROPE_AXIS, 2, dtype=F32) / ROPE_AXIS))
    ar = row.astype(F32)[:, None] * inv[None, :]
    ac = col.astype(F32)[:, None] * inv[None, :]
    ang = jnp.concatenate([ar, ar, ac, ac], axis=-1)
    return jnp.concatenate([jnp.cos(ang), jnp.sin(ang)], axis=-1)


def _hyena_layer(h, mod, rows, nb, j, with_ctx, p):
    u0 = _mm(h, p["hy_in_w"][j].astype(BF16), rows=rows, tn=1024, pro="adaln", mod=mod, slot=3,
             bias=p["hy_in_b"][j], name="hyena_in")
    u = _dwconv(u0, p["hy_conv_w"][j], p["hy_conv_b"][j], rows=rows, col0=0, act=False, name="hyena_conv")
    fparams = (p["hy_pos_w1"][j], p["hy_pos_b1"][j], p["hy_freq"][j], p["hy_pos_w2"][j], p["hy_pos_b2"][j],
               p["hy_pos_w3"][j])
    y = _hyena_core(u, MOD_ROWS, nb, 0, fparams, p["hy_bias"][j], "hy_lat")
    if with_ctx:
        yc = _hyena_core(u, CTX_LEN, nb, nb * MOD_ROWS, fparams, p["hy_bias"][j], "hy_ctx")
        y = jnp.concatenate([y, yc], axis=0)
    return _mm(y, p["hy_out_w"][j].astype(BF16), rows=rows, tn=D_MODEL, epi="gres", mod=mod, gslot=5,
               bias=p["hy_out_b"][j], res=h, name="hyena_out")


def _mamba_layer(h, mod, rows, nb, j, p):
    w_in = jnp.pad(p["mb_in_w"][j].astype(BF16), ((0, 0), (0, SSM_IN_PAD - SSM_IN_DIM)))
    zx = _mm(h, w_in, rows=rows, tn=SSM_IN_PAD // 7, pro="adaln", mod=mod, slot=3, name="mamba_in")
    xbc = _dwconv(zx, p["mb_conv_w"][j], p["mb_conv_b"][j], rows=rows, col0=SSM_INNER, act=True, name="mamba_conv")
    dtr = zx[:, SSM_INNER + SSM_CONV_DIM:SSM_IN_DIM].reshape(rows, 2, SSM_HEADS)
    dta = jnp.transpose(dtr, (1, 0, 2))
    dtt = jnp.transpose(dtr, (1, 2, 0))
    y2 = _ssd(xbc, dta, dtt, p["mb_dt_bias"][j], p["mb_A_log"][j], nb=nb)
    yn = _ssd_finish(y2, xbc, zx, p["mb_D"][j], p["mb_norm_w"][j], rows=rows)
    return _mm(yn, p["mb_out_w"][j].astype(BF16), rows=rows, tn=D_MODEL, epi="gres", mod=mod, gslot=5, res=h,
               name="mamba_out")


def _mla_layer(h, mod, rows, nb, j, cs, p):
    wd = p["mla_w_down"][j]
    kpe_w = wd[:, MLA_Q_RANK + MLA_KV_RANK:]
    wd = jnp.concatenate([wd, _rot_cols(kpe_w),
                          jnp.zeros((D_MODEL, MLA_DOWN_PAD - wd.shape[1] - MLA_ROPE), F32)], axis=1).astype(BF16)
    wq = p["mla_w_uq"][j].reshape(MLA_Q_RANK, MLA_HEADS, MLA_QK)
    wq = jnp.concatenate([wq, _rot_cols(wq[..., MLA_NOPE:])], axis=-1).reshape(MLA_Q_RANK, -1).astype(BF16)
    dn = _mm(h, wd, rows=rows, tn=MLA_DOWN_PAD, pro="adaln", mod=mod, slot=3, out_dtype=F32, name="mla_down")
    q = _mm(dn, wq, rows=rows, tn=1024, pro="rms", nw=p["mla_q_norm"][j], out_dtype=BF16, name="mla_uq")
    kv = _mm(dn, p["mla_w_ukv"][j].astype(BF16), rows=rows, tn=1024, x_colblock=MLA_Q_RANK // MLA_KV_RANK,
             pro="rms", nw=p["mla_kv_norm"][j], out_dtype=BF16, name="mla_ukv")
    o = _attention(q, kv, dn, cs, nb=nb)
    return _mm(o, p["mla_w_o"][j].astype(BF16), rows=rows, tn=D_MODEL, epi="gres", mod=mod, gslot=5, res=h,
               name="mla_out")


def kernel(x, c, ctx, c_ctx, ada_w, ada_b, ffn_in, ffn_out, hy_in_w, hy_in_b, hy_conv_w, hy_conv_b, hy_pos_w1, hy_pos_b1, hy_freq, hy_pos_w2, hy_pos_b2, hy_pos_w3, hy_bias, hy_out_w, hy_out_b, mb_in_w, mb_conv_w, mb_conv_b, mb_dt_bias, mb_A_log, mb_D, mb_norm_w, mb_out_w, mla_w_down, mla_q_norm, mla_w_uq, mla_kv_norm, mla_w_ukv, mla_w_o, final_norm_w):
    p = dict(hy_in_w=hy_in_w, hy_in_b=hy_in_b, hy_conv_w=hy_conv_w, hy_conv_b=hy_conv_b, hy_pos_w1=hy_pos_w1,
             hy_pos_b1=hy_pos_b1, hy_freq=hy_freq, hy_pos_w2=hy_pos_w2, hy_pos_b2=hy_pos_b2, hy_pos_w3=hy_pos_w3,
             hy_bias=hy_bias, hy_out_w=hy_out_w, hy_out_b=hy_out_b, mb_in_w=mb_in_w, mb_conv_w=mb_conv_w,
             mb_conv_b=mb_conv_b, mb_dt_bias=mb_dt_bias, mb_A_log=mb_A_log, mb_D=mb_D, mb_norm_w=mb_norm_w,
             mb_out_w=mb_out_w, mla_w_down=mla_w_down, mla_q_norm=mla_q_norm, mla_w_uq=mla_w_uq,
             mla_kv_norm=mla_kv_norm, mla_w_ukv=mla_w_ukv, mla_w_o=mla_w_o)
    nb, n_lat, _ = x.shape
    assert n_lat == MOD_ROWS and nb * ctx.shape[1] == MOD_ROWS and ctx.shape[1] == CTX_LEN
    lat_rows = nb * n_lat
    all_rows = lat_rows + MOD_ROWS
    cs = _rope_table(n_lat)

    cc = jnp.concatenate([c, c_ctx[None], jnp.zeros((16 - nb - 1, D_MODEL), F32)], axis=0)
    mods = _mod_all(cc, ada_w, ada_b).reshape(DEPTH, 16, N_MOD, D_MODEL)
    h = jnp.concatenate([x.reshape(lat_rows, D_MODEL), ctx.reshape(MOD_ROWS, D_MODEL)], axis=0)

    for i in range(DEPTH):
        kind, j, last = i % N_MIXERS, i // N_MIXERS, i == DEPTH - 1
        mod = mods[i]
        ctx_needed = not (last and kind == 0)
        ctx_out = not last
        rows = all_rows if ctx_needed else lat_rows
        h = _ffn(h, mod, ffn_in[i, 0].astype(BF16), ffn_out[i, 0].astype(BF16), rows=rows, slot=0)
        if kind == 0:
            h = _hyena_layer(h, mod, rows, nb, j, ctx_out, p)
        elif kind == 1:
            h = _mamba_layer(h, mod, rows, nb, j, p)
        else:
            h = _mla_layer(h, mod, rows, nb, j, cs, p)
        rows = all_rows if ctx_out else lat_rows
        h = _ffn(h, mod, ffn_in[i, 1].astype(BF16), ffn_out[i, 1].astype(BF16), rows=rows, slot=6,
                 final_w=final_norm_w if last else None)
    return h[:lat_rows].reshape(nb, n_lat, D_MODEL)
```

```python
import functools
import math

import jax
import jax.numpy as jnp
from jax import lax
from jax.experimental import pallas as pl
from jax.experimental.pallas import tpu as pltpu

F32 = jnp.float32
BF16 = jnp.bfloat16
HIGHEST = lax.Precision.HIGHEST

D_MODEL = 1024
DEPTH = 4
GRID_W = 64
CTX_LEN = 256
N_MIXERS = 3
N_MOD = 9
FFN_DIM = 2816
EPS = 1e-6

HYENA_EMB = 33
HYENA_BANDS = (HYENA_EMB - 1) // 2
HYENA_FILTER_HIDDEN = 64
HYENA_FAST_DECAY = 0.3
HYENA_SLOW_DECAY = 1.5
HYENA_DECAY_TARGET = 1e-2

SSM_INNER = 2 * D_MODEL
SSM_HEAD_DIM = 64
SSM_HEADS = SSM_INNER // SSM_HEAD_DIM
SSM_GROUPS = 8
HEADS_PER_GROUP = SSM_HEADS // SSM_GROUPS
SSM_STATE = 128
SSM_CHUNK = 128
SSM_BC_DIM = SSM_GROUPS * SSM_STATE
SSM_CONV_DIM = SSM_INNER + 2 * SSM_BC_DIM
SSM_IN_DIM = SSM_INNER + SSM_CONV_DIM + 2 * SSM_HEADS
SSM_IN_PAD = 6400

MLA_HEADS = 16
MLA_NOPE = 64
MLA_ROPE = 32
MLA_V = 64
MLA_Q_RANK = 768
MLA_KV_RANK = 256
MLA_QK = MLA_NOPE + MLA_ROPE
MLA_SCALE = MLA_QK ** -0.5
MLA_DOWN_PAD = 1152
ROPE_AXIS = MLA_ROPE // 2
ROPE_BASE = 10000.0

MOD_ROWS = 2048
VMEM_LIMIT_BYTES = 56 * 1024 * 1024


def _params(*sem):
    return pltpu.CompilerParams(dimension_semantics=sem, vmem_limit_bytes=VMEM_LIMIT_BYTES)


def _rms(x):
    return x * lax.rsqrt(jnp.mean(x * x, axis=-1, keepdims=True) + EPS)


def _silu(x):
    return x * jax.nn.sigmoid(x)


def _dot(a, b, **kw):
    return jnp.dot(a, b, preferred_element_type=F32, **kw)


def _dot_nt(a, b):
    return lax.dot_general(a, b, (((1,), (1,)), ((), ())), preferred_element_type=F32)


def _mod_kernel(x_ref, w_ref, b_ref, o_ref):
    a = _silu(x_ref[...]).astype(BF16)
    o_ref[...] = _dot(a, w_ref[...].astype(BF16)) + b_ref[...]


def _mod_all(cc, ada_w, ada_b):
    n = N_MOD * D_MODEL
    tn = 1024
    return pl.pallas_call(
        _mod_kernel,
        grid=(DEPTH, n // tn),
        in_specs=[pl.BlockSpec((16, D_MODEL), lambda l, j: (0, 0)),
                  pl.BlockSpec((None, D_MODEL, tn), lambda l, j: (l, 0, j)),
                  pl.BlockSpec((None, 1, tn), lambda l, j: (l, 0, j))],
        out_specs=pl.BlockSpec((None, 16, tn), lambda l, j: (l, 0, j)),
        out_shape=jax.ShapeDtypeStruct((DEPTH, 16, n), F32),
        compiler_params=_params("parallel", "parallel"),
        name="adaln_mod",
    )(cc, ada_w, ada_b.reshape(DEPTH, 1, n))


def _mm_kernel(*refs, pro, epi, has_bias, slot, gslot):
    it = iter(refs)
    x_ref, w_ref = next(it), next(it)
    mod_ref = next(it) if (pro == "adaln" or epi == "gres") else None
    nw_ref = next(it) if pro == "rms" else None
    b_ref = next(it) if has_bias else None
    res_ref = next(it) if epi == "gres" else None
    o_ref, xa_ref = next(it), next(it)

    @pl.when(pl.program_id(1) == 0)
    def _():
        x = x_ref[...].astype(F32)
        if pro == "adaln":
            x = _rms(x) * (1.0 + mod_ref[0, slot + 1:slot + 2, :]) + mod_ref[0, slot:slot + 1, :]
        elif pro == "rms":
            x = _rms(x) * nw_ref[...]
        xa_ref[...] = x.astype(BF16)

    acc = _dot(xa_ref[...], w_ref[...])
    if has_bias:
        acc = acc + b_ref[...]
    if epi == "gres":
        acc = res_ref[...] + mod_ref[0, gslot:gslot + 1, :] * acc
    o_ref[...] = acc.astype(o_ref.dtype)


def _mm(x, w, *, rows, tn, tm=1024, x_colblock=0, pro="none", epi="none", mod=None, slot=0, gslot=0,
        nw=None, bias=None, res=None, out_dtype=F32, name="mm"):
    k, n = w.shape
    assert rows % tm == 0 and n % tn == 0 and MOD_ROWS % tm == 0
    if epi == "gres":
        assert tn == n == D_MODEL
    per = MOD_ROWS // tm
    ins = [x, w]
    specs = [pl.BlockSpec((tm, k), lambda i, j: (i, x_colblock)),
             pl.BlockSpec((k, tn), lambda i, j: (0, j))]
    if pro == "adaln" or epi == "gres":
        ins.append(mod)
        specs.append(pl.BlockSpec((1, N_MOD, D_MODEL), lambda i, j: (i // per, 0, 0)))
    if pro == "rms":
        ins.append(nw.reshape(1, k))
        specs.append(pl.BlockSpec((1, k), lambda i, j: (0, 0)))
    if bias is not None:
        ins.append(bias.reshape(1, n))
        specs.append(pl.BlockSpec((1, tn), lambda i, j: (0, j)))
    if epi == "gres":
        ins.append(res)
        specs.append(pl.BlockSpec((tm, tn), lambda i, j: (i, j)))
    return pl.pallas_call(
        functools.partial(_mm_kernel, pro=pro, epi=epi, has_bias=bias is not None, slot=slot, gslot=gslot),
        grid=(rows // tm, n // tn),
        in_specs=specs,
        out_specs=pl.BlockSpec((tm, tn), lambda i, j: (i, j)),
        out_shape=jax.ShapeDtypeStruct((rows, n), out_dtype),
        scratch_shapes=[pltpu.VMEM((tm, k), BF16)],
        compiler_params=_params("parallel", "arbitrary"),
        name=name,
    )(*ins)


def _ffn_kernel(*refs, slot, final):
    if final:
        x_ref, mod_ref, wg_ref, wu_ref, wo_ref, fw_ref, o_ref, xa_ref, acc_ref = refs
    else:
        x_ref, mod_ref, wg_ref, wu_ref, wo_ref, o_ref, xa_ref, acc_ref = refs
    j = pl.program_id(1)

    @pl.when(j == 0)
    def _():
        x = x_ref[...]
        a = _rms(x) * (1.0 + mod_ref[0, slot + 1:slot + 2, :]) + mod_ref[0, slot:slot + 1, :]
        xa_ref[...] = a.astype(BF16)
        acc_ref[...] = jnp.zeros_like(acc_ref)

    xa = xa_ref[...]
    g = _dot(xa, wg_ref[...])
    u = _dot(xa, wu_ref[...])
    acc_ref[...] += _dot((_silu(g) * u).astype(BF16), wo_ref[...])

    @pl.when(j == pl.num_programs(1) - 1)
    def _():
        out = x_ref[...] + (0.5 * mod_ref[0, slot + 2:slot + 3, :]) * acc_ref[...]
        if final:
            out = _rms(out) * fw_ref[...]
        o_ref[...] = out


def _ffn(h, mod, w_in, w_out, *, rows, slot, final_w=None, tm=1024, tf=256):
    nf = FFN_DIM // tf
    per = MOD_ROWS // tm
    final = final_w is not None
    ins = [h, mod, w_in, w_in, w_out]
    specs = [pl.BlockSpec((tm, D_MODEL), lambda i, j: (i, 0)),
             pl.BlockSpec((1, N_MOD, D_MODEL), lambda i, j: (i // per, 0, 0)),
             pl.BlockSpec((D_MODEL, tf), lambda i, j: (0, j)),
             pl.BlockSpec((D_MODEL, tf), lambda i, j: (0, j + nf)),
             pl.BlockSpec((tf, D_MODEL), lambda i, j: (j, 0))]
    if final:
        ins.append(final_w.reshape(1, D_MODEL))
        specs.append(pl.BlockSpec((1, D_MODEL), lambda i, j: (0, 0)))
    return pl.pallas_call(
        functools.partial(_ffn_kernel, slot=slot, final=final),
        grid=(rows // tm, nf),
        in_specs=specs,
        out_specs=pl.BlockSpec((tm, D_MODEL), lambda i, j: (i, 0)),
        out_shape=jax.ShapeDtypeStruct((rows, D_MODEL), F32),
        scratch_shapes=[pltpu.VMEM((tm, D_MODEL), BF16), pltpu.VMEM((tm, D_MODEL), F32)],
        compiler_params=_params("parallel", "arbitrary"),
        name="ffn",
    )(*ins)


def _dwconv_kernel(x_ref, w_ref, b_ref, o_ref, *, ctx_block, act):
    rows = x_ref.shape[0]
    x = x_ref[...]
    seg = jnp.where(pl.program_id(0) == ctx_block, CTX_LEN, rows)
    pos = lax.broadcasted_iota(jnp.int32, x.shape, 0) & (seg - 1)
    prev = jnp.where(pos == 0, 0.0, pltpu.roll(x, 1, 0))
    nxt = jnp.where(pos == seg - 1, 0.0, pltpu.roll(x, rows - 1, 0))
    y = w_ref[0:1, :] * prev + w_ref[1:2, :] * x + w_ref[2:3, :] * nxt + b_ref[...]
    if act:
        y = _silu(y)
    o_ref[...] = y.astype(o_ref.dtype)


def _dwconv(x, w, b, *, rows, col0, act, tc=512, name="dwconv"):
    c = w.shape[1]
    cb0 = col0 // tc
    return pl.pallas_call(
        functools.partial(_dwconv_kernel, ctx_block=8, act=act),
        grid=(rows // MOD_ROWS, c // tc),
        in_specs=[pl.BlockSpec((MOD_ROWS, tc), lambda i, j: (i, cb0 + j)),
                  pl.BlockSpec((3, tc), lambda i, j: (0, j)),
                  pl.BlockSpec((1, tc), lambda i, j: (0, j))],
        out_specs=pl.BlockSpec((MOD_ROWS, tc), lambda i, j: (i, j)),
        out_shape=jax.ShapeDtypeStruct((rows, c), F32),
        compiler_params=_params("parallel", "parallel"),
        name=name,
    )(x, w, b.reshape(1, c))


def _filter_kernel(z_ref, w1_ref, b1_ref, fr_ref, w2_ref, b2_ref, w3_ref, dl_ref, sum_ref, diff_ref):
    tl = z_ref.shape[0]
    z = z_ref[...]
    fr = fr_ref[...]
    hid = jnp.sin(fr * (_dot(z, w1_ref[...], precision=HIGHEST) + b1_ref[...]))
    hid = jnp.sin(fr * (_dot(hid, w2_ref[...], precision=HIGHEST) + b2_ref[...]))
    h = _dot(hid, w3_ref[...], precision=HIGHEST)
    dec = jnp.exp(-z[:, 0:1] * dl_ref[...])
    row = lax.broadcasted_iota(jnp.int32, dec.shape, 0) + pl.program_id(0) * tl
    for o in range(2):
        hf = h[:, (2 * o) * D_MODEL:(2 * o + 1) * D_MODEL] * dec
        hb = jnp.where(row == 0, 0.0, h[:, (2 * o + 1) * D_MODEL:(2 * o + 2) * D_MODEL] * dec)
        sum_ref[:, o * D_MODEL:(o + 1) * D_MODEL] = (hf + hb).astype(BF16)
        diff_ref[:, o * D_MODEL:(o + 1) * D_MODEL] = (hb - hf).astype(BF16)


def _hyena_filters(L, w1, b1, freq, w2, b2, w3):
    pos = jnp.arange(L, dtype=F32)
    t = jnp.linspace(0.0, 1.0, L, dtype=F32)
    bands = jnp.linspace(1e-4, HYENA_BANDS - 1, HYENA_BANDS, dtype=F32)
    ang = (2.0 * math.pi / L) * pos[:, None] * bands[None, :]
    z = jnp.concatenate([t[:, None], jnp.cos(ang), -jnp.sin(ang)], axis=-1)
    hd = 128
    ph = hd - HYENA_FILTER_HIDDEN
    z = jnp.pad(z, ((0, 0), (0, hd - HYENA_EMB)))
    w1 = jnp.pad(w1, ((0, hd - HYENA_EMB), (0, ph)))
    w2 = jnp.pad(w2, ((0, ph), (0, ph)))
    w3 = jnp.pad(w3, ((0, ph), (0, 0)))
    b1, b2, freq = (jnp.pad(t, (0, ph)) for t in (b1, b2, freq))
    max_decay = math.log(HYENA_DECAY_TARGET) / HYENA_FAST_DECAY
    min_decay = math.log(HYENA_DECAY_TARGET) / HYENA_SLOW_DECAY
    deltas = jnp.abs(jnp.linspace(min_decay, max_decay, D_MODEL, dtype=F32)).reshape(1, D_MODEL)
    tl = min(L, 512)
    full = lambda shape: pl.BlockSpec(shape, lambda i: (0, 0))
    return pl.pallas_call(
        _filter_kernel,
        grid=(L // tl,),
        in_specs=[pl.BlockSpec((tl, hd), lambda i: (i, 0)), full((hd, hd)), full((1, hd)), full((1, hd)),
                  full((hd, hd)), full((1, hd)), full((hd, 4 * D_MODEL)), full((1, D_MODEL))],
        out_specs=[pl.BlockSpec((tl, 2 * D_MODEL), lambda i: (i, 0))] * 2,
        out_shape=[jax.ShapeDtypeStruct((L, 2 * D_MODEL), BF16)] * 2,
        compiler_params=_params("parallel"),
        name="hyena_filter",
    )(z, w1, b1.reshape(1, hd), freq.reshape(1, hd), w2, b2.reshape(1, hd), w3, deltas)


def _lmm_kernel(*refs, epi, tf, scale):
    it = iter(refs)
    a_ref, x_ref = next(it), next(it)
    if epi == "spec":
        k_ref = next(it)
    elif epi == "gate":
        g_ref, v_ref, bias_ref = next(it), next(it), next(it)
    o_ref, xb_ref = next(it), next(it)

    @pl.when(pl.program_id(2) == 0)
    def _():
        xb_ref[...] = x_ref[...].astype(BF16)

    acc = _dot(a_ref[...], xb_ref[...])
    if epi == "spec":
        xr, xs = acc[:tf], acc[tf:]
        kr, ki = k_ref[0:tf, :], k_ref[tf:2 * tf, :]
        o_ref[0:tf, :] = ((xr * kr + xs * ki) * scale).astype(o_ref.dtype)
        o_ref[tf:2 * tf, :] = ((xs * kr - xr * ki) * scale).astype(o_ref.dtype)
    elif epi == "gate":
        o_ref[...] = (g_ref[...] * (acc + v_ref[...].astype(F32) * bias_ref[...])).astype(o_ref.dtype)
    else:
        o_ref[...] = acc.astype(o_ref.dtype)


def _lmm(a, x, *, nb, ncb, tm, n, x_row0=0, x_cb0=0, epi="none", tf=0, scale=1.0, kspec=None, k_cb=0,
         g=None, g_cb=0, g_row0=0, v=None, v_cb=0, v_row0=0, bias=None, out_dtype=F32, name="lmm"):
    mo, k = a.shape
    nm = mo // tm
    xr0 = x_row0 // k
    ins = [a, x]
    specs = [pl.BlockSpec((tm, k), lambda b, c, m: (m, 0)),
             pl.BlockSpec((k, n), lambda b, c, m: (xr0 + b, x_cb0 + c))]
    if epi == "spec":
        ins.append(kspec)
        specs.append(pl.BlockSpec((tm, n), lambda b, c, m: (m, k_cb)))
    elif epi == "gate":
        gr0, vr0 = g_row0 // tm, v_row0 // tm
        ins += [g, v, bias.reshape(1, n)]
        specs += [pl.BlockSpec((tm, n), lambda b, c, m: (gr0 + b * nm + m, g_cb)),
                  pl.BlockSpec((tm, n), lambda b, c, m: (vr0 + b * nm + m, v_cb)),
                  pl.BlockSpec((1, n), lambda b, c, m: (0, 0))]
    return pl.pallas_call(
        functools.partial(_lmm_kernel, epi=epi, tf=tf, scale=scale),
        grid=(nb, ncb, nm),
        in_specs=specs,
        out_specs=pl.BlockSpec((tm, n), lambda b, c, m: (b * nm + m, c)),
        out_shape=jax.ShapeDtypeStruct((nb * mo, ncb * n), out_dtype),
        scratch_shapes=[pltpu.VMEM((k, n), BF16)],
        compiler_params=_params("parallel", "parallel", "arbitrary"),
        name=name,
    )(*ins)


def _dft_mats(L, tf):
    idx = jnp.arange(L, dtype=jnp.int32)
    ph = ((2 * idx[:, None] + 1) * idx[None, :]) % (4 * L)
    ang = ph.astype(F32) * (2.0 * math.pi / (4 * L))
    c, s = jnp.cos(ang), jnp.sin(ang)
    nf = L // tf
    fwd = jnp.stack([c.reshape(nf, tf, L), s.reshape(nf, tf, L)], axis=1).reshape(2 * L, L)
    inv = jnp.stack([c.T.reshape(L, nf, tf), s.T.reshape(L, nf, tf)], axis=2).reshape(L, 2 * L)
    return c.astype(BF16), s.astype(BF16), fwd.astype(BF16), inv.astype(BF16)


def _interleave(kr, ki, tf):
    L, n = kr.shape
    return jnp.stack([kr.reshape(L // tf, tf, n), ki.reshape(L // tf, tf, n)], axis=1).reshape(2 * L, n)


def _hyena_core(u, L, nb, row0, fparams, hy_bias, name):
    tf = min(L, 256)
    tmi = min(L, 256)
    cmat, smat, fwd, inv = _dft_mats(L, tf)
    hsum, hdiff = _hyena_filters(L, *fparams)
    kr = _lmm(cmat, hsum, nb=1, ncb=2, tm=tf, n=D_MODEL, name=name + "_kr")
    ki = _lmm(smat, hdiff, nb=1, ncb=2, tm=tf, n=D_MODEL, name=name + "_ki")
    kspec = _interleave(kr, ki, tf)
    scale = 1.0 / L
    y1 = _lmm(fwd, u, nb=nb, ncb=1, tm=2 * tf, n=D_MODEL, x_row0=row0, x_cb0=2, epi="spec", tf=tf, scale=scale,
              kspec=kspec, k_cb=0, out_dtype=BF16, name=name + "_fwd1")
    z = _lmm(inv, y1, nb=nb, ncb=1, tm=tmi, n=D_MODEL, epi="gate", g=u, g_cb=0, g_row0=row0, v=u, v_cb=2,
             v_row0=row0, bias=hy_bias[0], out_dtype=F32, name=name + "_inv1")
    y2 = _lmm(fwd, z, nb=nb, ncb=1, tm=2 * tf, n=D_MODEL, epi="spec", tf=tf, scale=scale,
              kspec=kspec, k_cb=1, out_dtype=BF16, name=name + "_fwd2")
    return _lmm(inv, y2, nb=nb, ncb=1, tm=tmi, n=D_MODEL, epi="gate", g=u, g_cb=1, g_row0=row0, v=z, v_cb=0,
                bias=hy_bias[1], out_dtype=BF16, name=name + "_inv2")


def _softplus(x):
    return jnp.maximum(x, 0.0) + jnp.log(1.0 + jnp.exp(-jnp.abs(x)))


def _ssd_kernel(xs_ref, b_ref, c_ref, dt_ref, dtt_ref, dtb_ref, dtbt_ref, al_ref, alt_ref, y_ref, st_ref):
    T = SSM_CHUNK
    N = SSM_STATE
    d = pl.program_id(1)

    @pl.when(pl.program_id(2) == 0)
    def _():
        st_ref[...] = jnp.zeros_like(st_ref)

    sgn = jnp.where(d == 0, 1, -1)
    ri = lax.broadcasted_iota(jnp.int32, (T, T), 0)
    ci = lax.broadcasted_iota(jnp.int32, (T, T), 1)
    mask = (ri - ci) * sgn >= 0
    tri_col = mask.astype(F32)
    tri_row = ((ci - ri) * sgn >= 0).astype(F32)
    low = lax.broadcasted_iota(jnp.int32, (T, 2 * SSM_HEAD_DIM), 1) < SSM_HEAD_DIM

    a = _softplus(dt_ref[...] + dtb_ref[...]) * -jnp.exp(al_ref[...])
    dtt = _softplus(dtt_ref[...] + dtbt_ref[...])
    at = dtt * -jnp.exp(alt_ref[...])
    cs_col = _dot(tri_col, a, precision=HIGHEST)
    cs_row = _dot(at, tri_row, precision=HIGHEST)
    tot = jnp.sum(at, axis=1, keepdims=True)
    row_dt = cs_row - jnp.log(dtt)
    w_out = dtt * jnp.exp(tot - cs_row)
    e_tot = jnp.exp(tot)

    for g in range(SSM_GROUPS):
        bt = jnp.transpose(b_ref[:, g * N:(g + 1) * N])
        cg = c_ref[:, g * N:(g + 1) * N]
        cb = _dot(cg.astype(BF16), bt.astype(BF16))
        for k in range(HEADS_PER_GROUP // 2):
            h0 = g * HEADS_PER_GROUP + 2 * k
            lanes = slice(h0 * SSM_HEAD_DIM, (h0 + 2) * SSM_HEAD_DIM)
            xs = xs_ref[:, lanes].astype(BF16)
            st = st_ref[g, :, k * 128:(k + 1) * 128]
            rhs = jnp.concatenate([xs, st.astype(BF16)], axis=0)
            ys, upds = [], []
            for h in (h0, h0 + 1):
                col = jnp.broadcast_to(cs_col[:, h:h + 1], (T, T))
                m = jnp.where(mask, jnp.exp(col - row_dt[h:h + 1, :]), 0.0) * cb
                lhs = jnp.concatenate([m, jnp.exp(col) * cg], axis=1).astype(BF16)
                ys.append(_dot(lhs, rhs))
                upds.append(_dot((bt * w_out[h:h + 1, :]).astype(BF16), xs))
            y_ref[:, lanes] = jnp.where(low, ys[0], ys[1])
            decay = jnp.where(low, e_tot[h0:h0 + 1, :], e_tot[h0 + 1:h0 + 2, :])
            st_ref[g, :, k * 128:(k + 1) * 128] = st * decay + jnp.where(low, upds[0], upds[1])


def _ssd(xbc, dta, dtt, dt_bias, a_log, *, nb):
    m = xbc.shape[0]
    T = SSM_CHUNK
    H = SSM_HEADS
    lat_c = MOD_ROWS // T
    ctx_c = CTX_LEN // T
    ctx0 = nb * lat_c
    steps = lat_c + ctx_c

    def rb(b, d, t):
        ctx = ctx0 + ctx_c * b + jnp.where(d == 0, t, ctx_c - 1 - t)
        lat = lat_c * b + jnp.where(d == 0, t - ctx_c, steps - 1 - t)
        return jnp.where(t < ctx_c, ctx, lat)

    return pl.pallas_call(
        _ssd_kernel,
        grid=(nb, 2, steps),
        in_specs=[pl.BlockSpec((T, SSM_INNER), lambda b, d, t: (rb(b, d, t), 0)),
                  pl.BlockSpec((T, SSM_BC_DIM), lambda b, d, t: (rb(b, d, t), 2)),
                  pl.BlockSpec((T, SSM_BC_DIM), lambda b, d, t: (rb(b, d, t), 3)),
                  pl.BlockSpec((None, T, H), lambda b, d, t: (d, rb(b, d, t), 0)),
                  pl.BlockSpec((None, H, T), lambda b, d, t: (d, 0, rb(b, d, t))),
                  pl.BlockSpec((None, 1, H), lambda b, d, t: (d, 0, 0)),
                  pl.BlockSpec((None, H, 1), lambda b, d, t: (d, 0, 0)),
                  pl.BlockSpec((None, 1, H), lambda b, d, t: (d, 0, 0)),
                  pl.BlockSpec((None, H, 1), lambda b, d, t: (d, 0, 0))],
        out_specs=pl.BlockSpec((None, T, SSM_INNER), lambda b, d, t: (d, rb(b, d, t), 0)),
        out_shape=jax.ShapeDtypeStruct((2, m, SSM_INNER), F32),
        scratch_shapes=[pltpu.VMEM((SSM_GROUPS, SSM_STATE, HEADS_PER_GROUP * SSM_HEAD_DIM), F32)],
        compiler_params=_params("parallel", "parallel", "arbitrary"),
        name="ssd_scan",
    )(xbc, xbc, xbc, dta, dtt, dt_bias.reshape(2, 1, H), dt_bias.reshape(2, H, 1),
      a_log.reshape(2, 1, H), a_log.reshape(2, H, 1))


def _ssd_finish_kernel(yf_ref, yb_ref, xs_ref, z_ref, dsk_ref, nw_ref, o_ref):
    gw = SSM_INNER // SSM_GROUPS
    z = z_ref[...]
    y = (yf_ref[...] + yb_ref[...] + dsk_ref[...] * xs_ref[...]) * _silu(z)
    for g in range(SSM_GROUPS):
        yg = _rms(y[:, g * gw:(g + 1) * gw])
        o_ref[:, g * gw:(g + 1) * gw] = (yg * nw_ref[:, g * gw:(g + 1) * gw]).astype(o_ref.dtype)


def _ssd_finish(y2, xbc, zx, d_skip, norm_w, *, rows, tm=256):
    n = SSM_INNER
    return pl.pallas_call(
        _ssd_finish_kernel,
        grid=(rows // tm,),
        in_specs=[pl.BlockSpec((None, tm, n), lambda i: (0, i, 0)),
                  pl.BlockSpec((None, tm, n), lambda i: (1, i, 0)),
                  pl.BlockSpec((tm, n), lambda i: (i, 0)),
                  pl.BlockSpec((tm, n), lambda i: (i, 0)),
                  pl.BlockSpec((1, n), lambda i: (0, 0)),
                  pl.BlockSpec((1, n), lambda i: (0, 0))],
        out_specs=pl.BlockSpec((tm, n), lambda i: (i, 0)),
        out_shape=jax.ShapeDtypeStruct((rows, n), BF16),
        compiler_params=_params("parallel"),
        name="ssd_finish",
    )(y2, y2, xbc, zx, jnp.repeat(d_skip, SSM_HEAD_DIM).reshape(1, n), norm_w.reshape(1, n))


HEAD_W = 2 * MLA_NOPE


def _attn_kernel(*refs, has_lat):
    if has_lat:
        q_ref, kvc_ref, krc_ref, kv_ref, kr_ref, tq_ref, tk_ref, o_ref, kc_scr, vc_scr, k_scr, v_scr = refs
    else:
        q_ref, kvc_ref, krc_ref, o_ref, kc_scr, vc_scr = refs
    R = MLA_ROPE
    kscale = MLA_SCALE * math.log2(math.e)

    def build_keys():
        krc = krc_ref[...].astype(F32)
        lane_c = lax.broadcasted_iota(jnp.int32, krc.shape, 1)
        pe_c = jnp.where((lane_c >= MLA_NOPE) & (lane_c < MLA_NOPE + R), pltpu.roll(krc, MLA_NOPE, 1), 0.0)
        for hh in range(2):
            kvh = kvc_ref[:, hh * HEAD_W:(hh + 1) * HEAD_W].astype(F32)
            kc_scr[hh] = (jnp.where(lane_c < MLA_NOPE, kvh, pe_c) * kscale).astype(BF16)
            vc_scr[hh] = jnp.where(lane_c < MLA_NOPE, 1.0, kvh).astype(BF16)
        if has_lat:
            t = kr_ref[...].astype(F32) * tk_ref[...]
            lane = lax.broadcasted_iota(jnp.int32, t.shape, 1)
            krot = jnp.where(lane < R, t + pltpu.roll(t, HEAD_W - R, 1), 0.0)
            krr = pltpu.roll(krot, MLA_NOPE, 1) + pltpu.roll(krot, MLA_NOPE + R, 1)
            for hh in range(2):
                kvh = kv_ref[:, hh * HEAD_W:(hh + 1) * HEAD_W].astype(F32)
                k_scr[hh] = (jnp.where(lane < MLA_NOPE, kvh, krr) * kscale).astype(BF16)
                v_scr[hh] = jnp.where(lane < MLA_NOPE, 1.0, kvh).astype(BF16)

    if has_lat:
        pl.when(pl.program_id(2) == 0)(build_keys)
    else:
        build_keys()

    res = []
    for hh in range(2):
        q = q_ref[:, hh * HEAD_W:(hh + 1) * HEAD_W]
        s_c = _dot_nt(q, kc_scr[hh])
        mx = jnp.max(s_c, axis=-1, keepdims=True)
        if has_lat:
            ql = (q.astype(F32) * tq_ref[...]).astype(BF16)
            s_l = _dot_nt(ql, k_scr[hh])
            mx = jnp.maximum(mx, jnp.max(s_l, axis=-1, keepdims=True))
            acc = _dot(jnp.exp2(s_l - mx).astype(BF16), v_scr[hh])
            acc = acc + _dot(jnp.exp2(s_c - mx).astype(BF16), vc_scr[hh])
        else:
            acc = _dot(jnp.exp2(s_c - mx).astype(BF16), vc_scr[hh])
        res.append(acc / pltpu.roll(acc, MLA_V, 1))
    lane_o = lax.broadcasted_iota(jnp.int32, res[0].shape, 1)
    o_ref[...] = jnp.where(lane_o < MLA_V, pltpu.roll(res[0], MLA_V, 1), res[1]).astype(o_ref.dtype)


def _attention(q, kv, dn, cs, *, nb, tq=512):
    L = MOD_ROWS
    hp = MLA_HEADS // 2
    w = 2 * HEAD_W
    nq = L // tq
    cb = nb * (L // CTX_LEN)
    kr_cb = (MLA_Q_RANK + MLA_KV_RANK) // HEAD_W
    tab_q = jnp.concatenate([jnp.ones((L, MLA_NOPE), F32), cs], axis=1)
    tab_k = jnp.concatenate([cs, jnp.zeros((L, HEAD_W - 2 * MLA_ROPE), F32)], axis=1)
    lat = pl.pallas_call(
        functools.partial(_attn_kernel, has_lat=True),
        grid=(nb, hp, nq),
        in_specs=[pl.BlockSpec((tq, w), lambda b, p, i: (b * nq + i, p)),
                  pl.BlockSpec((CTX_LEN, w), lambda b, p, i: (cb + b, p)),
                  pl.BlockSpec((CTX_LEN, HEAD_W), lambda b, p, i: (cb + b, kr_cb)),
                  pl.BlockSpec((L, w), lambda b, p, i: (b, p)),
                  pl.BlockSpec((L, HEAD_W), lambda b, p, i: (b, kr_cb)),
                  pl.BlockSpec((tq, HEAD_W), lambda b, p, i: (i, 0)),
                  pl.BlockSpec((L, HEAD_W), lambda b, p, i: (0, 0))],
        out_specs=pl.BlockSpec((tq, 2 * MLA_V), lambda b, p, i: (b * nq + i, p)),
        out_shape=jax.ShapeDtypeStruct((nb * L, MLA_HEADS * MLA_V), BF16),
        scratch_shapes=[pltpu.VMEM((2, CTX_LEN, HEAD_W), BF16)] * 2 + [pltpu.VMEM((2, L, HEAD_W), BF16)] * 2,
        compiler_params=_params("parallel", "parallel", "arbitrary"),
        name="mla_attn",
    )(q, kv, dn, kv, dn, tab_q, tab_k)
    ctx = pl.pallas_call(
        functools.partial(_attn_kernel, has_lat=False),
        grid=(nb, hp),
        in_specs=[pl.BlockSpec((CTX_LEN, w), lambda b, p: (cb + b, p)),
                  pl.BlockSpec((CTX_LEN, w), lambda b, p: (cb + b, p)),
                  pl.BlockSpec((CTX_LEN, HEAD_W), lambda b, p: (cb + b, kr_cb))],
        out_specs=pl.BlockSpec((CTX_LEN, 2 * MLA_V), lambda b, p: (b, p)),
        out_shape=jax.ShapeDtypeStruct((nb * CTX_LEN, MLA_HEADS * MLA_V), BF16),
        scratch_shapes=[pltpu.VMEM((2, CTX_LEN, HEAD_W), BF16)] * 2,
        compiler_params=_params("parallel", "parallel"),
        name="mla_attn_ctx",
    )(q, kv, dn)
    return jnp.concatenate([lat, ctx], axis=0)


def _rot_cols(w):
    wp = w.reshape(w.shape[:-1] + (2, 2, ROPE_AXIS // 2))
    return jnp.stack([-wp[..., 1, :], wp[..., 0, :]], axis=-2).reshape(w.shape)


def _rope_table(n_lat):
    rows = n_lat // GRID_W
    row = jnp.repeat(jnp.arange(rows), GRID_W)
    col = jnp.tile(jnp.arange(GRID_W), rows)
    inv = 1.0 / (ROPE_BASE ** (jnp.arange(0, ROPE_AXIS, 2, dtype=F32) / ROPE_AXIS))
    ar = row.astype(F32)[:, None] * inv[None, :]
    ac = col.astype(F32)[:, None] * inv[None, :]
    ang = jnp.concatenate([ar, ar, ac, ac], axis=-1)
    return jnp.concatenate([jnp.cos(ang), jnp.sin(ang)], axis=-1)


def _hyena_layer(h, mod, rows, nb, j, with_ctx, p):
    u0 = _mm(h, p["hy_in_w"][j].astype(BF16), rows=rows, tn=1024, pro="adaln", mod=mod, slot=3,
             bias=p["hy_in_b"][j], name="hyena_in")
    u = _dwconv(u0, p["hy_conv_w"][j], p["hy_conv_b"][j], rows=rows, col0=0, act=False, name="hyena_conv")
    fparams = (p["hy_pos_w1"][j], p["hy_pos_b1"][j], p["hy_freq"][j], p["hy_pos_w2"][j], p["hy_pos_b2"][j],
               p["hy_pos_w3"][j])
    y = _hyena_core(u, MOD_ROWS, nb, 0, fparams, p["hy_bias"][j], "hy_lat")
    if with_ctx:
        yc = _hyena_core(u, CTX_LEN, nb, nb * MOD_ROWS, fparams, p["hy_bias"][j], "hy_ctx")
        y = jnp.concatenate([y, yc], axis=0)
    return _mm(y, p["hy_out_w"][j].astype(BF16), rows=rows, tn=D_MODEL, epi="gres", mod=mod, gslot=5,
               bias=p["hy_out_b"][j], res=h, name="hyena_out")


def _mamba_layer(h, mod, rows, nb, j, p):
    w_in = jnp.pad(p["mb_in_w"][j].astype(BF16), ((0, 0), (0, SSM_IN_PAD - SSM_IN_DIM)))
    zx = _mm(h, w_in, rows=rows, tn=SSM_IN_PAD // 5, pro="adaln", mod=mod, slot=3, name="mamba_in")
    xbc = _dwconv(zx, p["mb_conv_w"][j], p["mb_conv_b"][j], rows=rows, col0=SSM_INNER, act=True, name="mamba_conv")
    dtr = zx[:, SSM_INNER + SSM_CONV_DIM:SSM_IN_DIM].reshape(rows, 2, SSM_HEADS)
    dta = jnp.transpose(dtr, (1, 0, 2))
    dtt = jnp.transpose(dtr, (1, 2, 0))
    y2 = _ssd(xbc, dta, dtt, p["mb_dt_bias"][j], p["mb_A_log"][j], nb=nb)
    yn = _ssd_finish(y2, xbc, zx, p["mb_D"][j], p["mb_norm_w"][j], rows=rows)
    return _mm(yn, p["mb_out_w"][j].astype(BF16), rows=rows, tn=D_MODEL, epi="gres", mod=mod, gslot=5, res=h,
               name="mamba_out")


def _mla_layer(h, mod, rows, nb, j, cs, p):
    wd = p["mla_w_down"][j]
    kpe_w = wd[:, MLA_Q_RANK + MLA_KV_RANK:]
    wd = jnp.concatenate([wd, _rot_cols(kpe_w),
                          jnp.zeros((D_MODEL, MLA_DOWN_PAD - wd.shape[1] - MLA_ROPE), F32)], axis=1).astype(BF16)
    wq = p["mla_w_uq"][j].reshape(MLA_Q_RANK, MLA_HEADS, MLA_QK)
    wq = jnp.concatenate([wq, _rot_cols(wq[..., MLA_NOPE:])], axis=-1).reshape(MLA_Q_RANK, -1).astype(BF16)
    dn = _mm(h, wd, rows=rows, tn=MLA_DOWN_PAD, pro="adaln", mod=mod, slot=3, out_dtype=F32, name="mla_down")
    q = _mm(dn, wq, rows=rows, tn=1024, pro="rms", nw=p["mla_q_norm"][j], out_dtype=BF16, name="mla_uq")
    kv = _mm(dn, p["mla_w_ukv"][j].astype(BF16), rows=rows, tn=1024, x_colblock=MLA_Q_RANK // MLA_KV_RANK,
             pro="rms", nw=p["mla_kv_norm"][j], out_dtype=BF16, name="mla_ukv")
    o = _attention(q, kv, dn, cs, nb=nb)
    return _mm(o, p["mla_w_o"][j].astype(BF16), rows=rows, tn=D_MODEL, epi="gres", mod=mod, gslot=5, res=h,
               name="mla_out")


def kernel(x, c, ctx, c_ctx, ada_w, ada_b, ffn_in, ffn_out, hy_in_w, hy_in_b, hy_conv_w, hy_conv_b, hy_pos_w1, hy_pos_b1, hy_freq, hy_pos_w2, hy_pos_b2, hy_pos_w3, hy_bias, hy_out_w, hy_out_b, mb_in_w, mb_conv_w, mb_conv_b, mb_dt_bias, mb_A_log, mb_D, mb_norm_w, mb_out_w, mla_w_down, mla_q_norm, mla_w_uq, mla_kv_norm, mla_w_ukv, mla_w_o, final_norm_w):
    p = dict(hy_in_w=hy_in_w, hy_in_b=hy_in_b, hy_conv_w=hy_conv_w, hy_conv_b=hy_conv_b, hy_pos_w1=hy_pos_w1,
             hy_pos_b1=hy_pos_b1, hy_freq=hy_freq, hy_pos_w2=hy_pos_w2, hy_pos_b2=hy_pos_b2, hy_pos_w3=hy_pos_w3,
             hy_bias=hy_bias, hy_out_w=hy_out_w, hy_out_b=hy_out_b, mb_in_w=mb_in_w, mb_conv_w=mb_conv_w,
             mb_conv_b=mb_conv_b, mb_dt_bias=mb_dt_bias, mb_A_log=mb_A_log, mb_D=mb_D, mb_norm_w=mb_norm_w,
             mb_out_w=mb_out_w, mla_w_down=mla_w_down, mla_q_norm=mla_q_norm, mla_w_uq=mla_w_uq,
             mla_kv_norm=mla_kv_norm, mla_w_ukv=mla_w_ukv, mla_w_o=mla_w_o)
    nb, n_lat, _ = x.shape
    assert n_lat == MOD_ROWS and nb * ctx.shape[1] == MOD_ROWS and ctx.shape[1] == CTX_LEN
    lat_rows = nb * n_lat
    all_rows = lat_rows + MOD_ROWS
    cs = _rope_table(n_lat)

    cc = jnp.concatenate([c, c_ctx[None], jnp.zeros((16 - nb - 1, D_MODEL), F32)], axis=0)
    mods = _mod_all(cc, ada_w, ada_b).reshape(DEPTH, 16, N_MOD, D_MODEL)
    h = jnp.concatenate([x.reshape(lat_rows, D_MODEL), ctx.reshape(MOD_ROWS, D_MODEL)], axis=0)

    for i in range(DEPTH):
        kind, j, last = i % N_MIXERS, i // N_MIXERS, i == DEPTH - 1
        mod = mods[i]
        ctx_needed = not (last and kind == 0)
        ctx_out = not last
        rows = all_rows if ctx_needed else lat_rows
        h = _ffn(h, mod, ffn_in[i, 0].astype(BF16), ffn_out[i, 0].astype(BF16), rows=rows, slot=0)
        if kind == 0:
            h = _hyena_layer(h, mod, rows, nb, j, ctx_out, p)
        elif kind == 1:
            h = _mamba_layer(h, mod, rows, nb, j, p)
        else:
            h = _mla_layer(h, mod, rows, nb, j, cs, p)
        rows = all_rows if ctx_out else lat_rows
        h = _ffn(h, mod, ffn_in[i, 1].astype(BF16), ffn_out[i, 1].astype(BF16), rows=rows, slot=6,
                 final_w=final_norm_w if last else None)
    return h[:lat_rows].reshape(nb, n_lat, D_MODEL)
```

```python
import functools
import math

import jax
import jax.numpy as jnp
from jax import lax
from jax.experimental import pallas as pl
from jax.experimental.pallas import tpu as pltpu

F32 = jnp.float32
BF16 = jnp.bfloat16
HIGHEST = lax.Precision.HIGHEST

D_MODEL = 1024
DEPTH = 4
GRID_W = 64
CTX_LEN = 256
N_MIXERS = 3
N_MOD = 9
FFN_DIM = 2816
EPS = 1e-6

HYENA_EMB = 33
HYENA_BANDS = (HYENA_EMB - 1) // 2
HYENA_FILTER_HIDDEN = 64
HYENA_FAST_DECAY = 0.3
HYENA_SLOW_DECAY = 1.5
HYENA_DECAY_TARGET = 1e-2

SSM_INNER = 2 * D_MODEL
SSM_HEAD_DIM = 64
SSM_HEADS = SSM_INNER // SSM_HEAD_DIM
SSM_GROUPS = 8
HEADS_PER_GROUP = SSM_HEADS // SSM_GROUPS
SSM_STATE = 128
SSM_CHUNK = 128
SSM_BC_DIM = SSM_GROUPS * SSM_STATE
SSM_CONV_DIM = SSM_INNER + 2 * SSM_BC_DIM
SSM_IN_DIM = SSM_INNER + SSM_CONV_DIM + 2 * SSM_HEADS
SSM_IN_PAD = 6400

MLA_HEADS = 16
MLA_NOPE = 64
MLA_ROPE = 32
MLA_V = 64
MLA_Q_RANK = 768
MLA_KV_RANK = 256
MLA_QK = MLA_NOPE + MLA_ROPE
MLA_SCALE = MLA_QK ** -0.5
MLA_DOWN_PAD = 1152
ROPE_AXIS = MLA_ROPE // 2
ROPE_BASE = 10000.0

MOD_ROWS = 2048
VMEM_LIMIT_BYTES = 56 * 1024 * 1024


def _params(*sem):
    return pltpu.CompilerParams(dimension_semantics=sem, vmem_limit_bytes=VMEM_LIMIT_BYTES)


def _rms(x):
    return x * lax.rsqrt(jnp.mean(x * x, axis=-1, keepdims=True) + EPS)


def _silu(x):
    return x * jax.nn.sigmoid(x)


def _dot(a, b, **kw):
    return jnp.dot(a, b, preferred_element_type=F32, **kw)


def _dot_nt(a, b):
    return lax.dot_general(a, b, (((1,), (1,)), ((), ())), preferred_element_type=F32)


def _mod_kernel(x_ref, w_ref, b_ref, o_ref):
    a = _silu(x_ref[...]).astype(BF16)
    o_ref[...] = _dot(a, w_ref[...].astype(BF16)) + b_ref[...]


def _mod_all(cc, ada_w, ada_b):
    n = N_MOD * D_MODEL
    tn = 1024
    return pl.pallas_call(
        _mod_kernel,
        grid=(DEPTH, n // tn),
        in_specs=[pl.BlockSpec((16, D_MODEL), lambda l, j: (0, 0)),
                  pl.BlockSpec((None, D_MODEL, tn), lambda l, j: (l, 0, j)),
                  pl.BlockSpec((None, 1, tn), lambda l, j: (l, 0, j))],
        out_specs=pl.BlockSpec((None, 16, tn), lambda l, j: (l, 0, j)),
        out_shape=jax.ShapeDtypeStruct((DEPTH, 16, n), F32),
        compiler_params=_params("parallel", "parallel"),
        name="adaln_mod",
    )(cc, ada_w, ada_b.reshape(DEPTH, 1, n))


def _mm_kernel(*refs, pro, epi, has_bias, slot, gslot):
    it = iter(refs)
    x_ref, w_ref = next(it), next(it)
    mod_ref = next(it) if (pro == "adaln" or epi == "gres") else None
    nw_ref = next(it) if pro == "rms" else None
    b_ref = next(it) if has_bias else None
    res_ref = next(it) if epi == "gres" else None
    o_ref, xa_ref = next(it), next(it)

    @pl.when(pl.program_id(1) == 0)
    def _():
        x = x_ref[...].astype(F32)
        if pro == "adaln":
            x = _rms(x) * (1.0 + mod_ref[0, slot + 1:slot + 2, :]) + mod_ref[0, slot:slot + 1, :]
        elif pro == "rms":
            x = _rms(x) * nw_ref[...]
        xa_ref[...] = x.astype(BF16)

    acc = _dot(xa_ref[...], w_ref[...])
    if has_bias:
        acc = acc + b_ref[...]
    if epi == "gres":
        acc = res_ref[...] + mod_ref[0, gslot:gslot + 1, :] * acc
    o_ref[...] = acc.astype(o_ref.dtype)


def _mm(x, w, *, rows, tn, tm=1024, x_colblock=0, pro="none", epi="none", mod=None, slot=0, gslot=0,
        nw=None, bias=None, res=None, out_dtype=F32, name="mm"):
    k, n = w.shape
    assert rows % tm == 0 and n % tn == 0 and MOD_ROWS % tm == 0
    if epi == "gres":
        assert tn == n == D_MODEL
    per = MOD_ROWS // tm
    ins = [x, w]
    specs = [pl.BlockSpec((tm, k), lambda i, j: (i, x_colblock)),
             pl.BlockSpec((k, tn), lambda i, j: (0, j))]
    if pro == "adaln" or epi == "gres":
        ins.append(mod)
        specs.append(pl.BlockSpec((1, N_MOD, D_MODEL), lambda i, j: (i // per, 0, 0)))
    if pro == "rms":
        ins.append(nw.reshape(1, k))
        specs.append(pl.BlockSpec((1, k), lambda i, j: (0, 0)))
    if bias is not None:
        ins.append(bias.reshape(1, n))
        specs.append(pl.BlockSpec((1, tn), lambda i, j: (0, j)))
    if epi == "gres":
        ins.append(res)
        specs.append(pl.BlockSpec((tm, tn), lambda i, j: (i, j)))
    return pl.pallas_call(
        functools.partial(_mm_kernel, pro=pro, epi=epi, has_bias=bias is not None, slot=slot, gslot=gslot),
        grid=(rows // tm, n // tn),
        in_specs=specs,
        out_specs=pl.BlockSpec((tm, tn), lambda i, j: (i, j)),
        out_shape=jax.ShapeDtypeStruct((rows, n), out_dtype),
        scratch_shapes=[pltpu.VMEM((tm, k), BF16)],
        compiler_params=_params("parallel", "arbitrary"),
        name=name,
    )(*ins)


def _ffn_kernel(*refs, slot, final):
    if final:
        x_ref, mod_ref, wg_ref, wu_ref, wo_ref, fw_ref, o_ref, xa_ref, acc_ref = refs
    else:
        x_ref, mod_ref, wg_ref, wu_ref, wo_ref, o_ref, xa_ref, acc_ref = refs
    j = pl.program_id(1)

    @pl.when(j == 0)
    def _():
        x = x_ref[...]
        a = _rms(x) * (1.0 + mod_ref[0, slot + 1:slot + 2, :]) + mod_ref[0, slot:slot + 1, :]
        xa_ref[...] = a.astype(BF16)
        acc_ref[...] = jnp.zeros_like(acc_ref)

    xa = xa_ref[...]
    g = _dot(xa, wg_ref[...])
    u = _dot(xa, wu_ref[...])
    acc_ref[...] += _dot((_silu(g) * u).astype(BF16), wo_ref[...])

    @pl.when(j == pl.num_programs(1) - 1)
    def _():
        out = x_ref[...] + (0.5 * mod_ref[0, slot + 2:slot + 3, :]) * acc_ref[...]
        if final:
            out = _rms(out) * fw_ref[...]
        o_ref[...] = out


def _ffn(h, mod, w_in, w_out, *, rows, slot, final_w=None, tm=512, tf=1408):
    nf = FFN_DIM // tf
    per = MOD_ROWS // tm
    final = final_w is not None
    ins = [h, mod, w_in, w_in, w_out]
    specs = [pl.BlockSpec((tm, D_MODEL), lambda i, j: (i, 0)),
             pl.BlockSpec((1, N_MOD, D_MODEL), lambda i, j: (i // per, 0, 0)),
             pl.BlockSpec((D_MODEL, tf), lambda i, j: (0, j)),
             pl.BlockSpec((D_MODEL, tf), lambda i, j: (0, j + nf)),
             pl.BlockSpec((tf, D_MODEL), lambda i, j: (j, 0))]
    if final:
        ins.append(final_w.reshape(1, D_MODEL))
        specs.append(pl.BlockSpec((1, D_MODEL), lambda i, j: (0, 0)))
    return pl.pallas_call(
        functools.partial(_ffn_kernel, slot=slot, final=final),
        grid=(rows // tm, nf),
        in_specs=specs,
        out_specs=pl.BlockSpec((tm, D_MODEL), lambda i, j: (i, 0)),
        out_shape=jax.ShapeDtypeStruct((rows, D_MODEL), F32),
        scratch_shapes=[pltpu.VMEM((tm, D_MODEL), BF16), pltpu.VMEM((tm, D_MODEL), F32)],
        compiler_params=_params("parallel", "arbitrary"),
        name="ffn",
    )(*ins)


def _dwconv_kernel(x_ref, w_ref, b_ref, o_ref, *, ctx_block, act):
    rows = x_ref.shape[0]
    x = x_ref[...]
    seg = jnp.where(pl.program_id(0) == ctx_block, CTX_LEN, rows)
    pos = lax.broadcasted_iota(jnp.int32, x.shape, 0) & (seg - 1)
    prev = jnp.where(pos == 0, 0.0, pltpu.roll(x, 1, 0))
    nxt = jnp.where(pos == seg - 1, 0.0, pltpu.roll(x, rows - 1, 0))
    y = w_ref[0:1, :] * prev + w_ref[1:2, :] * x + w_ref[2:3, :] * nxt + b_ref[...]
    if act:
        y = _silu(y)
    o_ref[...] = y.astype(o_ref.dtype)


def _dwconv(x, w, b, *, rows, col0, act, tc=512, out_dtype=F32, name="dwconv"):
    c = w.shape[1]
    cb0 = col0 // tc
    return pl.pallas_call(
        functools.partial(_dwconv_kernel, ctx_block=8, act=act),
        grid=(rows // MOD_ROWS, c // tc),
        in_specs=[pl.BlockSpec((MOD_ROWS, tc), lambda i, j: (i, cb0 + j)),
                  pl.BlockSpec((3, tc), lambda i, j: (0, j)),
                  pl.BlockSpec((1, tc), lambda i, j: (0, j))],
        out_specs=pl.BlockSpec((MOD_ROWS, tc), lambda i, j: (i, j)),
        out_shape=jax.ShapeDtypeStruct((rows, c), out_dtype),
        compiler_params=_params("parallel", "parallel"),
        name=name,
    )(x, w, b.reshape(1, c))


FILTER_W = 128


def _filter_kernel(za_ref, zb_ref, w1_ref, b1_ref, fr_ref, w2_ref, b2_ref, w3f_ref, w3b_ref, dl_ref,
                   sum_ref, diff_ref):
    fr = fr_ref[...]

    def taps(z_ref, w3_ref):
        z = z_ref[...]
        hid = jnp.sin(fr * (_dot(z, w1_ref[...], precision=HIGHEST) + b1_ref[...]))
        hid = jnp.sin(fr * (_dot(hid, w2_ref[...], precision=HIGHEST) + b2_ref[...]))
        window = jnp.exp(-z[:, 0:1] * dl_ref[...]) * z[:, FILTER_W - 1:FILTER_W]
        return _dot(hid, w3_ref[...], precision=HIGHEST) * window

    hf = taps(za_ref, w3f_ref)
    hb = taps(zb_ref, w3b_ref)
    sum_ref[...] = (hf + hb).astype(BF16)
    diff_ref[...] = (hb - hf).astype(BF16)


def _hyena_filters(L, poly, w1, b1, freq, w2, b2, w3):
    S = L // poly
    nrho = 2 * poly - 1
    jj = jnp.arange(S, dtype=jnp.int32)
    rho = jnp.arange(-(poly - 1), poly, dtype=jnp.int32)
    t_tab = jnp.linspace(0.0, 1.0, L, dtype=F32)
    bands = jnp.linspace(1e-4, HYENA_BANDS - 1, HYENA_BANDS, dtype=F32)

    def feats(pos, lowest):
        valid = (pos >= lowest).astype(F32)
        pc = jnp.maximum(pos, 0)
        ang = (2.0 * math.pi / L) * pc.astype(F32)[..., None] * bands
        z = jnp.concatenate([jnp.take(t_tab, pc)[..., None], jnp.cos(ang), -jnp.sin(ang)], axis=-1)
        z = jnp.pad(z, ((0, 0), (0, 0), (0, FILTER_W - 1 - HYENA_EMB)))
        return jnp.concatenate([z, valid[..., None]], axis=-1).reshape(nrho * S, FILTER_W)

    za = feats(poly * jj[None, :] + rho[:, None], 0)
    zb = feats(poly * jj[None, :] - rho[:, None], 1)
    ph = FILTER_W - HYENA_FILTER_HIDDEN
    w1 = jnp.pad(w1, ((0, FILTER_W - HYENA_EMB), (0, ph)))
    w2 = jnp.pad(w2, ((0, ph), (0, ph)))
    w3 = jnp.pad(w3, ((0, ph), (0, 0))).reshape(FILTER_W, 2, 2, D_MODEL)
    w3f = w3[:, :, 0].reshape(FILTER_W, 2 * D_MODEL)
    w3b = w3[:, :, 1].reshape(FILTER_W, 2 * D_MODEL)
    b1, b2, freq = (jnp.pad(t, (0, ph)).reshape(1, FILTER_W) for t in (b1, b2, freq))
    max_decay = math.log(HYENA_DECAY_TARGET) / HYENA_FAST_DECAY
    min_decay = math.log(HYENA_DECAY_TARGET) / HYENA_SLOW_DECAY
    deltas = jnp.abs(jnp.linspace(min_decay, max_decay, D_MODEL, dtype=F32))
    deltas = jnp.tile(deltas, 2).reshape(1, 2 * D_MODEL)
    full = lambda shape: pl.BlockSpec(shape, lambda i: (0, 0))
    sq = full((FILTER_W, FILTER_W))
    vec = full((1, FILTER_W))
    return pl.pallas_call(
        _filter_kernel,
        grid=(nrho,),
        in_specs=[pl.BlockSpec((S, FILTER_W), lambda i: (i, 0)), pl.BlockSpec((S, FILTER_W), lambda i: (i, 0)),
                  sq, vec, vec, sq, vec, full((FILTER_W, 2 * D_MODEL)), full((FILTER_W, 2 * D_MODEL)),
                  full((1, 2 * D_MODEL))],
        out_specs=[pl.BlockSpec((S, 2 * D_MODEL), lambda i: (i, 0))] * 2,
        out_shape=[jax.ShapeDtypeStruct((nrho * S, 2 * D_MODEL), BF16)] * 2,
        compiler_params=_params("parallel"),
        name="hyena_filter",
    )(za, zb, w1, b1, freq, w2, b2, w3f, w3b, deltas)


def _lmm_kernel(*refs, epi, tf, scale):
    it = iter(refs)
    a_ref, x_ref = next(it), next(it)
    if epi == "spec":
        k_ref = next(it)
    elif epi == "gate":
        g_ref, v_ref, bias_ref = next(it), next(it), next(it)
    o_ref, xb_ref = next(it), next(it)

    @pl.when(pl.program_id(2) == 0)
    def _():
        xb_ref[...] = x_ref[...].astype(BF16)

    acc = _dot(a_ref[...], xb_ref[...])
    if epi == "spec":
        xr, xs = acc[:tf], acc[tf:]
        kr, ki = k_ref[0:tf, :], k_ref[tf:2 * tf, :]
        o_ref[0:tf, :] = ((xr * kr + xs * ki) * scale).astype(o_ref.dtype)
        o_ref[tf:2 * tf, :] = ((xs * kr - xr * ki) * scale).astype(o_ref.dtype)
    elif epi == "gate":
        o_ref[...] = (g_ref[...] * (acc + v_ref[...].astype(F32) * bias_ref[...])).astype(o_ref.dtype)
    else:
        o_ref[...] = acc.astype(o_ref.dtype)


def _lmm(a, x, *, nb, ncb, tm, n, x_row0=0, x_cb0=0, epi="none", tf=0, scale=1.0, kspec=None, k_cb=0,
         g=None, g_cb=0, g_row0=0, v=None, v_cb=0, v_row0=0, bias=None, out_dtype=F32, name="lmm"):
    mo, k = a.shape
    nm = mo // tm
    xr0 = x_row0 // k
    ins = [a, x]
    specs = [pl.BlockSpec((tm, k), lambda b, c, m: (m, 0)),
             pl.BlockSpec((k, n), lambda b, c, m: (xr0 + b, x_cb0 + c))]
    if epi == "spec":
        ins.append(kspec)
        specs.append(pl.BlockSpec((tm, n), lambda b, c, m: (m, k_cb)))
    elif epi == "gate":
        gr0, vr0 = g_row0 // tm, v_row0 // tm
        ins += [g, v, bias.reshape(1, n)]
        specs += [pl.BlockSpec((tm, n), lambda b, c, m: (gr0 + b * nm + m, g_cb)),
                  pl.BlockSpec((tm, n), lambda b, c, m: (vr0 + b * nm + m, v_cb)),
                  pl.BlockSpec((1, n), lambda b, c, m: (0, 0))]
    return pl.pallas_call(
        functools.partial(_lmm_kernel, epi=epi, tf=tf, scale=scale),
        grid=(nb, ncb, nm),
        in_specs=specs,
        out_specs=pl.BlockSpec((tm, n), lambda b, c, m: (b * nm + m, c)),
        out_shape=jax.ShapeDtypeStruct((nb * mo, ncb * n), out_dtype),
        scratch_shapes=[pltpu.VMEM((k, n), BF16)],
        compiler_params=_params("parallel", "parallel", "arbitrary"),
        name=name,
    )(*ins)


def _dft_mats(L, tf):
    idx = jnp.arange(L, dtype=jnp.int32)
    ph = ((2 * idx[:, None] + 1) * idx[None, :]) % (4 * L)
    ang = ph.astype(F32) * (2.0 * math.pi / (4 * L))
    c, s = jnp.cos(ang), jnp.sin(ang)
    nf = L // tf
    fwd = jnp.stack([c.reshape(nf, tf, L), s.reshape(nf, tf, L)], axis=1).reshape(2 * L, L)
    inv = jnp.stack([c.T.reshape(L, nf, tf), s.T.reshape(L, nf, tf)], axis=2).reshape(L, 2 * L)
    return c.astype(BF16), s.astype(BF16), fwd.astype(BF16), inv.astype(BF16)


def _interleave(kr, ki, tf):
    L, n = kr.shape
    return jnp.stack([kr.reshape(L // tf, tf, n), ki.reshape(L // tf, tf, n)], axis=1).reshape(2 * L, n)


def _hyena_core(u, L, nb, row0, fparams, hy_bias, name):
    tf = min(L, 256)
    tmi = min(L, 256)
    cmat, smat, fwd, inv = _dft_mats(L, tf)
    hsum, hdiff = _hyena_filters(L, 1, *fparams)
    kr = _lmm(cmat, hsum, nb=1, ncb=2, tm=tf, n=D_MODEL, name=name + "_kr")
    ki = _lmm(smat, hdiff, nb=1, ncb=2, tm=tf, n=D_MODEL, name=name + "_ki")
    kspec = _interleave(kr, ki, tf)
    scale = 1.0 / L
    y1 = _lmm(fwd, u, nb=nb, ncb=1, tm=2 * tf, n=D_MODEL, x_row0=row0, x_cb0=2, epi="spec", tf=tf, scale=scale,
              kspec=kspec, k_cb=0, out_dtype=BF16, name=name + "_fwd1")
    z = _lmm(inv, y1, nb=nb, ncb=1, tm=tmi, n=D_MODEL, epi="gate", g=u, g_cb=0, g_row0=row0, v=u, v_cb=2,
             v_row0=row0, bias=hy_bias[0], out_dtype=F32, name=name + "_inv1")
    y2 = _lmm(fwd, z, nb=nb, ncb=1, tm=2 * tf, n=D_MODEL, epi="spec", tf=tf, scale=scale,
              kspec=kspec, k_cb=1, out_dtype=BF16, name=name + "_fwd2")
    return _lmm(inv, y2, nb=nb, ncb=1, tm=tmi, n=D_MODEL, epi="gate", g=u, g_cb=1, g_row0=row0, v=z, v_cb=0,
                bias=hy_bias[1], out_dtype=F32, name=name + "_inv2")


POLY = 4


def _poly_conv_kernel(a_ref, ai_ref, u_ref, kr_ref, ki_ref, g_ref, bias_ref, o_ref, *, scale):
    S = a_ref.shape[1]
    tc = u_ref.shape[1]
    phases = [u_ref[pl.ds(r, S, stride=POLY), :] for r in range(POLY)]
    e = _dot(a_ref[...], jnp.concatenate([ph.astype(BF16) for ph in phases], axis=1))
    f = []
    for r in range(POLY):
        fre = fim = None
        for rp in range(POLY):
            kr = kr_ref[r - rp + POLY - 1]
            ki = ki_ref[r - rp + POLY - 1]
            er = e[0:S, rp * tc:(rp + 1) * tc]
            es = e[S:2 * S, rp * tc:(rp + 1) * tc]
            tre = kr * er + ki * es
            tim = kr * es - ki * er
            fre = tre if fre is None else fre + tre
            fim = tim if fim is None else fim + tim
        f.append(jnp.concatenate([fre, fim], axis=0).astype(BF16))
    y = _dot(ai_ref[...], jnp.concatenate(f, axis=1)) * scale
    for r in range(POLY):
        out = g_ref[pl.ds(r, S, stride=POLY), :] * (y[:, r * tc:(r + 1) * tc] + phases[r] * bias_ref[...])
        o_ref[pl.ds(r, S, stride=POLY), :] = out


def _poly_conv(fwd, inv, x, x_cb, g, g_cb, kre, kim, order, bias, *, nb, tc=128, name="hy_conv"):
    L = MOD_ROWS
    S = L // POLY
    nct = D_MODEL // tc
    nrho = 2 * POLY - 1
    return pl.pallas_call(
        functools.partial(_poly_conv_kernel, scale=1.0 / S),
        grid=(nct, nb),
        in_specs=[pl.BlockSpec((2 * S, S), lambda c, b: (0, 0)),
                  pl.BlockSpec((S, 2 * S), lambda c, b: (0, 0)),
                  pl.BlockSpec((L, tc), lambda c, b: (b, x_cb * nct + c)),
                  pl.BlockSpec((nrho, S, tc), lambda c, b: (0, 0, order * nct + c)),
                  pl.BlockSpec((nrho, S, tc), lambda c, b: (0, 0, order * nct + c)),
                  pl.BlockSpec((L, tc), lambda c, b: (b, g_cb * nct + c)),
                  pl.BlockSpec((1, tc), lambda c, b: (0, c))],
        out_specs=pl.BlockSpec((L, tc), lambda c, b: (b, c)),
        out_shape=jax.ShapeDtypeStruct((nb * L, D_MODEL), F32),
        compiler_params=_params("parallel", "parallel"),
        name=name,
    )(fwd, inv, x, kre, kim, g, bias.reshape(1, D_MODEL))


def _hyena_core_poly(u, nb, fparams, hy_bias):
    S = MOD_ROWS // POLY
    nrho = 2 * POLY - 1
    cmat, smat, fwd, inv = _dft_mats(S, S)
    ksum, kdiff = _hyena_filters(MOD_ROWS, POLY, *fparams)
    kre = _lmm(cmat, ksum, nb=nrho, ncb=2, tm=S, n=D_MODEL, name="hy_lat_kr").reshape(nrho, S, 2 * D_MODEL)
    kim = _lmm(smat, kdiff, nb=nrho, ncb=2, tm=S, n=D_MODEL, name="hy_lat_ki").reshape(nrho, S, 2 * D_MODEL)
    z = _poly_conv(fwd, inv, u, 2, u, 0, kre, kim, 0, hy_bias[0], nb=nb, name="hy_lat_conv1")
    return _poly_conv(fwd, inv, z, 0, u, 1, kre, kim, 1, hy_bias[1], nb=nb, name="hy_lat_conv2")


def _softplus(x):
    return jnp.maximum(x, 0.0) + jnp.log(1.0 + jnp.exp(-jnp.abs(x)))


def _ssd_kernel(xs_ref, b_ref, c_ref, dt_ref, dtt_ref, dtb_ref, dtbt_ref, al_ref, alt_ref, y_ref, st_ref):
    T = SSM_CHUNK
    N = SSM_STATE
    d = pl.program_id(1)

    @pl.when(pl.program_id(2) == 0)
    def _():
        st_ref[...] = jnp.zeros_like(st_ref)

    sgn = jnp.where(d == 0, 1, -1)
    ri = lax.broadcasted_iota(jnp.int32, (T, T), 0)
    ci = lax.broadcasted_iota(jnp.int32, (T, T), 1)
    mask = (ri - ci) * sgn >= 0
    tri_col = mask.astype(F32)
    tri_row = ((ci - ri) * sgn >= 0).astype(F32)
    low = lax.broadcasted_iota(jnp.int32, (T, 2 * SSM_HEAD_DIM), 1) < SSM_HEAD_DIM

    a = _softplus(dt_ref[...] + dtb_ref[...]) * -jnp.exp(al_ref[...])
    dtt = _softplus(dtt_ref[...] + dtbt_ref[...])
    at = dtt * -jnp.exp(alt_ref[...])
    cs_col = _dot(tri_col, a, precision=HIGHEST)
    cs_row = _dot(at, tri_row, precision=HIGHEST)
    tot = jnp.sum(at, axis=1, keepdims=True)
    row_dt = cs_row - jnp.log(dtt)
    w_out = dtt * jnp.exp(tot - cs_row)
    e_tot = jnp.exp(tot)

    for g in range(SSM_GROUPS):
        bt = jnp.transpose(b_ref[:, g * N:(g + 1) * N].astype(F32))
        cg = c_ref[:, g * N:(g + 1) * N].astype(F32)
        cb = _dot(cg.astype(BF16), bt.astype(BF16))
        for k in range(HEADS_PER_GROUP // 2):
            h0 = g * HEADS_PER_GROUP + 2 * k
            lanes = slice(h0 * SSM_HEAD_DIM, (h0 + 2) * SSM_HEAD_DIM)
            xs = xs_ref[:, lanes].astype(BF16)
            st = st_ref[g, :, k * 128:(k + 1) * 128]
            rhs = jnp.concatenate([xs, st.astype(BF16)], axis=0)
            ys, upds = [], []
            for h in (h0, h0 + 1):
                col = jnp.broadcast_to(cs_col[:, h:h + 1], (T, T))
                m = jnp.where(mask, jnp.exp(col - row_dt[h:h + 1, :]), 0.0) * cb
                lhs = jnp.concatenate([m, jnp.exp(col) * cg], axis=1).astype(BF16)
                ys.append(_dot(lhs, rhs))
                upds.append(_dot((bt * w_out[h:h + 1, :]).astype(BF16), xs))
            y_ref[:, lanes] = jnp.where(low, ys[0], ys[1]).astype(y_ref.dtype)
            decay = jnp.where(low, e_tot[h0:h0 + 1, :], e_tot[h0 + 1:h0 + 2, :])
            st_ref[g, :, k * 128:(k + 1) * 128] = st * decay + jnp.where(low, upds[0], upds[1])


def _ssd(xbc, dta, dtt, dt_bias, a_log, *, nb):
    m = xbc.shape[0]
    T = SSM_CHUNK
    H = SSM_HEADS
    lat_c = MOD_ROWS // T
    ctx_c = CTX_LEN // T
    ctx0 = nb * lat_c
    steps = lat_c + ctx_c

    def rb(b, d, t):
        ctx = ctx0 + ctx_c * b + jnp.where(d == 0, t, ctx_c - 1 - t)
        lat = lat_c * b + jnp.where(d == 0, t - ctx_c, steps - 1 - t)
        return jnp.where(t < ctx_c, ctx, lat)

    return pl.pallas_call(
        _ssd_kernel,
        grid=(nb, 2, steps),
        in_specs=[pl.BlockSpec((T, SSM_INNER), lambda b, d, t: (rb(b, d, t), 0)),
                  pl.BlockSpec((T, SSM_BC_DIM), lambda b, d, t: (rb(b, d, t), 2)),
                  pl.BlockSpec((T, SSM_BC_DIM), lambda b, d, t: (rb(b, d, t), 3)),
                  pl.BlockSpec((None, T, H), lambda b, d, t: (d, rb(b, d, t), 0)),
                  pl.BlockSpec((None, H, T), lambda b, d, t: (d, 0, rb(b, d, t))),
                  pl.BlockSpec((None, 1, H), lambda b, d, t: (d, 0, 0)),
                  pl.BlockSpec((None, H, 1), lambda b, d, t: (d, 0, 0)),
                  pl.BlockSpec((None, 1, H), lambda b, d, t: (d, 0, 0)),
                  pl.BlockSpec((None, H, 1), lambda b, d, t: (d, 0, 0))],
        out_specs=pl.BlockSpec((None, T, SSM_INNER), lambda b, d, t: (d, rb(b, d, t), 0)),
        out_shape=jax.ShapeDtypeStruct((2, m, SSM_INNER), BF16),
        scratch_shapes=[pltpu.VMEM((SSM_GROUPS, SSM_STATE, HEADS_PER_GROUP * SSM_HEAD_DIM), F32)],
        compiler_params=_params("parallel", "parallel", "arbitrary"),
        name="ssd_scan",
    )(xbc, xbc, xbc, dta, dtt, dt_bias.reshape(2, 1, H), dt_bias.reshape(2, H, 1),
      a_log.reshape(2, 1, H), a_log.reshape(2, H, 1))


def _ssd_finish_kernel(yf_ref, yb_ref, xs_ref, z_ref, dsk_ref, nw_ref, o_ref):
    gw = SSM_INNER // SSM_GROUPS
    z = z_ref[...]
    y = yf_ref[...].astype(F32) + yb_ref[...].astype(F32) + dsk_ref[...] * xs_ref[...].astype(F32)
    y = y * _silu(z)
    for g in range(SSM_GROUPS):
        yg = _rms(y[:, g * gw:(g + 1) * gw])
        o_ref[:, g * gw:(g + 1) * gw] = (yg * nw_ref[:, g * gw:(g + 1) * gw]).astype(o_ref.dtype)


def _ssd_finish(y2, xbc, zx, d_skip, norm_w, *, rows, tm=256):
    n = SSM_INNER
    return pl.pallas_call(
        _ssd_finish_kernel,
        grid=(rows // tm,),
        in_specs=[pl.BlockSpec((None, tm, n), lambda i: (0, i, 0)),
                  pl.BlockSpec((None, tm, n), lambda i: (1, i, 0)),
                  pl.BlockSpec((tm, n), lambda i: (i, 0)),
                  pl.BlockSpec((tm, n), lambda i: (i, 0)),
                  pl.BlockSpec((1, n), lambda i: (0, 0)),
                  pl.BlockSpec((1, n), lambda i: (0, 0))],
        out_specs=pl.BlockSpec((tm, n), lambda i: (i, 0)),
        out_shape=jax.ShapeDtypeStruct((rows, n), BF16),
        compiler_params=_params("parallel"),
        name="ssd_finish",
    )(y2, y2, xbc, zx, jnp.repeat(d_skip, SSM_HEAD_DIM).reshape(1, n), norm_w.reshape(1, n))


HEAD_W = 2 * MLA_NOPE
ATTN_SUB_ROWS = 256


def _attn_kernel(*refs, has_lat):
    if has_lat:
        q_ref, kvc_ref, krc_ref, kv_ref, kr_ref, tq_ref, tk_ref, o_ref, kc_scr, vc_scr, k_scr, v_scr = refs
    else:
        q_ref, kvc_ref, krc_ref, o_ref, kc_scr, vc_scr = refs
    R = MLA_ROPE
    kscale = MLA_SCALE * math.log2(math.e)

    def build_keys():
        krc = krc_ref[...].astype(F32)
        lane_c = lax.broadcasted_iota(jnp.int32, krc.shape, 1)
        pe_c = jnp.where((lane_c >= MLA_NOPE) & (lane_c < MLA_NOPE + R), pltpu.roll(krc, MLA_NOPE, 1), 0.0)
        for hh in range(2):
            kvh = kvc_ref[:, hh * HEAD_W:(hh + 1) * HEAD_W].astype(F32)
            kc_scr[hh] = (jnp.where(lane_c < MLA_NOPE, kvh, pe_c) * kscale).astype(BF16)
            vc_scr[hh] = jnp.where(lane_c < MLA_NOPE, 1.0, kvh).astype(BF16)
        if has_lat:
            t = kr_ref[...].astype(F32) * tk_ref[...]
            lane = lax.broadcasted_iota(jnp.int32, t.shape, 1)
            krot = jnp.where(lane < R, t + pltpu.roll(t, HEAD_W - R, 1), 0.0)
            krr = pltpu.roll(krot, MLA_NOPE, 1) + pltpu.roll(krot, MLA_NOPE + R, 1)
            for hh in range(2):
                kvh = kv_ref[:, hh * HEAD_W:(hh + 1) * HEAD_W].astype(F32)
                k_scr[hh] = (jnp.where(lane < MLA_NOPE, kvh, krr) * kscale).astype(BF16)
                v_scr[hh] = jnp.where(lane < MLA_NOPE, 1.0, kvh).astype(BF16)

    if has_lat:
        pl.when(pl.program_id(2) == 0)(build_keys)
    else:
        build_keys()

    sub = min(q_ref.shape[0], ATTN_SUB_ROWS)
    for r0 in range(0, q_ref.shape[0], sub):
        rows = slice(r0, r0 + sub)
        res = []
        for hh in range(2):
            q = q_ref[rows, hh * HEAD_W:(hh + 1) * HEAD_W]
            s_c = _dot_nt(q, kc_scr[hh])
            mx = jnp.max(s_c, axis=-1, keepdims=True)
            if has_lat:
                ql = (q.astype(F32) * tq_ref[rows, :]).astype(BF16)
                s_l = _dot_nt(ql, k_scr[hh])
                mx = jnp.maximum(mx, jnp.max(s_l, axis=-1, keepdims=True))
                acc = _dot(jnp.exp2(s_l - mx).astype(BF16), v_scr[hh])
                acc = acc + _dot(jnp.exp2(s_c - mx).astype(BF16), vc_scr[hh])
            else:
                acc = _dot(jnp.exp2(s_c - mx).astype(BF16), vc_scr[hh])
            res.append(acc / pltpu.roll(acc, MLA_V, 1))
        lane_o = lax.broadcasted_iota(jnp.int32, res[0].shape, 1)
        o_ref[rows, :] = jnp.where(lane_o < MLA_V, pltpu.roll(res[0], MLA_V, 1), res[1]).astype(o_ref.dtype)


def _attention(q, kv, dn, cs, *, nb, tq=512):
    L = MOD_ROWS
    hp = MLA_HEADS // 2
    w = 2 * HEAD_W
    nq = L // tq
    cb = nb * (L // CTX_LEN)
    kr_cb = (MLA_Q_RANK + MLA_KV_RANK) // HEAD_W
    tab_q = jnp.concatenate([jnp.ones((L, MLA_NOPE), F32), cs], axis=1)
    tab_k = jnp.concatenate([cs, jnp.zeros((L, HEAD_W - 2 * MLA_ROPE), F32)], axis=1)
    lat = pl.pallas_call(
        functools.partial(_attn_kernel, has_lat=True),
        grid=(nb, hp, nq),
        in_specs=[pl.BlockSpec((tq, w), lambda b, p, i: (b * nq + i, p)),
                  pl.BlockSpec((CTX_LEN, w), lambda b, p, i: (cb + b, p)),
                  pl.BlockSpec((CTX_LEN, HEAD_W), lambda b, p, i: (cb + b, kr_cb)),
                  pl.BlockSpec((L, w), lambda b, p, i: (b, p)),
                  pl.BlockSpec((L, HEAD_W), lambda b, p, i: (b, kr_cb)),
                  pl.BlockSpec((tq, HEAD_W), lambda b, p, i: (i, 0)),
                  pl.BlockSpec((L, HEAD_W), lambda b, p, i: (0, 0))],
        out_specs=pl.BlockSpec((tq, 2 * MLA_V), lambda b, p, i: (b * nq + i, p)),
        out_shape=jax.ShapeDtypeStruct((nb * L, MLA_HEADS * MLA_V), BF16),
        scratch_shapes=[pltpu.VMEM((2, CTX_LEN, HEAD_W), BF16)] * 2 + [pltpu.VMEM((2, L, HEAD_W), BF16)] * 2,
        compiler_params=_params("parallel", "parallel", "arbitrary"),
        name="mla_attn",
    )(q, kv, dn, kv, dn, tab_q, tab_k)
    ctx = pl.pallas_call(
        functools.partial(_attn_kernel, has_lat=False),
        grid=(nb, hp),
        in_specs=[pl.BlockSpec((CTX_LEN, w), lambda b, p: (cb + b, p)),
                  pl.BlockSpec((CTX_LEN, w), lambda b, p: (cb + b, p)),
                  pl.BlockSpec((CTX_LEN, HEAD_W), lambda b, p: (cb + b, kr_cb))],
        out_specs=pl.BlockSpec((CTX_LEN, 2 * MLA_V), lambda b, p: (b, p)),
        out_shape=jax.ShapeDtypeStruct((nb * CTX_LEN, MLA_HEADS * MLA_V), BF16),
        scratch_shapes=[pltpu.VMEM((2, CTX_LEN, HEAD_W), BF16)] * 2,
        compiler_params=_params("parallel", "parallel"),
        name="mla_attn_ctx",
    )(q, kv, dn)
    return jnp.concatenate([lat, ctx], axis=0)


def _rot_cols(w):
    wp = w.reshape(w.shape[:-1] + (2, 2, ROPE_AXIS // 2))
    return jnp.stack([-wp[..., 1, :], wp[..., 0, :]], axis=-2).reshape(w.shape)


def _rope_table(n_lat):
    rows = n_lat // GRID_W
    row = jnp.repeat(jnp.arange(rows), GRID_W)
    col = jnp.tile(jnp.arange(GRID_W), rows)
    inv = 1.0 / (ROPE_BASE ** (jnp.arange(0, ROPE_AXIS, 2, dtype=F32) / ROPE_AXIS))
    ar = row.astype(F32)[:, None] * inv[None, :]
    ac = col.astype(F32)[:, None] * inv[None, :]
    ang = jnp.concatenate([ar, ar, ac, ac], axis=-1)
    return jnp.concatenate([jnp.cos(ang), jnp.sin(ang)], axis=-1)


def _hyena_layer(h, mod, rows, nb, j, with_ctx, p):
    u0 = _mm(h, p["hy_in_w"][j].astype(BF16), rows=rows, tn=1024, pro="adaln", mod=mod, slot=3,
             bias=p["hy_in_b"][j], name="hyena_in")
    u = _dwconv(u0, p["hy_conv_w"][j], p["hy_conv_b"][j], rows=rows, col0=0, act=False, name="hyena_conv")
    fparams = (p["hy_pos_w1"][j], p["hy_pos_b1"][j], p["hy_freq"][j], p["hy_pos_w2"][j], p["hy_pos_b2"][j],
               p["hy_pos_w3"][j])
    y = _hyena_core_poly(u, nb, fparams, p["hy_bias"][j])
    if with_ctx:
        yc = _hyena_core(u, CTX_LEN, nb, nb * MOD_ROWS, fparams, p["hy_bias"][j], "hy_ctx")
        y = jnp.concatenate([y, yc], axis=0)
    return _mm(y, p["hy_out_w"][j].astype(BF16), rows=rows, tn=D_MODEL, epi="gres", mod=mod, gslot=5,
               bias=p["hy_out_b"][j], res=h, name="hyena_out")


def _mamba_layer(h, mod, rows, nb, j, p):
    w_in = jnp.pad(p["mb_in_w"][j].astype(BF16), ((0, 0), (0, SSM_IN_PAD - SSM_IN_DIM)))
    zx = _mm(h, w_in, rows=rows, tn=SSM_IN_PAD // 5, pro="adaln", mod=mod, slot=3, name="mamba_in")
    xbc = _dwconv(zx, p["mb_conv_w"][j], p["mb_conv_b"][j], rows=rows, col0=SSM_INNER, act=True, out_dtype=BF16,
                  name="mamba_conv")
    dtr = zx[:, SSM_INNER + SSM_CONV_DIM:SSM_IN_DIM].reshape(rows, 2, SSM_HEADS)
    dta = jnp.transpose(dtr, (1, 0, 2))
    dtt = jnp.transpose(dtr, (1, 2, 0))
    y2 = _ssd(xbc, dta, dtt, p["mb_dt_bias"][j], p["mb_A_log"][j], nb=nb)
    yn = _ssd_finish(y2, xbc, zx, p["mb_D"][j], p["mb_norm_w"][j], rows=rows)
    return _mm(yn, p["mb_out_w"][j].astype(BF16), rows=rows, tn=D_MODEL, epi="gres", mod=mod, gslot=5, res=h,
               name="mamba_out")


def _mla_layer(h, mod, rows, nb, j, cs, p):
    wd = p["mla_w_down"][j]
    kpe_w = wd[:, MLA_Q_RANK + MLA_KV_RANK:]
    wd = jnp.concatenate([wd, _rot_cols(kpe_w),
                          jnp.zeros((D_MODEL, MLA_DOWN_PAD - wd.shape[1] - MLA_ROPE), F32)], axis=1).astype(BF16)
    wq = p["mla_w_uq"][j].reshape(MLA_Q_RANK, MLA_HEADS, MLA_QK)
    wq = jnp.concatenate([wq, _rot_cols(wq[..., MLA_NOPE:])], axis=-1).reshape(MLA_Q_RANK, -1).astype(BF16)
    dn = _mm(h, wd, rows=rows, tn=MLA_DOWN_PAD, pro="adaln", mod=mod, slot=3, out_dtype=F32, name="mla_down")
    q = _mm(dn, wq, rows=rows, tn=1024, pro="rms", nw=p["mla_q_norm"][j], out_dtype=BF16, name="mla_uq")
    kv = _mm(dn, p["mla_w_ukv"][j].astype(BF16), rows=rows, tn=1024, x_colblock=MLA_Q_RANK // MLA_KV_RANK,
             pro="rms", nw=p["mla_kv_norm"][j], out_dtype=BF16, name="mla_ukv")
    o = _attention(q, kv, dn, cs, nb=nb)
    return _mm(o, p["mla_w_o"][j].astype(BF16), rows=rows, tn=D_MODEL, epi="gres", mod=mod, gslot=5, res=h,
               name="mla_out")


def kernel(x, c, ctx, c_ctx, ada_w, ada_b, ffn_in, ffn_out, hy_in_w, hy_in_b, hy_conv_w, hy_conv_b, hy_pos_w1, hy_pos_b1, hy_freq, hy_pos_w2, hy_pos_b2, hy_pos_w3, hy_bias, hy_out_w, hy_out_b, mb_in_w, mb_conv_w, mb_conv_b, mb_dt_bias, mb_A_log, mb_D, mb_norm_w, mb_out_w, mla_w_down, mla_q_norm, mla_w_uq, mla_kv_norm, mla_w_ukv, mla_w_o, final_norm_w):
    p = dict(hy_in_w=hy_in_w, hy_in_b=hy_in_b, hy_conv_w=hy_conv_w, hy_conv_b=hy_conv_b, hy_pos_w1=hy_pos_w1,
             hy_pos_b1=hy_pos_b1, hy_freq=hy_freq, hy_pos_w2=hy_pos_w2, hy_pos_b2=hy_pos_b2, hy_pos_w3=hy_pos_w3,
             hy_bias=hy_bias, hy_out_w=hy_out_w, hy_out_b=hy_out_b, mb_in_w=mb_in_w, mb_conv_w=mb_conv_w,
             mb_conv_b=mb_conv_b, mb_dt_bias=mb_dt_bias, mb_A_log=mb_A_log, mb_D=mb_D, mb_norm_w=mb_norm_w,
             mb_out_w=mb_out_w, mla_w_down=mla_w_down, mla_q_norm=mla_q_norm, mla_w_uq=mla_w_uq,
             mla_kv_norm=mla_kv_norm, mla_w_ukv=mla_w_ukv, mla_w_o=mla_w_o)
    nb, n_lat, _ = x.shape
    assert n_lat == MOD_ROWS and nb * ctx.shape[1] == MOD_ROWS and ctx.shape[1] == CTX_LEN
    lat_rows = nb * n_lat
    all_rows = lat_rows + MOD_ROWS
    cs = _rope_table(n_lat)

    cc = jnp.concatenate([c, c_ctx[None], jnp.zeros((16 - nb - 1, D_MODEL), F32)], axis=0)
    mods = _mod_all(cc, ada_w, ada_b).reshape(DEPTH, 16, N_MOD, D_MODEL)
    h = jnp.concatenate([x.reshape(lat_rows, D_MODEL), ctx.reshape(MOD_ROWS, D_MODEL)], axis=0)

    for i in range(DEPTH):
        kind, j, last = i % N_MIXERS, i // N_MIXERS, i == DEPTH - 1
        mod = mods[i]
        ctx_needed = not (last and kind == 0)
        ctx_out = not last
        rows = all_rows if ctx_needed else lat_rows
        h = _ffn(h, mod, ffn_in[i, 0].astype(BF16), ffn_out[i, 0].astype(BF16), rows=rows, slot=0)
        if kind == 0:
            h = _hyena_layer(h, mod, rows, nb, j, ctx_out, p)
        elif kind == 1:
            h = _mamba_layer(h, mod, rows, nb, j, p)
        else:
            h = _mla_layer(h, mod, rows, nb, j, cs, p)
        rows = all_rows if ctx_out else lat_rows
        h = _ffn(h, mod, ffn_in[i, 1].astype(BF16), ffn_out[i, 1].astype(BF16), rows=rows, slot=6,
                 final_w=final_norm_w if last else None)
    return h[:lat_rows].reshape(nb, n_lat, D_MODEL)
```

```python
import functools
import math

import jax
import jax.numpy as jnp
from jax import lax
from jax.experimental import pallas as pl
from jax.experimental.pallas import tpu as pltpu

F32 = jnp.float32
BF16 = jnp.bfloat16
HIGHEST = lax.Precision.HIGHEST

D_MODEL = 1024
DEPTH = 4
GRID_W = 64
CTX_LEN = 256
N_MIXERS = 3
N_MOD = 9
FFN_DIM = 2816
EPS = 1e-6

HYENA_EMB = 33
HYENA_BANDS = (HYENA_EMB - 1) // 2
HYENA_FILTER_HIDDEN = 64
HYENA_FAST_DECAY = 0.3
HYENA_SLOW_DECAY = 1.5
HYENA_DECAY_TARGET = 1e-2

SSM_INNER = 2 * D_MODEL
SSM_HEAD_DIM = 64
SSM_HEADS = SSM_INNER // SSM_HEAD_DIM
SSM_GROUPS = 8
HEADS_PER_GROUP = SSM_HEADS // SSM_GROUPS
SSM_STATE = 128
SSM_CHUNK = 128
SSM_BC_DIM = SSM_GROUPS * SSM_STATE
SSM_CONV_DIM = SSM_INNER + 2 * SSM_BC_DIM
SSM_IN_DIM = SSM_INNER + SSM_CONV_DIM + 2 * SSM_HEADS
SSM_ZDT_PAD = SSM_INNER + 128

MLA_HEADS = 16
MLA_NOPE = 64
MLA_ROPE = 32
MLA_V = 64
MLA_Q_RANK = 768
MLA_KV_RANK = 256
MLA_QK = MLA_NOPE + MLA_ROPE
MLA_SCALE = MLA_QK ** -0.5
MLA_DOWN_PAD = 1152
ROPE_AXIS = MLA_ROPE // 2
ROPE_BASE = 10000.0

MOD_ROWS = 2048
VMEM_LIMIT_BYTES = 56 * 1024 * 1024


def _params(*sem):
    return pltpu.CompilerParams(dimension_semantics=sem, vmem_limit_bytes=VMEM_LIMIT_BYTES)


def _rms(x):
    return x * lax.rsqrt(jnp.mean(x * x, axis=-1, keepdims=True) + EPS)


def _silu(x):
    return x * jax.nn.sigmoid(x)


def _dot(a, b, **kw):
    return jnp.dot(a, b, preferred_element_type=F32, **kw)


def _dot_nt(a, b):
    return lax.dot_general(a, b, (((1,), (1,)), ((), ())), preferred_element_type=F32)


def _mod_kernel(x_ref, w_ref, b_ref, o_ref):
    a = _silu(x_ref[...]).astype(BF16)
    o_ref[...] = _dot(a, w_ref[...].astype(BF16)) + b_ref[...]


def _mod_all(cc, ada_w, ada_b):
    n = N_MOD * D_MODEL
    tn = 1024
    return pl.pallas_call(
        _mod_kernel,
        grid=(DEPTH, n // tn),
        in_specs=[pl.BlockSpec((16, D_MODEL), lambda l, j: (0, 0)),
                  pl.BlockSpec((None, D_MODEL, tn), lambda l, j: (l, 0, j)),
                  pl.BlockSpec((None, 1, tn), lambda l, j: (l, 0, j))],
        out_specs=pl.BlockSpec((None, 16, tn), lambda l, j: (l, 0, j)),
        out_shape=jax.ShapeDtypeStruct((DEPTH, 16, n), F32),
        compiler_params=_params("parallel", "parallel"),
        name="adaln_mod",
    )(cc, ada_w, ada_b.reshape(DEPTH, 1, n))


HALO = 16


def _mm_kernel(*refs, pro, epi, has_bias, slot, gslot, ctx_row0):
    conv = epi in ("conv", "conv_silu")
    it = iter(refs)
    x_ref, w_ref = next(it), next(it)
    xp_ref, xn_ref = (next(it), next(it)) if conv else (None, None)
    mod_ref = next(it) if (pro == "adaln" or epi == "gres") else None
    nw_ref = next(it) if pro == "rms" else None
    b_ref = next(it) if has_bias else None
    res_ref = next(it) if epi == "gres" else None
    cw_ref, cb_ref = (next(it), next(it)) if conv else (None, None)
    o_ref, xa_ref = next(it), next(it)
    tm = x_ref.shape[0]

    def prologue(x):
        x = x.astype(F32)
        if pro == "adaln":
            x = _rms(x) * (1.0 + mod_ref[0, slot + 1:slot + 2, :]) + mod_ref[0, slot:slot + 1, :]
        elif pro == "rms":
            x = _rms(x) * nw_ref[...]
        return x.astype(BF16)

    @pl.when(pl.program_id(1) == 0)
    def _():
        if conv:
            xa_ref[0:HALO, :] = prologue(xp_ref[...])
            xa_ref[HALO:HALO + tm, :] = prologue(x_ref[...])
            xa_ref[HALO + tm:2 * HALO + tm, :] = prologue(xn_ref[...])
        else:
            xa_ref[...] = prologue(x_ref[...])

    acc = _dot(xa_ref[...], w_ref[...])
    if has_bias:
        acc = acc + b_ref[...]
    if conv:
        n_all = tm + 2 * HALO
        row0 = pl.program_id(0) * tm
        seg = jnp.where(row0 >= ctx_row0, CTX_LEN, MOD_ROWS)
        cur = acc[HALO:HALO + tm]
        pos = (lax.broadcasted_iota(jnp.int32, cur.shape, 0) + row0) & (seg - 1)
        prev = jnp.where(pos == 0, 0.0, pltpu.roll(acc, 1, 0)[HALO:HALO + tm])
        nxt = jnp.where(pos == seg - 1, 0.0, pltpu.roll(acc, n_all - 1, 0)[HALO:HALO + tm])
        acc = cw_ref[0:1, :] * prev + cw_ref[1:2, :] * cur + cw_ref[2:3, :] * nxt + cb_ref[...]
        if epi == "conv_silu":
            acc = _silu(acc)
    if epi == "gres":
        acc = res_ref[...] + mod_ref[0, gslot:gslot + 1, :] * acc
    o_ref[...] = acc.astype(o_ref.dtype)


def _mm(x, w, *, rows, tn, tm=1024, x_colblock=0, pro="none", epi="none", mod=None, slot=0, gslot=0,
        nw=None, bias=None, res=None, conv_w=None, conv_b=None, ctx_row0=None, out_dtype=F32, name="mm"):
    k, n = w.shape
    assert rows % tm == 0 and n % tn == 0 and MOD_ROWS % tm == 0
    if epi == "gres":
        assert tn == n == D_MODEL
    conv = epi in ("conv", "conv_silu")
    per = MOD_ROWS // tm
    ins = [x, w]
    specs = [pl.BlockSpec((tm, k), lambda i, j: (i, x_colblock)),
             pl.BlockSpec((k, tn), lambda i, j: (0, j))]
    if conv:
        assert x_colblock == 0 and tm % CTX_LEN == 0
        hb = tm // HALO
        last = rows // HALO - 1
        ins += [x, x]
        specs += [pl.BlockSpec((HALO, k), lambda i, j: (jnp.maximum(i * hb - 1, 0), 0)),
                  pl.BlockSpec((HALO, k), lambda i, j: (jnp.minimum((i + 1) * hb, last), 0))]
    if pro == "adaln" or epi == "gres":
        ins.append(mod)
        specs.append(pl.BlockSpec((1, N_MOD, D_MODEL), lambda i, j: (i // per, 0, 0)))
    if pro == "rms":
        ins.append(nw.reshape(1, k))
        specs.append(pl.BlockSpec((1, k), lambda i, j: (0, 0)))
    if bias is not None:
        ins.append(bias.reshape(1, n))
        specs.append(pl.BlockSpec((1, tn), lambda i, j: (0, j)))
    if epi == "gres":
        ins.append(res)
        specs.append(pl.BlockSpec((tm, tn), lambda i, j: (i, j)))
    if conv:
        ins += [conv_w, conv_b.reshape(1, n)]
        specs += [pl.BlockSpec((3, tn), lambda i, j: (0, j)), pl.BlockSpec((1, tn), lambda i, j: (0, j))]
    return pl.pallas_call(
        functools.partial(_mm_kernel, pro=pro, epi=epi, has_bias=bias is not None, slot=slot, gslot=gslot,
                          ctx_row0=ctx_row0),
        grid=(rows // tm, n // tn),
        in_specs=specs,
        out_specs=pl.BlockSpec((tm, tn), lambda i, j: (i, j)),
        out_shape=jax.ShapeDtypeStruct((rows, n), out_dtype),
        scratch_shapes=[pltpu.VMEM((tm + 2 * HALO if conv else tm, k), BF16)],
        compiler_params=_params("parallel", "arbitrary"),
        name=name,
    )(*ins)


def _ffn_kernel(*refs, slot, final, fc):
    if final:
        x_ref, mod_ref, wi_ref, wo_ref, fw_ref, o_ref = refs
    else:
        x_ref, mod_ref, wi_ref, wo_ref, o_ref = refs
    x = x_ref[...]
    a = (_rms(x) * (1.0 + mod_ref[0, slot + 1:slot + 2, :]) + mod_ref[0, slot:slot + 1, :]).astype(BF16)
    acc = None
    for c in range(0, FFN_DIM, fc):
        g = _dot(a, wi_ref[:, c:c + fc])
        u = _dot(a, wi_ref[:, FFN_DIM + c:FFN_DIM + c + fc])
        t = _dot((_silu(g) * u).astype(BF16), wo_ref[c:c + fc, :])
        acc = t if acc is None else acc + t
    out = x + (0.5 * mod_ref[0, slot + 2:slot + 3, :]) * acc
    if final:
        out = _rms(out) * fw_ref[...]
    o_ref[...] = out


def _ffn(h, mod, w_in, w_out, layer, k, *, rows, slot, final_w=None, tm=512, fc=FFN_DIM):
    per = MOD_ROWS // tm
    final = final_w is not None
    resident = pl.Buffered(1)
    ins = [h, mod, w_in, w_out]
    specs = [pl.BlockSpec((tm, D_MODEL), lambda i: (i, 0)),
             pl.BlockSpec((1, N_MOD, D_MODEL), lambda i: (i // per, 0, 0)),
             pl.BlockSpec((None, None, D_MODEL, 2 * FFN_DIM), lambda i: (layer, k, 0, 0), pipeline_mode=resident),
             pl.BlockSpec((None, None, FFN_DIM, D_MODEL), lambda i: (layer, k, 0, 0), pipeline_mode=resident)]
    if final:
        ins.append(final_w.reshape(1, D_MODEL))
        specs.append(pl.BlockSpec((1, D_MODEL), lambda i: (0, 0)))
    return pl.pallas_call(
        functools.partial(_ffn_kernel, slot=slot, final=final, fc=fc),
        grid=(rows // tm,),
        in_specs=specs,
        out_specs=pl.BlockSpec((tm, D_MODEL), lambda i: (i, 0)),
        out_shape=jax.ShapeDtypeStruct((rows, D_MODEL), F32),
        compiler_params=_params("parallel"),
        name="ffn",
    )(*ins)


FILTER_W = 128


def _filter_kernel(za_ref, zb_ref, w1_ref, b1_ref, fr_ref, w2_ref, b2_ref, w3f_ref, w3b_ref, dl_ref,
                   sum_ref, diff_ref):
    fr = fr_ref[...]

    def taps(z_ref, w3_ref):
        z = z_ref[...]
        hid = jnp.sin(fr * (_dot(z, w1_ref[...], precision=HIGHEST) + b1_ref[...]))
        hid = jnp.sin(fr * (_dot(hid, w2_ref[...], precision=HIGHEST) + b2_ref[...]))
        window = jnp.exp(-z[:, 0:1] * dl_ref[...]) * z[:, FILTER_W - 1:FILTER_W]
        return _dot(hid.astype(BF16), w3_ref[...].astype(BF16)) * window

    hf = taps(za_ref, w3f_ref)
    hb = taps(zb_ref, w3b_ref)
    sum_ref[...] = (hf + hb).astype(BF16)
    diff_ref[...] = (hb - hf).astype(BF16)


def _hyena_filters(L, poly, w1, b1, freq, w2, b2, w3):
    S = L // poly
    nrho = 2 * poly - 1
    jj = jnp.arange(S, dtype=jnp.int32)
    rho = jnp.arange(-(poly - 1), poly, dtype=jnp.int32)
    t_tab = jnp.linspace(0.0, 1.0, L, dtype=F32)
    bands = jnp.linspace(1e-4, HYENA_BANDS - 1, HYENA_BANDS, dtype=F32)

    def feats(pos, lowest):
        valid = (pos >= lowest).astype(F32)
        pc = jnp.maximum(pos, 0)
        ang = (2.0 * math.pi / L) * pc.astype(F32)[..., None] * bands
        z = jnp.concatenate([jnp.take(t_tab, pc)[..., None], jnp.cos(ang), -jnp.sin(ang)], axis=-1)
        z = jnp.pad(z, ((0, 0), (0, 0), (0, FILTER_W - 1 - HYENA_EMB)))
        return jnp.concatenate([z, valid[..., None]], axis=-1).reshape(nrho * S, FILTER_W)

    za = feats(poly * jj[None, :] + rho[:, None], 0)
    zb = feats(poly * jj[None, :] - rho[:, None], 1)
    ph = FILTER_W - HYENA_FILTER_HIDDEN
    w1 = jnp.pad(w1, ((0, FILTER_W - HYENA_EMB), (0, ph)))
    w2 = jnp.pad(w2, ((0, ph), (0, ph)))
    w3 = jnp.pad(w3, ((0, ph), (0, 0))).reshape(FILTER_W, 2, 2, D_MODEL)
    w3f = w3[:, :, 0].reshape(FILTER_W, 2 * D_MODEL)
    w3b = w3[:, :, 1].reshape(FILTER_W, 2 * D_MODEL)
    b1, b2, freq = (jnp.pad(t, (0, ph)).reshape(1, FILTER_W) for t in (b1, b2, freq))
    max_decay = math.log(HYENA_DECAY_TARGET) / HYENA_FAST_DECAY
    min_decay = math.log(HYENA_DECAY_TARGET) / HYENA_SLOW_DECAY
    deltas = jnp.abs(jnp.linspace(min_decay, max_decay, D_MODEL, dtype=F32))
    deltas = jnp.tile(deltas, 2).reshape(1, 2 * D_MODEL)
    full = lambda shape: pl.BlockSpec(shape, lambda i: (0, 0))
    sq = full((FILTER_W, FILTER_W))
    vec = full((1, FILTER_W))
    return pl.pallas_call(
        _filter_kernel,
        grid=(nrho,),
        in_specs=[pl.BlockSpec((S, FILTER_W), lambda i: (i, 0)), pl.BlockSpec((S, FILTER_W), lambda i: (i, 0)),
                  sq, vec, vec, sq, vec, full((FILTER_W, 2 * D_MODEL)), full((FILTER_W, 2 * D_MODEL)),
                  full((1, 2 * D_MODEL))],
        out_specs=[pl.BlockSpec((S, 2 * D_MODEL), lambda i: (i, 0))] * 2,
        out_shape=[jax.ShapeDtypeStruct((nrho * S, 2 * D_MODEL), BF16)] * 2,
        compiler_params=_params("parallel"),
        name="hyena_filter",
    )(za, zb, w1, b1, freq, w2, b2, w3f, w3b, deltas)


def _lmm_kernel(*refs, epi, tf, scale, has_dest):
    it = iter(refs)
    a_ref, x_ref = next(it), next(it)
    if epi == "spec":
        k_ref = next(it)
    elif epi == "gate":
        g_ref, v_ref, bias_ref = next(it), next(it), next(it)
    if has_dest:
        next(it)
    o_ref, xb_ref = next(it), next(it)

    @pl.when(pl.program_id(2) == 0)
    def _():
        xb_ref[...] = x_ref[...].astype(BF16)

    acc = _dot(a_ref[...], xb_ref[...])
    if epi == "spec":
        xr, xs = acc[:tf], acc[tf:]
        kr, ki = k_ref[0:tf, :], k_ref[tf:2 * tf, :]
        o_ref[0:tf, :] = ((xr * kr + xs * ki) * scale).astype(o_ref.dtype)
        o_ref[tf:2 * tf, :] = ((xs * kr - xr * ki) * scale).astype(o_ref.dtype)
    elif epi == "gate":
        o_ref[...] = (g_ref[...] * (acc + v_ref[...].astype(F32) * bias_ref[...])).astype(o_ref.dtype)
    else:
        o_ref[...] = acc.astype(o_ref.dtype)


def _lmm(a, x, *, nb, ncb, tm, n, x_row0=0, x_cb0=0, epi="none", tf=0, scale=1.0, kspec=None, k_cb=0,
         g=None, g_cb=0, g_row0=0, v=None, v_cb=0, v_row0=0, bias=None, out_dtype=F32, dest=None, dest_row0=0,
         name="lmm"):
    mo, k = a.shape
    nm = mo // tm
    xr0 = x_row0 // k
    or0 = dest_row0 // tm
    ins = [a, x]
    specs = [pl.BlockSpec((tm, k), lambda b, c, m: (m, 0)),
             pl.BlockSpec((k, n), lambda b, c, m: (xr0 + b, x_cb0 + c))]
    if epi == "spec":
        ins.append(kspec)
        specs.append(pl.BlockSpec((tm, n), lambda b, c, m: (m, k_cb)))
    elif epi == "gate":
        gr0, vr0 = g_row0 // tm, v_row0 // tm
        ins += [g, v, bias.reshape(1, n)]
        specs += [pl.BlockSpec((tm, n), lambda b, c, m: (gr0 + b * nm + m, g_cb)),
                  pl.BlockSpec((tm, n), lambda b, c, m: (vr0 + b * nm + m, v_cb)),
                  pl.BlockSpec((1, n), lambda b, c, m: (0, 0))]
    aliases = {}
    out_shape = jax.ShapeDtypeStruct((nb * mo, ncb * n), out_dtype)
    if dest is not None:
        aliases = {len(ins): 0}
        ins.append(dest)
        specs.append(pl.BlockSpec(memory_space=pl.ANY))
        out_shape = jax.ShapeDtypeStruct(dest.shape, dest.dtype)
    return pl.pallas_call(
        functools.partial(_lmm_kernel, epi=epi, tf=tf, scale=scale, has_dest=dest is not None),
        grid=(nb, ncb, nm),
        in_specs=specs,
        out_specs=pl.BlockSpec((tm, n), lambda b, c, m: (or0 + b * nm + m, c)),
        out_shape=out_shape,
        scratch_shapes=[pltpu.VMEM((k, n), BF16)],
        input_output_aliases=aliases,
        compiler_params=_params("parallel", "parallel", "arbitrary"),
        name=name,
    )(*ins)


def _dft_mats(L, tf):
    idx = jnp.arange(L, dtype=jnp.int32)
    ph = ((2 * idx[:, None] + 1) * idx[None, :]) % (4 * L)
    ang = ph.astype(F32) * (2.0 * math.pi / (4 * L))
    c, s = jnp.cos(ang), jnp.sin(ang)
    nf = L // tf
    fwd = jnp.stack([c.reshape(nf, tf, L), s.reshape(nf, tf, L)], axis=1).reshape(2 * L, L)
    inv = jnp.stack([c.T.reshape(L, nf, tf), s.T.reshape(L, nf, tf)], axis=2).reshape(L, 2 * L)
    return c.astype(BF16), s.astype(BF16), fwd.astype(BF16), inv.astype(BF16)


def _interleave(kr, ki, tf):
    L, n = kr.shape
    return jnp.stack([kr.reshape(L // tf, tf, n), ki.reshape(L // tf, tf, n)], axis=1).reshape(2 * L, n)


def _hyena_core(u, L, nb, row0, fparams, hy_bias, name, dest=None):
    tf = min(L, 256)
    tmi = min(L, 256)
    cmat, smat, fwd, inv = _dft_mats(L, tf)
    hsum, hdiff = _hyena_filters(L, 1, *fparams)
    kr = _lmm(cmat, hsum, nb=1, ncb=2, tm=tf, n=D_MODEL, name=name + "_kr")
    ki = _lmm(smat, hdiff, nb=1, ncb=2, tm=tf, n=D_MODEL, name=name + "_ki")
    kspec = _interleave(kr, ki, tf)
    scale = 1.0 / L
    y1 = _lmm(fwd, u, nb=nb, ncb=1, tm=2 * tf, n=D_MODEL, x_row0=row0, x_cb0=2, epi="spec", tf=tf, scale=scale,
              kspec=kspec, k_cb=0, out_dtype=BF16, name=name + "_fwd1")
    z = _lmm(inv, y1, nb=nb, ncb=1, tm=tmi, n=D_MODEL, epi="gate", g=u, g_cb=0, g_row0=row0, v=u, v_cb=2,
             v_row0=row0, bias=hy_bias[0], out_dtype=F32, name=name + "_inv1")
    y2 = _lmm(fwd, z, nb=nb, ncb=1, tm=2 * tf, n=D_MODEL, epi="spec", tf=tf, scale=scale,
              kspec=kspec, k_cb=1, out_dtype=BF16, name=name + "_fwd2")
    return _lmm(inv, y2, nb=nb, ncb=1, tm=tmi, n=D_MODEL, epi="gate", g=u, g_cb=1, g_row0=row0, v=z, v_cb=0,
                bias=hy_bias[1], out_dtype=F32, dest=dest, dest_row0=row0, name=name + "_inv2")


POLY = 4


def _poly_conv_kernel(a_ref, ai_ref, u_ref, kr_ref, ki_ref, g_ref, bias_ref, o_ref, *, scale):
    S = a_ref.shape[1]
    tc = u_ref.shape[1]
    phases = [u_ref[pl.ds(r, S, stride=POLY), :] for r in range(POLY)]
    e = _dot(a_ref[...], jnp.concatenate([ph.astype(BF16) for ph in phases], axis=1))
    f = []
    for r in range(POLY):
        fre = fim = None
        for rp in range(POLY):
            kr = kr_ref[r - rp + POLY - 1]
            ki = ki_ref[r - rp + POLY - 1]
            er = e[0:S, rp * tc:(rp + 1) * tc]
            es = e[S:2 * S, rp * tc:(rp + 1) * tc]
            tre = kr * er + ki * es
            tim = kr * es - ki * er
            fre = tre if fre is None else fre + tre
            fim = tim if fim is None else fim + tim
        f.append(jnp.concatenate([fre, fim], axis=0).astype(BF16))
    y = _dot(ai_ref[...], jnp.concatenate(f, axis=1)) * scale
    for r in range(POLY):
        out = g_ref[pl.ds(r, S, stride=POLY), :] * (y[:, r * tc:(r + 1) * tc] + phases[r] * bias_ref[...])
        o_ref[pl.ds(r, S, stride=POLY), :] = out


def _poly_conv(fwd, inv, x, x_cb, g, g_cb, kre, kim, order, bias, *, nb, out_rows=None, tc=128, name="hy_conv"):
    L = MOD_ROWS
    S = L // POLY
    nct = D_MODEL // tc
    nrho = 2 * POLY - 1
    return pl.pallas_call(
        functools.partial(_poly_conv_kernel, scale=1.0 / S),
        grid=(nct, nb),
        in_specs=[pl.BlockSpec((2 * S, S), lambda c, b: (0, 0)),
                  pl.BlockSpec((S, 2 * S), lambda c, b: (0, 0)),
                  pl.BlockSpec((L, tc), lambda c, b: (b, x_cb * nct + c)),
                  pl.BlockSpec((nrho, S, tc), lambda c, b: (0, 0, order * nct + c)),
                  pl.BlockSpec((nrho, S, tc), lambda c, b: (0, 0, order * nct + c)),
                  pl.BlockSpec((L, tc), lambda c, b: (b, g_cb * nct + c)),
                  pl.BlockSpec((1, tc), lambda c, b: (0, c))],
        out_specs=pl.BlockSpec((L, tc), lambda c, b: (b, c)),
        out_shape=jax.ShapeDtypeStruct((out_rows or nb * L, D_MODEL), F32),
        compiler_params=_params("parallel", "parallel"),
        name=name,
    )(fwd, inv, x, kre, kim, g, bias.reshape(1, D_MODEL))


def _hyena_core_poly(u, nb, fparams, hy_bias, out_rows):
    S = MOD_ROWS // POLY
    nrho = 2 * POLY - 1
    cmat, smat, fwd, inv = _dft_mats(S, S)
    ksum, kdiff = _hyena_filters(MOD_ROWS, POLY, *fparams)
    kre = _lmm(cmat, ksum, nb=nrho, ncb=2, tm=S, n=D_MODEL, name="hy_lat_kr").reshape(nrho, S, 2 * D_MODEL)
    kim = _lmm(smat, kdiff, nb=nrho, ncb=2, tm=S, n=D_MODEL, name="hy_lat_ki").reshape(nrho, S, 2 * D_MODEL)
    z = _poly_conv(fwd, inv, u, 2, u, 0, kre, kim, 0, hy_bias[0], nb=nb, name="hy_lat_conv1")
    return _poly_conv(fwd, inv, z, 0, u, 1, kre, kim, 1, hy_bias[1], nb=nb, out_rows=out_rows, name="hy_lat_conv2")


def _softplus(x):
    return jnp.maximum(x, 0.0) + jnp.log(1.0 + jnp.exp(-jnp.abs(x)))


def _ssd_kernel(xs_f, b_f, c_f, dt_f, dtt_f, xs_b, b_b, c_b, dt_b, dtt_b, dtb_ref, dtbt_ref, al_ref, alt_ref,
                yf_ref, yb_ref, st_ref):
    @pl.when(pl.program_id(1) == 0)
    def _():
        st_ref[...] = jnp.zeros_like(st_ref)

    _ssd_chunk(0, xs_f, b_f, c_f, dt_f, dtt_f, dtb_ref[0], dtbt_ref[0], al_ref[0], alt_ref[0], yf_ref, st_ref.at[0])
    _ssd_chunk(1, xs_b, b_b, c_b, dt_b, dtt_b, dtb_ref[1], dtbt_ref[1], al_ref[1], alt_ref[1], yb_ref, st_ref.at[1])


def _ssd_chunk(d, xs_ref, b_ref, c_ref, dt_ref, dtt_ref, dt_bias, dt_bias_t, a_log, a_log_t, y_ref, st_ref):
    T = SSM_CHUNK
    N = SSM_STATE
    ri = lax.broadcasted_iota(jnp.int32, (T, T), 0)
    ci = lax.broadcasted_iota(jnp.int32, (T, T), 1)
    mask = (ri >= ci) if d == 0 else (ri <= ci)
    tri_col = mask.astype(F32)
    tri_row = ((ci >= ri) if d == 0 else (ci <= ri)).astype(F32)
    low = lax.broadcasted_iota(jnp.int32, (T, 2 * SSM_HEAD_DIM), 1) < SSM_HEAD_DIM

    a = _softplus(dt_ref[...] + dt_bias) * -jnp.exp(a_log)
    dtt = _softplus(dtt_ref[...] + dt_bias_t)
    at = dtt * -jnp.exp(a_log_t)
    cs_col = _dot(tri_col, a, precision=HIGHEST)
    cs_row = _dot(at, tri_row, precision=HIGHEST)
    tot = jnp.sum(at, axis=1, keepdims=True)
    row_dt = cs_row - jnp.log(dtt)
    w_out = dtt * jnp.exp(tot - cs_row)
    e_tot = jnp.exp(tot)

    for g in range(SSM_GROUPS):
        bt = jnp.transpose(b_ref[:, g * N:(g + 1) * N].astype(F32))
        cg = c_ref[:, g * N:(g + 1) * N].astype(F32)
        cb = _dot(cg.astype(BF16), bt.astype(BF16))
        for k in range(HEADS_PER_GROUP // 2):
            h0 = g * HEADS_PER_GROUP + 2 * k
            lanes = slice(h0 * SSM_HEAD_DIM, (h0 + 2) * SSM_HEAD_DIM)
            xs = xs_ref[:, lanes].astype(BF16)
            st = st_ref[g, :, k * 128:(k + 1) * 128]
            rhs = jnp.concatenate([xs, st.astype(BF16)], axis=0)
            ys, upds = [], []
            for h in (h0, h0 + 1):
                col = jnp.broadcast_to(cs_col[:, h:h + 1], (T, T))
                m = jnp.where(mask, jnp.exp(col - row_dt[h:h + 1, :]), 0.0) * cb
                lhs = jnp.concatenate([m, jnp.exp(col) * cg], axis=1).astype(BF16)
                ys.append(_dot(lhs, rhs))
                upds.append(_dot((bt * w_out[h:h + 1, :]).astype(BF16), xs))
            y_ref[:, lanes] = jnp.where(low, ys[0], ys[1]).astype(y_ref.dtype)
            decay = jnp.where(low, e_tot[h0:h0 + 1, :], e_tot[h0 + 1:h0 + 2, :])
            st_ref[g, :, k * 128:(k + 1) * 128] = st * decay + jnp.where(low, upds[0], upds[1])


def _ssd(xbc, dta, dtt, dt_bias, a_log, *, nb):
    m = xbc.shape[0]
    T = SSM_CHUNK
    H = SSM_HEADS
    lat_c = MOD_ROWS // T
    ctx_c = CTX_LEN // T
    ctx0 = nb * lat_c
    steps = lat_c + ctx_c

    def rb(b, d, t):
        ctx = ctx0 + ctx_c * b + jnp.where(d == 0, t, ctx_c - 1 - t)
        lat = lat_c * b + jnp.where(d == 0, t - ctx_c, steps - 1 - t)
        return jnp.where(t < ctx_c, ctx, lat)

    def direction(d):
        return [pl.BlockSpec((T, SSM_INNER), lambda b, t: (rb(b, d, t), 0)),
                pl.BlockSpec((T, SSM_BC_DIM), lambda b, t: (rb(b, d, t), 2)),
                pl.BlockSpec((T, SSM_BC_DIM), lambda b, t: (rb(b, d, t), 3)),
                pl.BlockSpec((None, T, H), lambda b, t: (d, rb(b, d, t), 0)),
                pl.BlockSpec((None, H, T), lambda b, t: (d, 0, rb(b, d, t)))]

    whole = lambda shape: pl.BlockSpec(shape, lambda b, t: (0, 0, 0))
    return pl.pallas_call(
        _ssd_kernel,
        grid=(nb, steps),
        in_specs=direction(0) + direction(1) + [whole((2, 1, H)), whole((2, H, 1)), whole((2, 1, H)), whole((2, H, 1))],
        out_specs=[pl.BlockSpec((T, SSM_INNER), lambda b, t: (rb(b, 0, t), 0)),
                   pl.BlockSpec((T, SSM_INNER), lambda b, t: (rb(b, 1, t), 0))],
        out_shape=[jax.ShapeDtypeStruct((m, SSM_INNER), BF16)] * 2,
        scratch_shapes=[pltpu.VMEM((2, SSM_GROUPS, SSM_STATE, HEADS_PER_GROUP * SSM_HEAD_DIM), F32)],
        compiler_params=_params("parallel", "arbitrary"),
        name="ssd_scan",
    )(xbc, xbc, xbc, dta, dtt, xbc, xbc, xbc, dta, dtt, dt_bias.reshape(2, 1, H), dt_bias.reshape(2, H, 1),
      a_log.reshape(2, 1, H), a_log.reshape(2, H, 1))


def _ssd_finish_kernel(yf_ref, yb_ref, xs_ref, z_ref, dsk_ref, nw_ref, o_ref):
    gw = SSM_INNER // SSM_GROUPS
    z = z_ref[...]
    y = yf_ref[...].astype(F32) + yb_ref[...].astype(F32) + dsk_ref[...] * xs_ref[...].astype(F32)
    y = y * _silu(z)
    for g in range(SSM_GROUPS):
        yg = _rms(y[:, g * gw:(g + 1) * gw])
        o_ref[:, g * gw:(g + 1) * gw] = (yg * nw_ref[:, g * gw:(g + 1) * gw]).astype(o_ref.dtype)


def _ssd_finish(yf, yb, xbc, zx, d_skip, norm_w, *, rows, tm=256):
    n = SSM_INNER
    return pl.pallas_call(
        _ssd_finish_kernel,
        grid=(rows // tm,),
        in_specs=[pl.BlockSpec((tm, n), lambda i: (i, 0)),
                  pl.BlockSpec((tm, n), lambda i: (i, 0)),
                  pl.BlockSpec((tm, n), lambda i: (i, 0)),
                  pl.BlockSpec((tm, n), lambda i: (i, 0)),
                  pl.BlockSpec((1, n), lambda i: (0, 0)),
                  pl.BlockSpec((1, n), lambda i: (0, 0))],
        out_specs=pl.BlockSpec((tm, n), lambda i: (i, 0)),
        out_shape=jax.ShapeDtypeStruct((rows, n), BF16),
        compiler_params=_params("parallel"),
        name="ssd_finish",
    )(yf, yb, xbc, zx, jnp.repeat(d_skip, SSM_HEAD_DIM).reshape(1, n), norm_w.reshape(1, n))


HEAD_W = 2 * MLA_NOPE
ATTN_SUB_ROWS = 256


def _attn_kernel(*refs, has_lat):
    if has_lat:
        q_ref, kvc_ref, krc_ref, kv_ref, kr_ref, tq_ref, tk_ref, o_ref, kc_scr, vc_scr, k_scr, v_scr = refs
    else:
        q_ref, kvc_ref, krc_ref, _, o_ref, kc_scr, vc_scr = refs
    R = MLA_ROPE
    kscale = MLA_SCALE * math.log2(math.e)

    def build_keys():
        krc = krc_ref[...].astype(F32)
        lane_c = lax.broadcasted_iota(jnp.int32, krc.shape, 1)
        pe_c = jnp.where((lane_c >= MLA_NOPE) & (lane_c < MLA_NOPE + R), pltpu.roll(krc, MLA_NOPE, 1), 0.0)
        for hh in range(2):
            kvh = kvc_ref[:, hh * HEAD_W:(hh + 1) * HEAD_W].astype(F32)
            kc_scr[hh] = (jnp.where(lane_c < MLA_NOPE, kvh, pe_c) * kscale).astype(BF16)
            vc_scr[hh] = jnp.where(lane_c < MLA_NOPE, 1.0, kvh).astype(BF16)
        if has_lat:
            t = kr_ref[...].astype(F32) * tk_ref[...]
            lane = lax.broadcasted_iota(jnp.int32, t.shape, 1)
            krot = jnp.where(lane < R, t + pltpu.roll(t, HEAD_W - R, 1), 0.0)
            krr = pltpu.roll(krot, MLA_NOPE, 1) + pltpu.roll(krot, MLA_NOPE + R, 1)
            for hh in range(2):
                kvh = kv_ref[:, hh * HEAD_W:(hh + 1) * HEAD_W].astype(F32)
                k_scr[hh] = (jnp.where(lane < MLA_NOPE, kvh, krr) * kscale).astype(BF16)
                v_scr[hh] = jnp.where(lane < MLA_NOPE, 1.0, kvh).astype(BF16)

    if has_lat:
        pl.when(pl.program_id(2) == 0)(build_keys)
    else:
        build_keys()

    sub = min(q_ref.shape[0], ATTN_SUB_ROWS)
    for r0 in range(0, q_ref.shape[0], sub):
        rows = slice(r0, r0 + sub)
        res = []
        for hh in range(2):
            q = q_ref[rows, hh * HEAD_W:(hh + 1) * HEAD_W]
            s_c = _dot_nt(q, kc_scr[hh])
            mx = jnp.max(s_c, axis=-1, keepdims=True)
            if has_lat:
                ql = (q.astype(F32) * tq_ref[rows, :]).astype(BF16)
                s_l = _dot_nt(ql, k_scr[hh])
                mx = jnp.maximum(mx, jnp.max(s_l, axis=-1, keepdims=True))
                acc = _dot(jnp.exp2(s_l - mx).astype(BF16), v_scr[hh])
                acc = acc + _dot(jnp.exp2(s_c - mx).astype(BF16), vc_scr[hh])
            else:
                acc = _dot(jnp.exp2(s_c - mx).astype(BF16), vc_scr[hh])
            res.append(acc / pltpu.roll(acc, MLA_V, 1))
        lane_o = lax.broadcasted_iota(jnp.int32, res[0].shape, 1)
        o_ref[rows, :] = jnp.where(lane_o < MLA_V, pltpu.roll(res[0], MLA_V, 1), res[1]).astype(o_ref.dtype)


def _attention(q, kv, dn, cs, *, nb, tq=512):
    L = MOD_ROWS
    hp = MLA_HEADS // 2
    w = 2 * HEAD_W
    nq = L // tq
    cb = nb * (L // CTX_LEN)
    kr_cb = (MLA_Q_RANK + MLA_KV_RANK) // HEAD_W
    tab_q = jnp.concatenate([jnp.ones((L, MLA_NOPE), F32), cs], axis=1)
    tab_k = jnp.concatenate([cs, jnp.zeros((L, HEAD_W - 2 * MLA_ROPE), F32)], axis=1)
    lat = pl.pallas_call(
        functools.partial(_attn_kernel, has_lat=True),
        grid=(nb, hp, nq),
        in_specs=[pl.BlockSpec((tq, w), lambda b, p, i: (b * nq + i, p)),
                  pl.BlockSpec((CTX_LEN, w), lambda b, p, i: (cb + b, p)),
                  pl.BlockSpec((CTX_LEN, HEAD_W), lambda b, p, i: (cb + b, kr_cb)),
                  pl.BlockSpec((L, w), lambda b, p, i: (b, p)),
                  pl.BlockSpec((L, HEAD_W), lambda b, p, i: (b, kr_cb)),
                  pl.BlockSpec((tq, HEAD_W), lambda b, p, i: (i, 0)),
                  pl.BlockSpec((L, HEAD_W), lambda b, p, i: (0, 0))],
        out_specs=pl.BlockSpec((tq, 2 * MLA_V), lambda b, p, i: (b * nq + i, p)),
        out_shape=jax.ShapeDtypeStruct((nb * (L + CTX_LEN), MLA_HEADS * MLA_V), BF16),
        scratch_shapes=[pltpu.VMEM((2, CTX_LEN, HEAD_W), BF16)] * 2 + [pltpu.VMEM((2, L, HEAD_W), BF16)] * 2,
        compiler_params=_params("parallel", "parallel", "arbitrary"),
        name="mla_attn",
    )(q, kv, dn, kv, dn, tab_q, tab_k)
    return pl.pallas_call(
        functools.partial(_attn_kernel, has_lat=False),
        grid=(nb, hp),
        in_specs=[pl.BlockSpec((CTX_LEN, w), lambda b, p: (cb + b, p)),
                  pl.BlockSpec((CTX_LEN, w), lambda b, p: (cb + b, p)),
                  pl.BlockSpec((CTX_LEN, HEAD_W), lambda b, p: (cb + b, kr_cb)),
                  pl.BlockSpec(memory_space=pl.ANY)],
        out_specs=pl.BlockSpec((CTX_LEN, 2 * MLA_V), lambda b, p: (cb + b, p)),
        out_shape=jax.ShapeDtypeStruct(lat.shape, lat.dtype),
        scratch_shapes=[pltpu.VMEM((2, CTX_LEN, HEAD_W), BF16)] * 2,
        input_output_aliases={3: 0},
        compiler_params=_params("parallel", "parallel"),
        name="mla_attn_ctx",
    )(q, kv, dn, lat)


def _rot_cols(w):
    wp = w.reshape(w.shape[:-1] + (2, 2, ROPE_AXIS // 2))
    return jnp.stack([-wp[..., 1, :], wp[..., 0, :]], axis=-2).reshape(w.shape)


def _rope_table(n_lat):
    rows = n_lat // GRID_W
    row = jnp.repeat(jnp.arange(rows), GRID_W)
    col = jnp.tile(jnp.arange(GRID_W), rows)
    inv = 1.0 / (ROPE_BASE ** (jnp.arange(0, ROPE_AXIS, 2, dtype=F32) / ROPE_AXIS))
    ar = row.astype(F32)[:, None] * inv[None, :]
    ac = col.astype(F32)[:, None] * inv[None, :]
    ang = jnp.concatenate([ar, ar, ac, ac], axis=-1)
    return jnp.concatenate([jnp.cos(ang), jnp.sin(ang)], axis=-1)


def _hyena_layer(h, mod, rows, nb, j, with_ctx, p):
    u = _mm(h, p["hy_in_w"][j].astype(BF16), rows=rows, tn=1024, pro="adaln", mod=mod, slot=3,
            bias=p["hy_in_b"][j], epi="conv", conv_w=p["hy_conv_w"][j], conv_b=p["hy_conv_b"][j],
            ctx_row0=nb * MOD_ROWS, name="hyena_in")
    fparams = (p["hy_pos_w1"][j], p["hy_pos_b1"][j], p["hy_freq"][j], p["hy_pos_w2"][j], p["hy_pos_b2"][j],
               p["hy_pos_w3"][j])
    y = _hyena_core_poly(u, nb, fparams, p["hy_bias"][j], rows)
    if with_ctx:
        y = _hyena_core(u, CTX_LEN, nb, nb * MOD_ROWS, fparams, p["hy_bias"][j], "hy_ctx", dest=y)
    return _mm(y, p["hy_out_w"][j].astype(BF16), rows=rows, tn=D_MODEL, epi="gres", mod=mod, gslot=5,
               bias=p["hy_out_b"][j], res=h, name="hyena_out")


def _mamba_layer(h, mod, rows, nb, j, p):
    w_in = p["mb_in_w"][j].astype(BF16)
    w_zdt = jnp.concatenate([w_in[:, :SSM_INNER], w_in[:, SSM_INNER + SSM_CONV_DIM:],
                             jnp.zeros((D_MODEL, SSM_ZDT_PAD - SSM_INNER - 2 * SSM_HEADS), BF16)], axis=1)
    zx = _mm(h, w_zdt, rows=rows, tn=SSM_ZDT_PAD, pro="adaln", mod=mod, slot=3, name="mamba_in_zdt")
    xbc = _mm(h, w_in[:, SSM_INNER:SSM_INNER + SSM_CONV_DIM], rows=rows, tn=1024, pro="adaln", mod=mod, slot=3,
              epi="conv_silu", conv_w=p["mb_conv_w"][j], conv_b=p["mb_conv_b"][j], ctx_row0=nb * MOD_ROWS,
              out_dtype=BF16, name="mamba_in_xbc")
    dtr = zx[:, SSM_INNER:SSM_INNER + 2 * SSM_HEADS].reshape(rows, 2, SSM_HEADS)
    dta = jnp.transpose(dtr, (1, 0, 2))
    dtt = jnp.transpose(dtr, (1, 2, 0))
    yf, yb = _ssd(xbc, dta, dtt, p["mb_dt_bias"][j], p["mb_A_log"][j], nb=nb)
    yn = _ssd_finish(yf, yb, xbc, zx, p["mb_D"][j], p["mb_norm_w"][j], rows=rows)
    return _mm(yn, p["mb_out_w"][j].astype(BF16), rows=rows, tn=D_MODEL, epi="gres", mod=mod, gslot=5, res=h,
               name="mamba_out")


def _mla_layer(h, mod, rows, nb, j, cs, p):
    wd = p["mla_w_down"][j]
    kpe_w = wd[:, MLA_Q_RANK + MLA_KV_RANK:]
    wd = jnp.concatenate([wd, _rot_cols(kpe_w),
                          jnp.zeros((D_MODEL, MLA_DOWN_PAD - wd.shape[1] - MLA_ROPE), F32)], axis=1).astype(BF16)
    wq = p["mla_w_uq"][j].reshape(MLA_Q_RANK, MLA_HEADS, MLA_QK)
    wq = jnp.concatenate([wq, _rot_cols(wq[..., MLA_NOPE:])], axis=-1).reshape(MLA_Q_RANK, -1).astype(BF16)
    dn = _mm(h, wd, rows=rows, tn=MLA_DOWN_PAD, pro="adaln", mod=mod, slot=3, out_dtype=F32, name="mla_down")
    q = _mm(dn, wq, rows=rows, tn=1024, pro="rms", nw=p["mla_q_norm"][j], out_dtype=BF16, name="mla_uq")
    kv = _mm(dn, p["mla_w_ukv"][j].astype(BF16), rows=rows, tn=1024, x_colblock=MLA_Q_RANK // MLA_KV_RANK,
             pro="rms", nw=p["mla_kv_norm"][j], out_dtype=BF16, name="mla_ukv")
    o = _attention(q, kv, dn, cs, nb=nb)
    return _mm(o, p["mla_w_o"][j].astype(BF16), rows=rows, tn=D_MODEL, epi="gres", mod=mod, gslot=5, res=h,
               name="mla_out")


def kernel(x, c, ctx, c_ctx, ada_w, ada_b, ffn_in, ffn_out, hy_in_w, hy_in_b, hy_conv_w, hy_conv_b, hy_pos_w1, hy_pos_b1, hy_freq, hy_pos_w2, hy_pos_b2, hy_pos_w3, hy_bias, hy_out_w, hy_out_b, mb_in_w, mb_conv_w, mb_conv_b, mb_dt_bias, mb_A_log, mb_D, mb_norm_w, mb_out_w, mla_w_down, mla_q_norm, mla_w_uq, mla_kv_norm, mla_w_ukv, mla_w_o, final_norm_w):
    p = dict(hy_in_w=hy_in_w, hy_in_b=hy_in_b, hy_conv_w=hy_conv_w, hy_conv_b=hy_conv_b, hy_pos_w1=hy_pos_w1,
             hy_pos_b1=hy_pos_b1, hy_freq=hy_freq, hy_pos_w2=hy_pos_w2, hy_pos_b2=hy_pos_b2, hy_pos_w3=hy_pos_w3,
             hy_bias=hy_bias, hy_out_w=hy_out_w, hy_out_b=hy_out_b, mb_in_w=mb_in_w, mb_conv_w=mb_conv_w,
             mb_conv_b=mb_conv_b, mb_dt_bias=mb_dt_bias, mb_A_log=mb_A_log, mb_D=mb_D, mb_norm_w=mb_norm_w,
             mb_out_w=mb_out_w, mla_w_down=mla_w_down, mla_q_norm=mla_q_norm, mla_w_uq=mla_w_uq,
             mla_kv_norm=mla_kv_norm, mla_w_ukv=mla_w_ukv, mla_w_o=mla_w_o)
    nb, n_lat, _ = x.shape
    assert n_lat == MOD_ROWS and nb * ctx.shape[1] == MOD_ROWS and ctx.shape[1] == CTX_LEN
    lat_rows = nb * n_lat
    all_rows = lat_rows + MOD_ROWS
    cs = _rope_table(n_lat)

    cc = jnp.concatenate([c, c_ctx[None], jnp.zeros((16 - nb - 1, D_MODEL), F32)], axis=0)
    mods = _mod_all(cc, ada_w, ada_b).reshape(DEPTH, 16, N_MOD, D_MODEL)
    h = jnp.concatenate([x.reshape(lat_rows, D_MODEL), ctx.reshape(MOD_ROWS, D_MODEL)], axis=0)
    w_ffn_in, w_ffn_out = ffn_in.astype(BF16), ffn_out.astype(BF16)

    for i in range(DEPTH):
        kind, j, last = i % N_MIXERS, i // N_MIXERS, i == DEPTH - 1
        mod = mods[i]
        ctx_needed = not (last and kind == 0)
        ctx_out = not last
        rows = all_rows if ctx_needed else lat_rows
        h = _ffn(h, mod, w_ffn_in, w_ffn_out, i, 0, rows=rows, slot=0)
        if kind == 0:
            h = _hyena_layer(h, mod, rows, nb, j, ctx_out, p)
        elif kind == 1:
            h = _mamba_layer(h, mod, rows, nb, j, p)
        else:
            h = _mla_layer(h, mod, rows, nb, j, cs, p)
        rows = all_rows if ctx_out else lat_rows
        h = _ffn(h, mod, w_ffn_in, w_ffn_out, i, 1, rows=rows, slot=6, final_w=final_norm_w if last else None)
    return h[:lat_rows].reshape(nb, n_lat, D_MODEL)
```

```python
import functools
import math

import jax
import jax.numpy as jnp
from jax import lax
from jax.experimental import pallas as pl
from jax.experimental.pallas import tpu as pltpu

F32 = jnp.float32
BF16 = jnp.bfloat16
HIGHEST = lax.Precision.HIGHEST

D_MODEL = 1024
DEPTH = 4
GRID_W = 64
CTX_LEN = 256
N_MIXERS = 3
N_MOD = 9
FFN_DIM = 2816
EPS = 1e-6

HYENA_EMB = 33
HYENA_BANDS = (HYENA_EMB - 1) // 2
HYENA_FILTER_HIDDEN = 64
HYENA_FAST_DECAY = 0.3
HYENA_SLOW_DECAY = 1.5
HYENA_DECAY_TARGET = 1e-2

SSM_INNER = 2 * D_MODEL
SSM_HEAD_DIM = 64
SSM_HEADS = SSM_INNER // SSM_HEAD_DIM
SSM_GROUPS = 8
HEADS_PER_GROUP = SSM_HEADS // SSM_GROUPS
SSM_STATE = 128
SSM_CHUNK = 128
SSM_BC_DIM = SSM_GROUPS * SSM_STATE
SSM_CONV_DIM = SSM_INNER + 2 * SSM_BC_DIM
SSM_IN_DIM = SSM_INNER + SSM_CONV_DIM + 2 * SSM_HEADS
SSM_ZDT_PAD = SSM_INNER + 128

MLA_HEADS = 16
MLA_NOPE = 64
MLA_ROPE = 32
MLA_V = 64
MLA_Q_RANK = 768
MLA_KV_RANK = 256
MLA_QK = MLA_NOPE + MLA_ROPE
MLA_SCALE = MLA_QK ** -0.5
MLA_DOWN_PAD = 1152
ROPE_AXIS = MLA_ROPE // 2
ROPE_BASE = 10000.0

MOD_ROWS = 2048
VMEM_LIMIT_BYTES = 56 * 1024 * 1024


def _params(*sem):
    return pltpu.CompilerParams(dimension_semantics=sem, vmem_limit_bytes=VMEM_LIMIT_BYTES)


def _rms(x):
    return x * lax.rsqrt(jnp.mean(x * x, axis=-1, keepdims=True) + EPS)


def _silu(x):
    return x * jax.nn.sigmoid(x)


def _dot(a, b, **kw):
    return jnp.dot(a, b, preferred_element_type=F32, **kw)


def _dot_nt(a, b):
    return lax.dot_general(a, b, (((1,), (1,)), ((), ())), preferred_element_type=F32)


def _mod_kernel(x_ref, w_ref, b_ref, o_ref):
    a = _silu(x_ref[...]).astype(BF16)
    o_ref[...] = _dot(a, w_ref[...].astype(BF16)) + b_ref[...]


def _mod_all(cc, ada_w, ada_b):
    n = N_MOD * D_MODEL
    tn = 1024
    return pl.pallas_call(
        _mod_kernel,
        grid=(DEPTH, n // tn),
        in_specs=[pl.BlockSpec((16, D_MODEL), lambda l, j: (0, 0)),
                  pl.BlockSpec((None, D_MODEL, tn), lambda l, j: (l, 0, j)),
                  pl.BlockSpec((None, 1, tn), lambda l, j: (l, 0, j))],
        out_specs=pl.BlockSpec((None, 16, tn), lambda l, j: (l, 0, j)),
        out_shape=jax.ShapeDtypeStruct((DEPTH, 16, n), F32),
        compiler_params=_params("parallel", "parallel"),
        name="adaln_mod",
    )(cc, ada_w, ada_b.reshape(DEPTH, 1, n))


HALO = 16


def _mm_kernel(*refs, pro, epi, has_bias, slot, gslot, tile0, seq_len, has_dest):
    conv = epi in ("conv", "conv_silu")
    it = iter(refs)
    x_ref, w_ref = next(it), next(it)
    xp_ref, xn_ref = (next(it), next(it)) if conv else (None, None)
    mod_ref = next(it) if (pro == "adaln" or epi == "gres") else None
    nw_ref = next(it) if pro == "rms" else None
    b_ref = next(it) if has_bias else None
    res_ref = next(it) if epi == "gres" else None
    cw_ref, cb_ref = (next(it), next(it)) if conv else (None, None)
    if has_dest:
        next(it)
    o_ref, xa_ref = next(it), next(it)
    tm = x_ref.shape[0]

    def prologue(x):
        x = x.astype(F32)
        if pro == "adaln":
            x = _rms(x) * (1.0 + mod_ref[0, slot + 1:slot + 2, :]) + mod_ref[0, slot:slot + 1, :]
        elif pro == "rms":
            x = _rms(x) * nw_ref[...]
        return x

    if conv:
        row0 = (tile0 + pl.program_id(0)) * tm
        keep_top = jnp.where((row0 & (seq_len - 1)) == 0, 0.0, 1.0)
        keep_bot = jnp.where(((row0 + tm) & (seq_len - 1)) == 0, 0.0, 1.0)

    @pl.when(pl.program_id(1) == 0)
    def _():
        if conv:
            xa_ref[0:HALO, :] = (prologue(xp_ref[...]) * keep_top).astype(BF16)
            xa_ref[HALO:HALO + tm, :] = prologue(x_ref[...]).astype(BF16)
            xa_ref[HALO + tm:2 * HALO + tm, :] = (prologue(xn_ref[...]) * keep_bot).astype(BF16)
        else:
            xa_ref[...] = prologue(x_ref[...]).astype(BF16)

    acc = _dot(xa_ref[...], w_ref[...])
    if conv:
        w0, w1, w2 = cw_ref[0:1, :], cw_ref[1:2, :], cw_ref[2:3, :]
        prev = pltpu.roll(acc, 1, 0)[HALO:HALO + tm]
        nxt = pltpu.roll(acc, tm + 2 * HALO - 1, 0)[HALO:HALO + tm]
        const = cb_ref[...] + (b_ref[...] * (w0 + w1 + w2) if has_bias else 0.0)
        y = w0 * prev + w1 * acc[HALO:HALO + tm] + w2 * nxt + const
        act = (lambda v: _silu(v.astype(o_ref.dtype))) if epi == "conv_silu" else (lambda v: v)
        o_ref[...] = act(y).astype(o_ref.dtype)
        if has_bias:
            o_ref[0:1, :] = act(y[0:1] - (1.0 - keep_top) * (b_ref[...] * w0)).astype(o_ref.dtype)
            o_ref[tm - 1:tm, :] = act(y[tm - 1:tm] - (1.0 - keep_bot) * (b_ref[...] * w2)).astype(o_ref.dtype)
        return
    if has_bias:
        acc = acc + b_ref[...]
    if epi == "gres":
        acc = res_ref[...] + mod_ref[0, gslot:gslot + 1, :] * acc
    o_ref[...] = acc.astype(o_ref.dtype)


def _mm(x, w, *, rows, tn, tm=1024, x_colblock=0, pro="none", epi="none", mod=None, slot=0, gslot=0,
        nw=None, bias=None, res=None, conv_w=None, conv_b=None, seq_len=None, row0=0, out_rows=None, dest=None,
        out_dtype=F32, name="mm"):
    k, n = w.shape
    assert rows % tm == 0 and row0 % tm == 0 and n % tn == 0 and MOD_ROWS % tm == 0
    if epi == "gres":
        assert tn == n == D_MODEL
    conv = epi in ("conv", "conv_silu")
    per = MOD_ROWS // tm
    t0 = row0 // tm
    ins = [x, w]
    specs = [pl.BlockSpec((tm, k), lambda i, j: (t0 + i, x_colblock)),
             pl.BlockSpec((k, tn), lambda i, j: (0, j))]
    if conv:
        assert x_colblock == 0 and seq_len % tm == 0 and (bias is None or out_dtype == F32)
        hb = tm // HALO
        last = (row0 + rows) // HALO - 1
        ins += [x, x]
        specs += [pl.BlockSpec((HALO, k), lambda i, j: (jnp.maximum((t0 + i) * hb - 1, 0), 0)),
                  pl.BlockSpec((HALO, k), lambda i, j: (jnp.minimum((t0 + i + 1) * hb, last), 0))]
    if pro == "adaln" or epi == "gres":
        ins.append(mod)
        specs.append(pl.BlockSpec((1, N_MOD, D_MODEL), lambda i, j: ((t0 + i) // per, 0, 0)))
    if pro == "rms":
        ins.append(nw.reshape(1, k))
        specs.append(pl.BlockSpec((1, k), lambda i, j: (0, 0)))
    if bias is not None:
        ins.append(bias.reshape(1, n))
        specs.append(pl.BlockSpec((1, tn), lambda i, j: (0, j)))
    if epi == "gres":
        ins.append(res)
        specs.append(pl.BlockSpec((tm, tn), lambda i, j: (t0 + i, j)))
    if conv:
        ins += [conv_w, conv_b.reshape(1, n)]
        specs += [pl.BlockSpec((3, tn), lambda i, j: (0, j)), pl.BlockSpec((1, tn), lambda i, j: (0, j))]
    aliases = {}
    out_shape = jax.ShapeDtypeStruct((out_rows or row0 + rows, n), out_dtype)
    if dest is not None:
        aliases = {len(ins): 0}
        ins.append(dest)
        specs.append(pl.BlockSpec(memory_space=pl.ANY))
        out_shape = jax.ShapeDtypeStruct(dest.shape, dest.dtype)
    return pl.pallas_call(
        functools.partial(_mm_kernel, pro=pro, epi=epi, has_bias=bias is not None, slot=slot, gslot=gslot,
                          tile0=t0, seq_len=seq_len, has_dest=dest is not None),
        grid=(rows // tm, n // tn),
        in_specs=specs,
        out_specs=pl.BlockSpec((tm, tn), lambda i, j: (t0 + i, j)),
        out_shape=out_shape,
        scratch_shapes=[pltpu.VMEM((tm + 2 * HALO if conv else tm, k), BF16)],
        input_output_aliases=aliases,
        compiler_params=_params("parallel", "arbitrary"),
        name=name,
    )(*ins)


def _ffn_kernel(*refs, slot, final, fc, has_dest):
    if has_dest:
        refs = refs[:-2] + refs[-1:]
    if final:
        x_ref, mod_ref, wi_ref, wo_ref, fw_ref, o_ref = refs
    else:
        x_ref, mod_ref, wi_ref, wo_ref, o_ref = refs
    x = x_ref[...]
    a = (_rms(x) * (1.0 + mod_ref[0, slot + 1:slot + 2, :]) + mod_ref[0, slot:slot + 1, :]).astype(BF16)
    acc = None
    for c in range(0, FFN_DIM, fc):
        g = _dot(a, wi_ref[:, c:c + fc])
        u = _dot(a, wi_ref[:, FFN_DIM + c:FFN_DIM + c + fc])
        t = _dot((_silu(g) * u).astype(BF16), wo_ref[c:c + fc, :])
        acc = t if acc is None else acc + t
    out = x + (0.5 * mod_ref[0, slot + 2:slot + 3, :]) * acc
    if final:
        out = _rms(out) * fw_ref[...]
    o_ref[...] = out


def _ffn(h, mod, w_in, w_out, layer, k, *, rows, slot, final_w=None, out_row0=0, out_rows=None, dest=None,
         tm=512, fc=FFN_DIM):
    per = MOD_ROWS // tm
    o0 = out_row0 // tm
    final = final_w is not None
    resident = pl.Buffered(1)
    ins = [h, mod, w_in, w_out]
    specs = [pl.BlockSpec((tm, D_MODEL), lambda i: (i, 0)),
             pl.BlockSpec((1, N_MOD, D_MODEL), lambda i: ((o0 + i) // per, 0, 0)),
             pl.BlockSpec((None, None, D_MODEL, 2 * FFN_DIM), lambda i: (layer, k, 0, 0), pipeline_mode=resident),
             pl.BlockSpec((None, None, FFN_DIM, D_MODEL), lambda i: (layer, k, 0, 0), pipeline_mode=resident)]
    if final:
        ins.append(final_w.reshape(1, D_MODEL))
        specs.append(pl.BlockSpec((1, D_MODEL), lambda i: (0, 0)))
    aliases = {}
    out_shape = jax.ShapeDtypeStruct((out_rows or rows, D_MODEL), F32)
    if dest is not None:
        aliases = {len(ins): 0}
        ins.append(dest)
        specs.append(pl.BlockSpec(memory_space=pl.ANY))
        out_shape = jax.ShapeDtypeStruct(dest.shape, dest.dtype)
    return pl.pallas_call(
        functools.partial(_ffn_kernel, slot=slot, final=final, fc=fc, has_dest=dest is not None),
        grid=(rows // tm,),
        in_specs=specs,
        out_specs=pl.BlockSpec((tm, D_MODEL), lambda i: (o0 + i, 0)),
        out_shape=out_shape,
        input_output_aliases=aliases,
        compiler_params=_params("parallel"),
        name="ffn",
    )(*ins)


FILTER_W = 128


def _filter_kernel(za_ref, zb_ref, w1_ref, b1_ref, fr_ref, w2_ref, b2_ref, w3f_ref, w3b_ref, dl_ref,
                   sum_ref, diff_ref):
    fr = fr_ref[...]

    def taps(z_ref, w3_ref):
        z = z_ref[...]
        hid = jnp.sin(fr * (_dot(z, w1_ref[...], precision=HIGHEST) + b1_ref[...]))
        hid = jnp.sin(fr * (_dot(hid, w2_ref[...], precision=HIGHEST) + b2_ref[...]))
        window = jnp.exp(-z[:, 0:1] * dl_ref[...]) * z[:, FILTER_W - 1:FILTER_W]
        return _dot(hid.astype(BF16), w3_ref[...].astype(BF16)) * window

    hf = taps(za_ref, w3f_ref)
    hb = taps(zb_ref, w3b_ref)
    sum_ref[...] = (hf + hb).astype(BF16)
    diff_ref[...] = (hb - hf).astype(BF16)


def _hyena_filters(L, poly, w1, b1, freq, w2, b2, w3):
    S = L // poly
    nrho = 2 * poly - 1
    jj = jnp.arange(S, dtype=jnp.int32)
    rho = jnp.arange(-(poly - 1), poly, dtype=jnp.int32)
    t_tab = jnp.linspace(0.0, 1.0, L, dtype=F32)
    bands = jnp.linspace(1e-4, HYENA_BANDS - 1, HYENA_BANDS, dtype=F32)

    def feats(pos, lowest):
        valid = (pos >= lowest).astype(F32)
        pc = jnp.maximum(pos, 0)
        ang = (2.0 * math.pi / L) * pc.astype(F32)[..., None] * bands
        z = jnp.concatenate([jnp.take(t_tab, pc)[..., None], jnp.cos(ang), -jnp.sin(ang)], axis=-1)
        z = jnp.pad(z, ((0, 0), (0, 0), (0, FILTER_W - 1 - HYENA_EMB)))
        return jnp.concatenate([z, valid[..., None]], axis=-1).reshape(nrho * S, FILTER_W)

    za = feats(poly * jj[None, :] + rho[:, None], 0)
    zb = feats(poly * jj[None, :] - rho[:, None], 1)
    ph = FILTER_W - HYENA_FILTER_HIDDEN
    w1 = jnp.pad(w1, ((0, FILTER_W - HYENA_EMB), (0, ph)))
    w2 = jnp.pad(w2, ((0, ph), (0, ph)))
    w3 = jnp.pad(w3, ((0, ph), (0, 0))).reshape(FILTER_W, 2, 2, D_MODEL)
    w3f = w3[:, :, 0].reshape(FILTER_W, 2 * D_MODEL)
    w3b = w3[:, :, 1].reshape(FILTER_W, 2 * D_MODEL)
    b1, b2, freq = (jnp.pad(t, (0, ph)).reshape(1, FILTER_W) for t in (b1, b2, freq))
    max_decay = math.log(HYENA_DECAY_TARGET) / HYENA_FAST_DECAY
    min_decay = math.log(HYENA_DECAY_TARGET) / HYENA_SLOW_DECAY
    deltas = jnp.abs(jnp.linspace(min_decay, max_decay, D_MODEL, dtype=F32))
    deltas = jnp.tile(deltas, 2).reshape(1, 2 * D_MODEL)
    full = lambda shape: pl.BlockSpec(shape, lambda i: (0, 0))
    sq = full((FILTER_W, FILTER_W))
    vec = full((1, FILTER_W))
    return pl.pallas_call(
        _filter_kernel,
        grid=(nrho,),
        in_specs=[pl.BlockSpec((S, FILTER_W), lambda i: (i, 0)), pl.BlockSpec((S, FILTER_W), lambda i: (i, 0)),
                  sq, vec, vec, sq, vec, full((FILTER_W, 2 * D_MODEL)), full((FILTER_W, 2 * D_MODEL)),
                  full((1, 2 * D_MODEL))],
        out_specs=[pl.BlockSpec((S, 2 * D_MODEL), lambda i: (i, 0))] * 2,
        out_shape=[jax.ShapeDtypeStruct((nrho * S, 2 * D_MODEL), BF16)] * 2,
        compiler_params=_params("parallel"),
        name="hyena_filter",
    )(za, zb, w1, b1, freq, w2, b2, w3f, w3b, deltas)


def _lmm_kernel(*refs, epi, tf, scale, has_dest):
    it = iter(refs)
    a_ref, x_ref = next(it), next(it)
    if epi == "spec":
        k_ref = next(it)
    elif epi == "gate":
        g_ref, v_ref, bias_ref = next(it), next(it), next(it)
    if has_dest:
        next(it)
    o_ref, xb_ref = next(it), next(it)

    @pl.when(pl.program_id(2) == 0)
    def _():
        xb_ref[...] = x_ref[...].astype(BF16)

    acc = _dot(a_ref[...], xb_ref[...])
    if epi == "spec":
        xr, xs = acc[:tf], acc[tf:]
        kr, ki = k_ref[0:tf, :], k_ref[tf:2 * tf, :]
        o_ref[0:tf, :] = ((xr * kr + xs * ki) * scale).astype(o_ref.dtype)
        o_ref[tf:2 * tf, :] = ((xs * kr - xr * ki) * scale).astype(o_ref.dtype)
    elif epi == "gate":
        o_ref[...] = (g_ref[...] * (acc + v_ref[...].astype(F32) * bias_ref[...])).astype(o_ref.dtype)
    else:
        o_ref[...] = acc.astype(o_ref.dtype)


def _lmm(a, x, *, nb, ncb, tm, n, x_row0=0, x_cb0=0, epi="none", tf=0, scale=1.0, kspec=None, k_cb=0,
         g=None, g_cb=0, g_row0=0, v=None, v_cb=0, v_row0=0, bias=None, out_dtype=F32, dest=None, dest_row0=0,
         name="lmm"):
    mo, k = a.shape
    nm = mo // tm
    xr0 = x_row0 // k
    or0 = dest_row0 // tm
    ins = [a, x]
    specs = [pl.BlockSpec((tm, k), lambda b, c, m: (m, 0)),
             pl.BlockSpec((k, n), lambda b, c, m: (xr0 + b, x_cb0 + c))]
    if epi == "spec":
        ins.append(kspec)
        specs.append(pl.BlockSpec((tm, n), lambda b, c, m: (m, k_cb)))
    elif epi == "gate":
        gr0, vr0 = g_row0 // tm, v_row0 // tm
        ins += [g, v, bias.reshape(1, n)]
        specs += [pl.BlockSpec((tm, n), lambda b, c, m: (gr0 + b * nm + m, g_cb)),
                  pl.BlockSpec((tm, n), lambda b, c, m: (vr0 + b * nm + m, v_cb)),
                  pl.BlockSpec((1, n), lambda b, c, m: (0, 0))]
    aliases = {}
    out_shape = jax.ShapeDtypeStruct((nb * mo, ncb * n), out_dtype)
    if dest is not None:
        aliases = {len(ins): 0}
        ins.append(dest)
        specs.append(pl.BlockSpec(memory_space=pl.ANY))
        out_shape = jax.ShapeDtypeStruct(dest.shape, dest.dtype)
    return pl.pallas_call(
        functools.partial(_lmm_kernel, epi=epi, tf=tf, scale=scale, has_dest=dest is not None),
        grid=(nb, ncb, nm),
        in_specs=specs,
        out_specs=pl.BlockSpec((tm, n), lambda b, c, m: (or0 + b * nm + m, c)),
        out_shape=out_shape,
        scratch_shapes=[pltpu.VMEM((k, n), BF16)],
        input_output_aliases=aliases,
        compiler_params=_params("parallel", "parallel", "arbitrary"),
        name=name,
    )(*ins)


def _dft_mats(L, tf):
    idx = jnp.arange(L, dtype=jnp.int32)
    ph = ((2 * idx[:, None] + 1) * idx[None, :]) % (4 * L)
    ang = ph.astype(F32) * (2.0 * math.pi / (4 * L))
    c, s = jnp.cos(ang), jnp.sin(ang)
    nf = L // tf
    fwd = jnp.stack([c.reshape(nf, tf, L), s.reshape(nf, tf, L)], axis=1).reshape(2 * L, L)
    inv = jnp.stack([c.T.reshape(L, nf, tf), s.T.reshape(L, nf, tf)], axis=2).reshape(L, 2 * L)
    return c.astype(BF16), s.astype(BF16), fwd.astype(BF16), inv.astype(BF16)


def _interleave(kr, ki, tf):
    L, n = kr.shape
    return jnp.stack([kr.reshape(L // tf, tf, n), ki.reshape(L // tf, tf, n)], axis=1).reshape(2 * L, n)


def _hyena_core(u, L, nb, row0, fparams, hy_bias, name, dest=None):
    tf = min(L, 256)
    tmi = min(L, 256)
    cmat, smat, fwd, inv = _dft_mats(L, tf)
    hsum, hdiff = _hyena_filters(L, 1, *fparams)
    kr = _lmm(cmat, hsum, nb=1, ncb=2, tm=tf, n=D_MODEL, name=name + "_kr")
    ki = _lmm(smat, hdiff, nb=1, ncb=2, tm=tf, n=D_MODEL, name=name + "_ki")
    kspec = _interleave(kr, ki, tf)
    scale = 1.0 / L
    y1 = _lmm(fwd, u, nb=nb, ncb=1, tm=2 * tf, n=D_MODEL, x_row0=row0, x_cb0=2, epi="spec", tf=tf, scale=scale,
              kspec=kspec, k_cb=0, out_dtype=BF16, name=name + "_fwd1")
    z = _lmm(inv, y1, nb=nb, ncb=1, tm=tmi, n=D_MODEL, epi="gate", g=u, g_cb=0, g_row0=row0, v=u, v_cb=2,
             v_row0=row0, bias=hy_bias[0], out_dtype=F32, name=name + "_inv1")
    y2 = _lmm(fwd, z, nb=nb, ncb=1, tm=2 * tf, n=D_MODEL, epi="spec", tf=tf, scale=scale,
              kspec=kspec, k_cb=1, out_dtype=BF16, name=name + "_fwd2")
    return _lmm(inv, y2, nb=nb, ncb=1, tm=tmi, n=D_MODEL, epi="gate", g=u, g_cb=1, g_row0=row0, v=z, v_cb=0,
                bias=hy_bias[1], out_dtype=F32, dest=dest, dest_row0=row0, name=name + "_inv2")


POLY = 4


def _poly_conv_kernel(a_ref, ai_ref, u_ref, kr_ref, ki_ref, g_ref, bias_ref, o_ref, *, scale):
    S = a_ref.shape[1]
    tc = u_ref.shape[1]
    phases = [u_ref[pl.ds(r, S, stride=POLY), :] for r in range(POLY)]
    e = _dot(a_ref[...], jnp.concatenate([ph.astype(BF16) for ph in phases], axis=1))
    f = []
    for r in range(POLY):
        fre = fim = None
        for rp in range(POLY):
            kr = kr_ref[r - rp + POLY - 1]
            ki = ki_ref[r - rp + POLY - 1]
            er = e[0:S, rp * tc:(rp + 1) * tc]
            es = e[S:2 * S, rp * tc:(rp + 1) * tc]
            tre = kr * er + ki * es
            tim = kr * es - ki * er
            fre = tre if fre is None else fre + tre
            fim = tim if fim is None else fim + tim
        f.append(jnp.concatenate([fre, fim], axis=0).astype(BF16))
    y = _dot(ai_ref[...], jnp.concatenate(f, axis=1)) * scale
    for r in range(POLY):
        out = g_ref[pl.ds(r, S, stride=POLY), :] * (y[:, r * tc:(r + 1) * tc] + phases[r] * bias_ref[...])
        o_ref[pl.ds(r, S, stride=POLY), :] = out


def _poly_conv(fwd, inv, x, x_cb, g, g_cb, kre, kim, order, bias, *, nb, out_rows=None, tc=128, name="hy_conv"):
    L = MOD_ROWS
    S = L // POLY
    nct = D_MODEL // tc
    nrho = 2 * POLY - 1
    return pl.pallas_call(
        functools.partial(_poly_conv_kernel, scale=1.0 / S),
        grid=(nct, nb),
        in_specs=[pl.BlockSpec((2 * S, S), lambda c, b: (0, 0)),
                  pl.BlockSpec((S, 2 * S), lambda c, b: (0, 0)),
                  pl.BlockSpec((L, tc), lambda c, b: (b, x_cb * nct + c)),
                  pl.BlockSpec((nrho, S, tc), lambda c, b: (0, 0, order * nct + c)),
                  pl.BlockSpec((nrho, S, tc), lambda c, b: (0, 0, order * nct + c)),
                  pl.BlockSpec((L, tc), lambda c, b: (b, g_cb * nct + c)),
                  pl.BlockSpec((1, tc), lambda c, b: (0, c))],
        out_specs=pl.BlockSpec((L, tc), lambda c, b: (b, c)),
        out_shape=jax.ShapeDtypeStruct((out_rows or nb * L, D_MODEL), F32),
        compiler_params=_params("parallel", "parallel"),
        name=name,
    )(fwd, inv, x, kre, kim, g, bias.reshape(1, D_MODEL))


def _hyena_core_poly(u, nb, fparams, hy_bias, out_rows):
    S = MOD_ROWS // POLY
    nrho = 2 * POLY - 1
    cmat, smat, fwd, inv = _dft_mats(S, S)
    ksum, kdiff = _hyena_filters(MOD_ROWS, POLY, *fparams)
    kre = _lmm(cmat, ksum, nb=nrho, ncb=2, tm=S, n=D_MODEL, name="hy_lat_kr").reshape(nrho, S, 2 * D_MODEL)
    kim = _lmm(smat, kdiff, nb=nrho, ncb=2, tm=S, n=D_MODEL, name="hy_lat_ki").reshape(nrho, S, 2 * D_MODEL)
    z = _poly_conv(fwd, inv, u, 2, u, 0, kre, kim, 0, hy_bias[0], nb=nb, name="hy_lat_conv1")
    return _poly_conv(fwd, inv, z, 0, u, 1, kre, kim, 1, hy_bias[1], nb=nb, out_rows=out_rows, name="hy_lat_conv2")


def _softplus(x):
    return jnp.maximum(x, 0.0) + jnp.log(1.0 + jnp.exp(-jnp.abs(x)))


def _ssd_kernel(xs_f, b_f, c_f, dt_f, dtt_f, xs_b, b_b, c_b, dt_b, dtt_b, dtb_ref, dtbt_ref, al_ref, alt_ref,
                yf_ref, yb_ref, st_ref):
    @pl.when(pl.program_id(1) == 0)
    def _():
        st_ref[...] = jnp.zeros_like(st_ref)

    _ssd_chunk(0, xs_f, b_f, c_f, dt_f, dtt_f, dtb_ref[0], dtbt_ref[0], al_ref[0], alt_ref[0], yf_ref, st_ref.at[0])
    _ssd_chunk(1, xs_b, b_b, c_b, dt_b, dtt_b, dtb_ref[1], dtbt_ref[1], al_ref[1], alt_ref[1], yb_ref, st_ref.at[1])


def _ssd_chunk(d, xs_ref, b_ref, c_ref, dt_ref, dtt_ref, dt_bias, dt_bias_t, a_log, a_log_t, y_ref, st_ref):
    T = SSM_CHUNK
    N = SSM_STATE
    ri = lax.broadcasted_iota(jnp.int32, (T, T), 0)
    ci = lax.broadcasted_iota(jnp.int32, (T, T), 1)
    mask = (ri >= ci) if d == 0 else (ri <= ci)
    tri_col = mask.astype(F32)
    tri_row = ((ci >= ri) if d == 0 else (ci <= ri)).astype(F32)
    low = lax.broadcasted_iota(jnp.int32, (T, 2 * SSM_HEAD_DIM), 1) < SSM_HEAD_DIM

    a = _softplus(dt_ref[...] + dt_bias) * -jnp.exp(a_log)
    dtt = _softplus(dtt_ref[...] + dt_bias_t)
    at = dtt * -jnp.exp(a_log_t)
    cs_col = _dot(tri_col, a, precision=HIGHEST)
    cs_row = _dot(at, tri_row, precision=HIGHEST)
    tot = jnp.sum(at, axis=1, keepdims=True)
    row_dt = cs_row - jnp.log(dtt)
    w_out = dtt * jnp.exp(tot - cs_row)
    e_tot = jnp.exp(tot)

    for g in range(SSM_GROUPS):
        bt = jnp.transpose(b_ref[:, g * N:(g + 1) * N].astype(F32))
        cg = c_ref[:, g * N:(g + 1) * N].astype(F32)
        cb = _dot(cg.astype(BF16), bt.astype(BF16))
        for k in range(HEADS_PER_GROUP // 2):
            h0 = g * HEADS_PER_GROUP + 2 * k
            lanes = slice(h0 * SSM_HEAD_DIM, (h0 + 2) * SSM_HEAD_DIM)
            xs = xs_ref[:, lanes].astype(BF16)
            st = st_ref[g, :, k * 128:(k + 1) * 128]
            rhs = jnp.concatenate([xs, st.astype(BF16)], axis=0)
            ys, upds = [], []
            for h in (h0, h0 + 1):
                col = jnp.broadcast_to(cs_col[:, h:h + 1], (T, T))
                m = jnp.where(mask, jnp.exp(col - row_dt[h:h + 1, :]), 0.0) * cb
                lhs = jnp.concatenate([m, jnp.exp(col) * cg], axis=1).astype(BF16)
                ys.append(_dot(lhs, rhs))
                upds.append(_dot((bt * w_out[h:h + 1, :]).astype(BF16), xs))
            y_ref[:, lanes] = jnp.where(low, ys[0], ys[1]).astype(y_ref.dtype)
            decay = jnp.where(low, e_tot[h0:h0 + 1, :], e_tot[h0 + 1:h0 + 2, :])
            st_ref[g, :, k * 128:(k + 1) * 128] = st * decay + jnp.where(low, upds[0], upds[1])


def _ssd(xbc, dta, dtt, dt_bias, a_log, *, nb):
    m = xbc.shape[0]
    T = SSM_CHUNK
    H = SSM_HEADS
    lat_c = MOD_ROWS // T
    ctx_c = CTX_LEN // T
    ctx0 = nb * lat_c
    steps = lat_c + ctx_c

    def rb(b, d, t):
        ctx = ctx0 + ctx_c * b + jnp.where(d == 0, t, ctx_c - 1 - t)
        lat = lat_c * b + jnp.where(d == 0, t - ctx_c, steps - 1 - t)
        return jnp.where(t < ctx_c, ctx, lat)

    def direction(d):
        return [pl.BlockSpec((T, SSM_INNER), lambda b, t: (rb(b, d, t), 0)),
                pl.BlockSpec((T, SSM_BC_DIM), lambda b, t: (rb(b, d, t), 2)),
                pl.BlockSpec((T, SSM_BC_DIM), lambda b, t: (rb(b, d, t), 3)),
                pl.BlockSpec((None, T, H), lambda b, t: (d, rb(b, d, t), 0)),
                pl.BlockSpec((None, H, T), lambda b, t: (d, 0, rb(b, d, t)))]

    whole = lambda shape: pl.BlockSpec(shape, lambda b, t: (0, 0, 0))
    return pl.pallas_call(
        _ssd_kernel,
        grid=(nb, steps),
        in_specs=direction(0) + direction(1) + [whole((2, 1, H)), whole((2, H, 1)), whole((2, 1, H)), whole((2, H, 1))],
        out_specs=[pl.BlockSpec((T, SSM_INNER), lambda b, t: (rb(b, 0, t), 0)),
                   pl.BlockSpec((T, SSM_INNER), lambda b, t: (rb(b, 1, t), 0))],
        out_shape=[jax.ShapeDtypeStruct((m, SSM_INNER), BF16)] * 2,
        scratch_shapes=[pltpu.VMEM((2, SSM_GROUPS, SSM_STATE, HEADS_PER_GROUP * SSM_HEAD_DIM), F32)],
        compiler_params=_params("parallel", "arbitrary"),
        name="ssd_scan",
    )(xbc, xbc, xbc, dta, dtt, xbc, xbc, xbc, dta, dtt, dt_bias.reshape(2, 1, H), dt_bias.reshape(2, H, 1),
      a_log.reshape(2, 1, H), a_log.reshape(2, H, 1))


def _ssd_finish_kernel(yf_ref, yb_ref, xs_ref, z_ref, dsk_ref, nw_ref, o_ref):
    gw = SSM_INNER // SSM_GROUPS
    z = z_ref[...]
    y = yf_ref[...].astype(F32) + yb_ref[...].astype(F32) + dsk_ref[...] * xs_ref[...].astype(F32)
    y = y * _silu(z)
    for g in range(SSM_GROUPS):
        yg = _rms(y[:, g * gw:(g + 1) * gw])
        o_ref[:, g * gw:(g + 1) * gw] = (yg * nw_ref[:, g * gw:(g + 1) * gw]).astype(o_ref.dtype)


def _ssd_finish(yf, yb, xbc, zx, d_skip, norm_w, *, rows, tm=256):
    n = SSM_INNER
    return pl.pallas_call(
        _ssd_finish_kernel,
        grid=(rows // tm,),
        in_specs=[pl.BlockSpec((tm, n), lambda i: (i, 0)),
                  pl.BlockSpec((tm, n), lambda i: (i, 0)),
                  pl.BlockSpec((tm, n), lambda i: (i, 0)),
                  pl.BlockSpec((tm, n), lambda i: (i, 0)),
                  pl.BlockSpec((1, n), lambda i: (0, 0)),
                  pl.BlockSpec((1, n), lambda i: (0, 0))],
        out_specs=pl.BlockSpec((tm, n), lambda i: (i, 0)),
        out_shape=jax.ShapeDtypeStruct((rows, n), BF16),
        compiler_params=_params("parallel"),
        name="ssd_finish",
    )(yf, yb, xbc, zx, jnp.repeat(d_skip, SSM_HEAD_DIM).reshape(1, n), norm_w.reshape(1, n))


HEAD_W = 2 * MLA_NOPE
ATTN_SUB_ROWS = 256


def _attn_kernel(*refs, has_lat):
    if has_lat:
        q_ref, kvc_ref, krc_ref, kv_ref, kr_ref, tq_ref, tk_ref, o_ref, kc_scr, vc_scr, k_scr, v_scr = refs
    else:
        q_ref, kvc_ref, krc_ref, _, o_ref, kc_scr, vc_scr = refs
    R = MLA_ROPE
    kscale = MLA_SCALE * math.log2(math.e)

    def build_keys():
        krc = krc_ref[...].astype(F32)
        lane_c = lax.broadcasted_iota(jnp.int32, krc.shape, 1)
        pe_c = jnp.where((lane_c >= MLA_NOPE) & (lane_c < MLA_NOPE + R), pltpu.roll(krc, MLA_NOPE, 1), 0.0)
        for hh in range(2):
            kvh = kvc_ref[:, hh * HEAD_W:(hh + 1) * HEAD_W].astype(F32)
            kc_scr[hh] = (jnp.where(lane_c < MLA_NOPE, kvh, pe_c) * kscale).astype(BF16)
            vc_scr[hh] = jnp.where(lane_c < MLA_NOPE, 1.0, kvh).astype(BF16)
        if has_lat:
            t = kr_ref[...].astype(F32) * tk_ref[...]
            lane = lax.broadcasted_iota(jnp.int32, t.shape, 1)
            krot = jnp.where(lane < R, t + pltpu.roll(t, HEAD_W - R, 1), 0.0)
            krr = pltpu.roll(krot, MLA_NOPE, 1) + pltpu.roll(krot, MLA_NOPE + R, 1)
            for hh in range(2):
                kvh = kv_ref[:, hh * HEAD_W:(hh + 1) * HEAD_W].astype(F32)
                k_scr[hh] = (jnp.where(lane < MLA_NOPE, kvh, krr) * kscale).astype(BF16)
                v_scr[hh] = jnp.where(lane < MLA_NOPE, 1.0, kvh).astype(BF16)

    if has_lat:
        pl.when(pl.program_id(2) == 0)(build_keys)
    else:
        build_keys()

    sub = min(q_ref.shape[0], ATTN_SUB_ROWS)
    for r0 in range(0, q_ref.shape[0], sub):
        rows = slice(r0, r0 + sub)
        res = []
        for hh in range(2):
            q = q_ref[rows, hh * HEAD_W:(hh + 1) * HEAD_W]
            s_c = _dot_nt(q, kc_scr[hh])
            mx = jnp.max(s_c, axis=-1, keepdims=True)
            if has_lat:
                ql = (q.astype(F32) * tq_ref[rows, :]).astype(BF16)
                s_l = _dot_nt(ql, k_scr[hh])
                mx = jnp.maximum(mx, jnp.max(s_l, axis=-1, keepdims=True))
                acc = _dot(jnp.exp2(s_l - mx).astype(BF16), v_scr[hh])
                acc = acc + _dot(jnp.exp2(s_c - mx).astype(BF16), vc_scr[hh])
            else:
                acc = _dot(jnp.exp2(s_c - mx).astype(BF16), vc_scr[hh])
            res.append(acc / pltpu.roll(acc, MLA_V, 1))
        lane_o = lax.broadcasted_iota(jnp.int32, res[0].shape, 1)
        o_ref[rows, :] = jnp.where(lane_o < MLA_V, pltpu.roll(res[0], MLA_V, 1), res[1]).astype(o_ref.dtype)


def _attention(q, kv, dn, cs, *, nb, tq=512):
    L = MOD_ROWS
    hp = MLA_HEADS // 2
    w = 2 * HEAD_W
    nq = L // tq
    cb = nb * (L // CTX_LEN)
    kr_cb = (MLA_Q_RANK + MLA_KV_RANK) // HEAD_W
    tab_q = jnp.concatenate([jnp.ones((L, MLA_NOPE), F32), cs], axis=1)
    tab_k = jnp.concatenate([cs, jnp.zeros((L, HEAD_W - 2 * MLA_ROPE), F32)], axis=1)
    lat = pl.pallas_call(
        functools.partial(_attn_kernel, has_lat=True),
        grid=(nb, hp, nq),
        in_specs=[pl.BlockSpec((tq, w), lambda b, p, i: (b * nq + i, p)),
                  pl.BlockSpec((CTX_LEN, w), lambda b, p, i: (cb + b, p)),
                  pl.BlockSpec((CTX_LEN, HEAD_W), lambda b, p, i: (cb + b, kr_cb)),
                  pl.BlockSpec((L, w), lambda b, p, i: (b, p)),
                  pl.BlockSpec((L, HEAD_W), lambda b, p, i: (b, kr_cb)),
                  pl.BlockSpec((tq, HEAD_W), lambda b, p, i: (i, 0)),
                  pl.BlockSpec((L, HEAD_W), lambda b, p, i: (0, 0))],
        out_specs=pl.BlockSpec((tq, 2 * MLA_V), lambda b, p, i: (b * nq + i, p)),
        out_shape=jax.ShapeDtypeStruct((nb * (L + CTX_LEN), MLA_HEADS * MLA_V), BF16),
        scratch_shapes=[pltpu.VMEM((2, CTX_LEN, HEAD_W), BF16)] * 2 + [pltpu.VMEM((2, L, HEAD_W), BF16)] * 2,
        compiler_params=_params("parallel", "parallel", "arbitrary"),
        name="mla_attn",
    )(q, kv, dn, kv, dn, tab_q, tab_k)
    return pl.pallas_call(
        functools.partial(_attn_kernel, has_lat=False),
        grid=(nb, hp),
        in_specs=[pl.BlockSpec((CTX_LEN, w), lambda b, p: (cb + b, p)),
                  pl.BlockSpec((CTX_LEN, w), lambda b, p: (cb + b, p)),
                  pl.BlockSpec((CTX_LEN, HEAD_W), lambda b, p: (cb + b, kr_cb)),
                  pl.BlockSpec(memory_space=pl.ANY)],
        out_specs=pl.BlockSpec((CTX_LEN, 2 * MLA_V), lambda b, p: (cb + b, p)),
        out_shape=jax.ShapeDtypeStruct(lat.shape, lat.dtype),
        scratch_shapes=[pltpu.VMEM((2, CTX_LEN, HEAD_W), BF16)] * 2,
        input_output_aliases={3: 0},
        compiler_params=_params("parallel", "parallel"),
        name="mla_attn_ctx",
    )(q, kv, dn, lat)


def _rot_cols(w):
    wp = w.reshape(w.shape[:-1] + (2, 2, ROPE_AXIS // 2))
    return jnp.stack([-wp[..., 1, :], wp[..., 0, :]], axis=-2).reshape(w.shape)


def _rope_table(n_lat):
    rows = n_lat // GRID_W
    row = jnp.repeat(jnp.arange(rows), GRID_W)
    col = jnp.tile(jnp.arange(GRID_W), rows)
    inv = 1.0 / (ROPE_BASE ** (jnp.arange(0, ROPE_AXIS, 2, dtype=F32) / ROPE_AXIS))
    ar = row.astype(F32)[:, None] * inv[None, :]
    ac = col.astype(F32)[:, None] * inv[None, :]
    ang = jnp.concatenate([ar, ar, ac, ac], axis=-1)
    return jnp.concatenate([jnp.cos(ang), jnp.sin(ang)], axis=-1)


def _conv_proj(h, w, bias, conv_w, conv_b, mod, rows, nb, epi, out_dtype, name):
    lat_rows = nb * MOD_ROWS
    common = dict(tn=1024, pro="adaln", mod=mod, slot=3, bias=bias, epi=epi, conv_w=conv_w, conv_b=conv_b,
                  out_dtype=out_dtype)
    out = _mm(h, w, rows=lat_rows, seq_len=MOD_ROWS, out_rows=rows, name=name, **common)
    if rows > lat_rows:
        out = _mm(h, w, rows=rows - lat_rows, tm=CTX_LEN, seq_len=CTX_LEN, row0=lat_rows, dest=out,
                  name=name + "_ctx", **common)
    return out


def _hyena_layer(h, mod, rows, nb, j, with_ctx, p):
    u = _conv_proj(h, p["hy_in_w"][j].astype(BF16), p["hy_in_b"][j], p["hy_conv_w"][j], p["hy_conv_b"][j], mod,
                   rows, nb, "conv", F32, "hyena_in")
    fparams = (p["hy_pos_w1"][j], p["hy_pos_b1"][j], p["hy_freq"][j], p["hy_pos_w2"][j], p["hy_pos_b2"][j],
               p["hy_pos_w3"][j])
    y = _hyena_core_poly(u, nb, fparams, p["hy_bias"][j], rows)
    if with_ctx:
        y = _hyena_core(u, CTX_LEN, nb, nb * MOD_ROWS, fparams, p["hy_bias"][j], "hy_ctx", dest=y)
    return _mm(y, p["hy_out_w"][j].astype(BF16), rows=rows, tn=D_MODEL, epi="gres", mod=mod, gslot=5,
               bias=p["hy_out_b"][j], res=h, name="hyena_out")


def _mamba_layer(h, mod, rows, nb, j, p):
    w_in = p["mb_in_w"][j].astype(BF16)
    w_zdt = jnp.concatenate([w_in[:, :SSM_INNER], w_in[:, SSM_INNER + SSM_CONV_DIM:],
                             jnp.zeros((D_MODEL, SSM_ZDT_PAD - SSM_INNER - 2 * SSM_HEADS), BF16)], axis=1)
    zx = _mm(h, w_zdt, rows=rows, tn=SSM_ZDT_PAD, pro="adaln", mod=mod, slot=3, name="mamba_in_zdt")
    xbc = _conv_proj(h, w_in[:, SSM_INNER:SSM_INNER + SSM_CONV_DIM], None, p["mb_conv_w"][j], p["mb_conv_b"][j], mod,
                     rows, nb, "conv_silu", BF16, "mamba_in_xbc")
    dtr = zx[:, SSM_INNER:SSM_INNER + 2 * SSM_HEADS].reshape(rows, 2, SSM_HEADS)
    dta = jnp.transpose(dtr, (1, 0, 2))
    dtt = jnp.transpose(dtr, (1, 2, 0))
    yf, yb = _ssd(xbc, dta, dtt, p["mb_dt_bias"][j], p["mb_A_log"][j], nb=nb)
    yn = _ssd_finish(yf, yb, xbc, zx, p["mb_D"][j], p["mb_norm_w"][j], rows=rows)
    return _mm(yn, p["mb_out_w"][j].astype(BF16), rows=rows, tn=D_MODEL, epi="gres", mod=mod, gslot=5, res=h,
               name="mamba_out")


def _mla_layer(h, mod, rows, nb, j, cs, p):
    wd = p["mla_w_down"][j]
    kpe_w = wd[:, MLA_Q_RANK + MLA_KV_RANK:]
    wd = jnp.concatenate([wd, _rot_cols(kpe_w),
                          jnp.zeros((D_MODEL, MLA_DOWN_PAD - wd.shape[1] - MLA_ROPE), F32)], axis=1).astype(BF16)
    wq = p["mla_w_uq"][j].reshape(MLA_Q_RANK, MLA_HEADS, MLA_QK)
    wq = jnp.concatenate([wq, _rot_cols(wq[..., MLA_NOPE:])], axis=-1).reshape(MLA_Q_RANK, -1).astype(BF16)
    dn = _mm(h, wd, rows=rows, tn=MLA_DOWN_PAD, pro="adaln", mod=mod, slot=3, out_dtype=F32, name="mla_down")
    q = _mm(dn, wq, rows=rows, tn=1024, pro="rms", nw=p["mla_q_norm"][j], out_dtype=BF16, name="mla_uq")
    kv = _mm(dn, p["mla_w_ukv"][j].astype(BF16), rows=rows, tn=1024, x_colblock=MLA_Q_RANK // MLA_KV_RANK,
             pro="rms", nw=p["mla_kv_norm"][j], out_dtype=BF16, name="mla_ukv")
    o = _attention(q, kv, dn, cs, nb=nb)
    return _mm(o, p["mla_w_o"][j].astype(BF16), rows=rows, tn=D_MODEL, epi="gres", mod=mod, gslot=5, res=h,
               name="mla_out")


def kernel(x, c, ctx, c_ctx, ada_w, ada_b, ffn_in, ffn_out, hy_in_w, hy_in_b, hy_conv_w, hy_conv_b, hy_pos_w1, hy_pos_b1, hy_freq, hy_pos_w2, hy_pos_b2, hy_pos_w3, hy_bias, hy_out_w, hy_out_b, mb_in_w, mb_conv_w, mb_conv_b, mb_dt_bias, mb_A_log, mb_D, mb_norm_w, mb_out_w, mla_w_down, mla_q_norm, mla_w_uq, mla_kv_norm, mla_w_ukv, mla_w_o, final_norm_w):
    p = dict(hy_in_w=hy_in_w, hy_in_b=hy_in_b, hy_conv_w=hy_conv_w, hy_conv_b=hy_conv_b, hy_pos_w1=hy_pos_w1,
             hy_pos_b1=hy_pos_b1, hy_freq=hy_freq, hy_pos_w2=hy_pos_w2, hy_pos_b2=hy_pos_b2, hy_pos_w3=hy_pos_w3,
             hy_bias=hy_bias, hy_out_w=hy_out_w, hy_out_b=hy_out_b, mb_in_w=mb_in_w, mb_conv_w=mb_conv_w,
             mb_conv_b=mb_conv_b, mb_dt_bias=mb_dt_bias, mb_A_log=mb_A_log, mb_D=mb_D, mb_norm_w=mb_norm_w,
             mb_out_w=mb_out_w, mla_w_down=mla_w_down, mla_q_norm=mla_q_norm, mla_w_uq=mla_w_uq,
             mla_kv_norm=mla_kv_norm, mla_w_ukv=mla_w_ukv, mla_w_o=mla_w_o)
    nb, n_lat, _ = x.shape
    assert n_lat == MOD_ROWS and nb * ctx.shape[1] == MOD_ROWS and ctx.shape[1] == CTX_LEN
    lat_rows = nb * n_lat
    all_rows = lat_rows + MOD_ROWS
    cs = _rope_table(n_lat)

    cc = jnp.concatenate([c, c_ctx[None], jnp.zeros((16 - nb - 1, D_MODEL), F32)], axis=0)
    mods = _mod_all(cc, ada_w, ada_b).reshape(DEPTH, 16, N_MOD, D_MODEL)
    w_ffn_in, w_ffn_out = ffn_in.astype(BF16), ffn_out.astype(BF16)
    h = None

    for i in range(DEPTH):
        kind, j, last = i % N_MIXERS, i // N_MIXERS, i == DEPTH - 1
        mod = mods[i]
        ctx_needed = not (last and kind == 0)
        ctx_out = not last
        rows = all_rows if ctx_needed else lat_rows
        if i == 0:
            h = _ffn(x.reshape(lat_rows, D_MODEL), mod, w_ffn_in, w_ffn_out, i, 0, rows=lat_rows, slot=0,
                     out_rows=rows)
            if ctx_needed:
                h = _ffn(ctx.reshape(MOD_ROWS, D_MODEL), mod, w_ffn_in, w_ffn_out, i, 0, rows=MOD_ROWS, slot=0,
                         out_row0=lat_rows, dest=h)
        else:
            h = _ffn(h, mod, w_ffn_in, w_ffn_out, i, 0, rows=rows, slot=0)
        if kind == 0:
            h = _hyena_layer(h, mod, rows, nb, j, ctx_out, p)
        elif kind == 1:
            h = _mamba_layer(h, mod, rows, nb, j, p)
        else:
            h = _mla_layer(h, mod, rows, nb, j, cs, p)
        rows = all_rows if ctx_out else lat_rows
        h = _ffn(h, mod, w_ffn_in, w_ffn_out, i, 1, rows=rows, slot=6, final_w=final_norm_w if last else None)
    return h[:lat_rows].reshape(nb, n_lat, D_MODEL)
```

```python
import functools
import math

import jax
import jax.numpy as jnp
from jax import lax
from jax.experimental import pallas as pl
from jax.experimental.pallas import tpu as pltpu

F32 = jnp.float32
BF16 = jnp.bfloat16
HIGHEST = lax.Precision.HIGHEST

D_MODEL = 1024
DEPTH = 4
GRID_W = 64
CTX_LEN = 256
N_MIXERS = 3
N_MOD = 9
FFN_DIM = 2816
EPS = 1e-6

HYENA_EMB = 33
HYENA_BANDS = (HYENA_EMB - 1) // 2
HYENA_FILTER_HIDDEN = 64
HYENA_FAST_DECAY = 0.3
HYENA_SLOW_DECAY = 1.5
HYENA_DECAY_TARGET = 1e-2

SSM_INNER = 2 * D_MODEL
SSM_HEAD_DIM = 64
SSM_HEADS = SSM_INNER // SSM_HEAD_DIM
SSM_GROUPS = 8
HEADS_PER_GROUP = SSM_HEADS // SSM_GROUPS
SSM_STATE = 128
SSM_CHUNK = 128
SSM_BC_DIM = SSM_GROUPS * SSM_STATE
SSM_CONV_DIM = SSM_INNER + 2 * SSM_BC_DIM
SSM_IN_DIM = SSM_INNER + SSM_CONV_DIM + 2 * SSM_HEADS
SSM_ZDT_PAD = SSM_INNER + 128

MLA_HEADS = 16
MLA_NOPE = 64
MLA_ROPE = 32
MLA_V = 64
MLA_Q_RANK = 768
MLA_KV_RANK = 256
MLA_QK = MLA_NOPE + MLA_ROPE
MLA_SCALE = MLA_QK ** -0.5
MLA_DOWN_PAD = 1152
ROPE_AXIS = MLA_ROPE // 2
ROPE_BASE = 10000.0

MOD_ROWS = 2048
VMEM_LIMIT_BYTES = 56 * 1024 * 1024


def _params(*sem):
    return pltpu.CompilerParams(dimension_semantics=sem, vmem_limit_bytes=VMEM_LIMIT_BYTES)


def _rms(x):
    return x * lax.rsqrt(jnp.mean(x * x, axis=-1, keepdims=True) + EPS)


def _silu(x):
    return x * jax.nn.sigmoid(x)


def _dot(a, b, **kw):
    return jnp.dot(a, b, preferred_element_type=F32, **kw)


def _dot_nt(a, b):
    return lax.dot_general(a, b, (((1,), (1,)), ((), ())), preferred_element_type=F32)


def _mod_kernel(x_ref, w_ref, b_ref, o_ref):
    a = _silu(x_ref[...]).astype(BF16)
    o_ref[...] = _dot(a, w_ref[...].astype(BF16)) + b_ref[...]


def _mod_all(cc, ada_w, ada_b):
    n = N_MOD * D_MODEL
    tn = 1024
    return pl.pallas_call(
        _mod_kernel,
        grid=(DEPTH, n // tn),
        in_specs=[pl.BlockSpec((16, D_MODEL), lambda l, j: (0, 0)),
                  pl.BlockSpec((None, D_MODEL, tn), lambda l, j: (l, 0, j)),
                  pl.BlockSpec((None, 1, tn), lambda l, j: (l, 0, j))],
        out_specs=pl.BlockSpec((None, 16, tn), lambda l, j: (l, 0, j)),
        out_shape=jax.ShapeDtypeStruct((DEPTH, 16, n), F32),
        compiler_params=_params("parallel", "parallel"),
        name="adaln_mod",
    )(cc, ada_w, ada_b.reshape(DEPTH, 1, n))


HALO = 16


def _mm_kernel(*refs, pro, epi, has_bias, slot, gslot, tile0, seq_len, has_dest):
    conv = epi in ("conv", "conv_silu")
    it = iter(refs)
    x_ref, w_ref = next(it), next(it)
    xp_ref, xn_ref = (next(it), next(it)) if conv else (None, None)
    mod_ref = next(it) if (pro == "adaln" or epi == "gres") else None
    nw_ref = next(it) if pro == "rms" else None
    ssd_refs = [next(it) for _ in range(5)] if pro == "ssd" else None
    b_ref = next(it) if has_bias else None
    res_ref = next(it) if epi == "gres" else None
    cw_ref, cb_ref = (next(it), next(it)) if conv else (None, None)
    if has_dest:
        next(it)
    o_ref, xa_ref = next(it), next(it)
    tm = x_ref.shape[0]

    def prologue(x):
        x = x.astype(F32)
        if pro == "adaln":
            x = _rms(x) * (1.0 + mod_ref[0, slot + 1:slot + 2, :]) + mod_ref[0, slot:slot + 1, :]
        elif pro == "rms":
            x = _rms(x) * nw_ref[...]
        elif pro == "ssd":
            yb_ref, xs_ref, z_ref, dsk_ref, gn_ref = ssd_refs
            y = (x + yb_ref[...].astype(F32) + dsk_ref[...] * xs_ref[...].astype(F32)) * _silu(z_ref[...])
            gw = SSM_INNER // SSM_GROUPS
            x = jnp.concatenate([_rms(y[:, g * gw:(g + 1) * gw]) * gn_ref[:, g * gw:(g + 1) * gw]
                                 for g in range(SSM_GROUPS)], axis=1)
        return x

    if conv:
        row0 = (tile0 + pl.program_id(0)) * tm
        keep_top = jnp.where((row0 & (seq_len - 1)) == 0, 0.0, 1.0)
        keep_bot = jnp.where(((row0 + tm) & (seq_len - 1)) == 0, 0.0, 1.0)

    @pl.when(pl.program_id(1) == 0)
    def _():
        if conv:
            xa_ref[0:HALO, :] = (prologue(xp_ref[...]) * keep_top).astype(BF16)
            xa_ref[HALO:HALO + tm, :] = prologue(x_ref[...]).astype(BF16)
            xa_ref[HALO + tm:2 * HALO + tm, :] = (prologue(xn_ref[...]) * keep_bot).astype(BF16)
        else:
            xa_ref[...] = prologue(x_ref[...]).astype(BF16)

    acc = _dot(xa_ref[...], w_ref[...])
    if conv:
        w0, w1, w2 = cw_ref[0:1, :], cw_ref[1:2, :], cw_ref[2:3, :]
        prev = pltpu.roll(acc, 1, 0)[HALO:HALO + tm]
        nxt = pltpu.roll(acc, tm + 2 * HALO - 1, 0)[HALO:HALO + tm]
        const = cb_ref[...] + (b_ref[...] * (w0 + w1 + w2) if has_bias else 0.0)
        y = w0 * prev + w1 * acc[HALO:HALO + tm] + w2 * nxt + const
        act = (lambda v: _silu(v.astype(o_ref.dtype))) if epi == "conv_silu" else (lambda v: v)
        o_ref[...] = act(y).astype(o_ref.dtype)
        if has_bias:
            o_ref[0:1, :] = act(y[0:1] - (1.0 - keep_top) * (b_ref[...] * w0)).astype(o_ref.dtype)
            o_ref[tm - 1:tm, :] = act(y[tm - 1:tm] - (1.0 - keep_bot) * (b_ref[...] * w2)).astype(o_ref.dtype)
        return
    if has_bias:
        acc = acc + b_ref[...]
    if epi == "gres":
        acc = res_ref[...] + mod_ref[0, gslot:gslot + 1, :] * acc
    o_ref[...] = acc.astype(o_ref.dtype)


def _mm(x, w, *, rows, tn, tm=1024, x_colblock=0, pro="none", epi="none", mod=None, slot=0, gslot=0,
        nw=None, ssd=None, bias=None, res=None, conv_w=None, conv_b=None, seq_len=None, row0=0, out_rows=None,
        dest=None, out_dtype=F32, name="mm"):
    k, n = w.shape
    assert rows % tm == 0 and row0 % tm == 0 and n % tn == 0 and MOD_ROWS % tm == 0
    if epi == "gres":
        assert tn == n == D_MODEL
    conv = epi in ("conv", "conv_silu")
    per = MOD_ROWS // tm
    t0 = row0 // tm
    ins = [x, w]
    specs = [pl.BlockSpec((tm, k), lambda i, j: (t0 + i, x_colblock)),
             pl.BlockSpec((k, tn), lambda i, j: (0, j))]
    if conv:
        assert x_colblock == 0 and seq_len % tm == 0 and (bias is None or out_dtype == F32)
        hb = tm // HALO
        last = (row0 + rows) // HALO - 1
        ins += [x, x]
        specs += [pl.BlockSpec((HALO, k), lambda i, j: (jnp.maximum((t0 + i) * hb - 1, 0), 0)),
                  pl.BlockSpec((HALO, k), lambda i, j: (jnp.minimum((t0 + i + 1) * hb, last), 0))]
    if pro == "adaln" or epi == "gres":
        ins.append(mod)
        specs.append(pl.BlockSpec((1, N_MOD, D_MODEL), lambda i, j: ((t0 + i) // per, 0, 0)))
    if pro == "rms":
        ins.append(nw.reshape(1, k))
        specs.append(pl.BlockSpec((1, k), lambda i, j: (0, 0)))
    if pro == "ssd":
        yb, xs, z, d_skip, norm_w = ssd
        ins += [yb, xs, z, d_skip.reshape(1, k), norm_w.reshape(1, k)]
        specs += [pl.BlockSpec((tm, k), lambda i, j: (t0 + i, 0))] * 3 + [pl.BlockSpec((1, k), lambda i, j: (0, 0))] * 2
    if bias is not None:
        ins.append(bias.reshape(1, n))
        specs.append(pl.BlockSpec((1, tn), lambda i, j: (0, j)))
    if epi == "gres":
        ins.append(res)
        specs.append(pl.BlockSpec((tm, tn), lambda i, j: (t0 + i, j)))
    if conv:
        ins += [conv_w, conv_b.reshape(1, n)]
        specs += [pl.BlockSpec((3, tn), lambda i, j: (0, j)), pl.BlockSpec((1, tn), lambda i, j: (0, j))]
    aliases = {}
    out_shape = jax.ShapeDtypeStruct((out_rows or row0 + rows, n), out_dtype)
    if dest is not None:
        aliases = {len(ins): 0}
        ins.append(dest)
        specs.append(pl.BlockSpec(memory_space=pl.ANY))
        out_shape = jax.ShapeDtypeStruct(dest.shape, dest.dtype)
    return pl.pallas_call(
        functools.partial(_mm_kernel, pro=pro, epi=epi, has_bias=bias is not None, slot=slot, gslot=gslot,
                          tile0=t0, seq_len=seq_len, has_dest=dest is not None),
        grid=(rows // tm, n // tn),
        in_specs=specs,
        out_specs=pl.BlockSpec((tm, tn), lambda i, j: (t0 + i, j)),
        out_shape=out_shape,
        scratch_shapes=[pltpu.VMEM((tm + 2 * HALO if conv else tm, k), BF16)],
        input_output_aliases=aliases,
        compiler_params=_params("parallel", "arbitrary"),
        name=name,
    )(*ins)


def _ffn_kernel(*refs, slot, final, fc, has_dest):
    if has_dest:
        refs = refs[:-2] + refs[-1:]
    if final:
        x_ref, mod_ref, wi_ref, wo_ref, fw_ref, o_ref = refs
    else:
        x_ref, mod_ref, wi_ref, wo_ref, o_ref = refs
    x = x_ref[...]
    a = (_rms(x) * (1.0 + mod_ref[0, slot + 1:slot + 2, :]) + mod_ref[0, slot:slot + 1, :]).astype(BF16)
    acc = None
    for c in range(0, FFN_DIM, fc):
        g = _dot(a, wi_ref[:, c:c + fc])
        u = _dot(a, wi_ref[:, FFN_DIM + c:FFN_DIM + c + fc])
        t = _dot((_silu(g) * u).astype(BF16), wo_ref[c:c + fc, :])
        acc = t if acc is None else acc + t
    out = x + (0.5 * mod_ref[0, slot + 2:slot + 3, :]) * acc
    if final:
        out = _rms(out) * fw_ref[...]
    o_ref[...] = out


def _ffn(h, mod, w_in, w_out, layer, k, *, rows, slot, final_w=None, out_row0=0, out_rows=None, dest=None,
         tm=512, fc=FFN_DIM):
    per = MOD_ROWS // tm
    o0 = out_row0 // tm
    final = final_w is not None
    resident = pl.Buffered(1)
    ins = [h, mod, w_in, w_out]
    specs = [pl.BlockSpec((tm, D_MODEL), lambda i: (i, 0)),
             pl.BlockSpec((1, N_MOD, D_MODEL), lambda i: ((o0 + i) // per, 0, 0)),
             pl.BlockSpec((None, None, D_MODEL, 2 * FFN_DIM), lambda i: (layer, k, 0, 0), pipeline_mode=resident),
             pl.BlockSpec((None, None, FFN_DIM, D_MODEL), lambda i: (layer, k, 0, 0), pipeline_mode=resident)]
    if final:
        ins.append(final_w.reshape(1, D_MODEL))
        specs.append(pl.BlockSpec((1, D_MODEL), lambda i: (0, 0)))
    aliases = {}
    out_shape = jax.ShapeDtypeStruct((out_rows or rows, D_MODEL), F32)
    if dest is not None:
        aliases = {len(ins): 0}
        ins.append(dest)
        specs.append(pl.BlockSpec(memory_space=pl.ANY))
        out_shape = jax.ShapeDtypeStruct(dest.shape, dest.dtype)
    return pl.pallas_call(
        functools.partial(_ffn_kernel, slot=slot, final=final, fc=fc, has_dest=dest is not None),
        grid=(rows // tm,),
        in_specs=specs,
        out_specs=pl.BlockSpec((tm, D_MODEL), lambda i: (o0 + i, 0)),
        out_shape=out_shape,
        input_output_aliases=aliases,
        compiler_params=_params("parallel"),
        name="ffn",
    )(*ins)


FILTER_W = 128


def _filter_kernel(za_ref, zb_ref, w1_ref, b1_ref, fr_ref, w2_ref, b2_ref, w3f_ref, w3b_ref, dl_ref,
                   sum_ref, diff_ref):
    fr = fr_ref[...]

    def taps(z_ref, w3_ref):
        z = z_ref[...]
        hid = jnp.sin(fr * (_dot(z, w1_ref[...], precision=HIGHEST) + b1_ref[...]))
        hid = jnp.sin(fr * (_dot(hid, w2_ref[...], precision=HIGHEST) + b2_ref[...]))
        window = jnp.exp(-z[:, 0:1] * dl_ref[...]) * z[:, FILTER_W - 1:FILTER_W]
        return _dot(hid.astype(BF16), w3_ref[...].astype(BF16)) * window

    hf = taps(za_ref, w3f_ref)
    hb = taps(zb_ref, w3b_ref)
    sum_ref[...] = (hf + hb).astype(BF16)
    diff_ref[...] = (hb - hf).astype(BF16)


def _hyena_filters(L, poly, w1, b1, freq, w2, b2, w3):
    S = L // poly
    nrho = 2 * poly - 1
    jj = jnp.arange(S, dtype=jnp.int32)
    rho = jnp.arange(-(poly - 1), poly, dtype=jnp.int32)
    t_tab = jnp.linspace(0.0, 1.0, L, dtype=F32)
    bands = jnp.linspace(1e-4, HYENA_BANDS - 1, HYENA_BANDS, dtype=F32)

    def feats(pos, lowest):
        valid = (pos >= lowest).astype(F32)
        pc = jnp.maximum(pos, 0)
        ang = (2.0 * math.pi / L) * pc.astype(F32)[..., None] * bands
        z = jnp.concatenate([jnp.take(t_tab, pc)[..., None], jnp.cos(ang), -jnp.sin(ang)], axis=-1)
        z = jnp.pad(z, ((0, 0), (0, 0), (0, FILTER_W - 1 - HYENA_EMB)))
        return jnp.concatenate([z, valid[..., None]], axis=-1).reshape(nrho * S, FILTER_W)

    za = feats(poly * jj[None, :] + rho[:, None], 0)
    zb = feats(poly * jj[None, :] - rho[:, None], 1)
    ph = FILTER_W - HYENA_FILTER_HIDDEN
    w1 = jnp.pad(w1, ((0, FILTER_W - HYENA_EMB), (0, ph)))
    w2 = jnp.pad(w2, ((0, ph), (0, ph)))
    w3 = jnp.pad(w3, ((0, ph), (0, 0))).reshape(FILTER_W, 2, 2, D_MODEL)
    w3f = w3[:, :, 0].reshape(FILTER_W, 2 * D_MODEL)
    w3b = w3[:, :, 1].reshape(FILTER_W, 2 * D_MODEL)
    b1, b2, freq = (jnp.pad(t, (0, ph)).reshape(1, FILTER_W) for t in (b1, b2, freq))
    max_decay = math.log(HYENA_DECAY_TARGET) / HYENA_FAST_DECAY
    min_decay = math.log(HYENA_DECAY_TARGET) / HYENA_SLOW_DECAY
    deltas = jnp.abs(jnp.linspace(min_decay, max_decay, D_MODEL, dtype=F32))
    deltas = jnp.tile(deltas, 2).reshape(1, 2 * D_MODEL)
    full = lambda shape: pl.BlockSpec(shape, lambda i: (0, 0))
    sq = full((FILTER_W, FILTER_W))
    vec = full((1, FILTER_W))
    return pl.pallas_call(
        _filter_kernel,
        grid=(nrho,),
        in_specs=[pl.BlockSpec((S, FILTER_W), lambda i: (i, 0)), pl.BlockSpec((S, FILTER_W), lambda i: (i, 0)),
                  sq, vec, vec, sq, vec, full((FILTER_W, 2 * D_MODEL)), full((FILTER_W, 2 * D_MODEL)),
                  full((1, 2 * D_MODEL))],
        out_specs=[pl.BlockSpec((S, 2 * D_MODEL), lambda i: (i, 0))] * 2,
        out_shape=[jax.ShapeDtypeStruct((nrho * S, 2 * D_MODEL), BF16)] * 2,
        compiler_params=_params("parallel"),
        name="hyena_filter",
    )(za, zb, w1, b1, freq, w2, b2, w3f, w3b, deltas)


def _lmm_kernel(*refs, epi, tf, scale, has_dest):
    it = iter(refs)
    a_ref, x_ref = next(it), next(it)
    if epi == "spec":
        k_ref = next(it)
    elif epi == "gate":
        g_ref, v_ref, bias_ref = next(it), next(it), next(it)
    if has_dest:
        next(it)
    o_ref, xb_ref = next(it), next(it)

    @pl.when(pl.program_id(2) == 0)
    def _():
        xb_ref[...] = x_ref[...].astype(BF16)

    acc = _dot(a_ref[...], xb_ref[...])
    if epi == "spec":
        xr, xs = acc[:tf], acc[tf:]
        kr, ki = k_ref[0:tf, :], k_ref[tf:2 * tf, :]
        o_ref[0:tf, :] = ((xr * kr + xs * ki) * scale).astype(o_ref.dtype)
        o_ref[tf:2 * tf, :] = ((xs * kr - xr * ki) * scale).astype(o_ref.dtype)
    elif epi == "gate":
        o_ref[...] = (g_ref[...] * (acc + v_ref[...].astype(F32) * bias_ref[...])).astype(o_ref.dtype)
    else:
        o_ref[...] = acc.astype(o_ref.dtype)


def _lmm(a, x, *, nb, ncb, tm, n, x_row0=0, x_cb0=0, epi="none", tf=0, scale=1.0, kspec=None, k_cb=0,
         g=None, g_cb=0, g_row0=0, v=None, v_cb=0, v_row0=0, bias=None, out_dtype=F32, dest=None, dest_row0=0,
         name="lmm"):
    mo, k = a.shape
    nm = mo // tm
    xr0 = x_row0 // k
    or0 = dest_row0 // tm
    ins = [a, x]
    specs = [pl.BlockSpec((tm, k), lambda b, c, m: (m, 0)),
             pl.BlockSpec((k, n), lambda b, c, m: (xr0 + b, x_cb0 + c))]
    if epi == "spec":
        ins.append(kspec)
        specs.append(pl.BlockSpec((tm, n), lambda b, c, m: (m, k_cb)))
    elif epi == "gate":
        gr0, vr0 = g_row0 // tm, v_row0 // tm
        ins += [g, v, bias.reshape(1, n)]
        specs += [pl.BlockSpec((tm, n), lambda b, c, m: (gr0 + b * nm + m, g_cb)),
                  pl.BlockSpec((tm, n), lambda b, c, m: (vr0 + b * nm + m, v_cb)),
                  pl.BlockSpec((1, n), lambda b, c, m: (0, 0))]
    aliases = {}
    out_shape = jax.ShapeDtypeStruct((nb * mo, ncb * n), out_dtype)
    if dest is not None:
        aliases = {len(ins): 0}
        ins.append(dest)
        specs.append(pl.BlockSpec(memory_space=pl.ANY))
        out_shape = jax.ShapeDtypeStruct(dest.shape, dest.dtype)
    return pl.pallas_call(
        functools.partial(_lmm_kernel, epi=epi, tf=tf, scale=scale, has_dest=dest is not None),
        grid=(nb, ncb, nm),
        in_specs=specs,
        out_specs=pl.BlockSpec((tm, n), lambda b, c, m: (or0 + b * nm + m, c)),
        out_shape=out_shape,
        scratch_shapes=[pltpu.VMEM((k, n), BF16)],
        input_output_aliases=aliases,
        compiler_params=_params("parallel", "parallel", "arbitrary"),
        name=name,
    )(*ins)


def _dft_mats(L, tf):
    idx = jnp.arange(L, dtype=jnp.int32)
    ph = ((2 * idx[:, None] + 1) * idx[None, :]) % (4 * L)
    ang = ph.astype(F32) * (2.0 * math.pi / (4 * L))
    c, s = jnp.cos(ang), jnp.sin(ang)
    nf = L // tf
    fwd = jnp.stack([c.reshape(nf, tf, L), s.reshape(nf, tf, L)], axis=1).reshape(2 * L, L)
    inv = jnp.stack([c.T.reshape(L, nf, tf), s.T.reshape(L, nf, tf)], axis=2).reshape(L, 2 * L)
    return c.astype(BF16), s.astype(BF16), fwd.astype(BF16), inv.astype(BF16)


def _interleave(kr, ki, tf):
    L, n = kr.shape
    return jnp.stack([kr.reshape(L // tf, tf, n), ki.reshape(L // tf, tf, n)], axis=1).reshape(2 * L, n)


def _hyena_core(u, L, nb, row0, fparams, hy_bias, name, dest=None):
    tf = min(L, 256)
    tmi = min(L, 256)
    cmat, smat, fwd, inv = _dft_mats(L, tf)
    hsum, hdiff = _hyena_filters(L, 1, *fparams)
    kr = _lmm(cmat, hsum, nb=1, ncb=2, tm=tf, n=D_MODEL, name=name + "_kr")
    ki = _lmm(smat, hdiff, nb=1, ncb=2, tm=tf, n=D_MODEL, name=name + "_ki")
    kspec = _interleave(kr, ki, tf)
    scale = 1.0 / L
    y1 = _lmm(fwd, u, nb=nb, ncb=1, tm=2 * tf, n=D_MODEL, x_row0=row0, x_cb0=2, epi="spec", tf=tf, scale=scale,
              kspec=kspec, k_cb=0, out_dtype=BF16, name=name + "_fwd1")
    z = _lmm(inv, y1, nb=nb, ncb=1, tm=tmi, n=D_MODEL, epi="gate", g=u, g_cb=0, g_row0=row0, v=u, v_cb=2,
             v_row0=row0, bias=hy_bias[0], out_dtype=F32, name=name + "_inv1")
    y2 = _lmm(fwd, z, nb=nb, ncb=1, tm=2 * tf, n=D_MODEL, epi="spec", tf=tf, scale=scale,
              kspec=kspec, k_cb=1, out_dtype=BF16, name=name + "_fwd2")
    return _lmm(inv, y2, nb=nb, ncb=1, tm=tmi, n=D_MODEL, epi="gate", g=u, g_cb=1, g_row0=row0, v=z, v_cb=0,
                bias=hy_bias[1], out_dtype=F32, dest=dest, dest_row0=row0, name=name + "_inv2")


POLY = 4


def _poly_conv_kernel(a_ref, ai_ref, u_ref, kr_ref, ki_ref, g_ref, bias_ref, o_ref, *, scale):
    S = a_ref.shape[1]
    tc = u_ref.shape[1]
    phases = [u_ref[pl.ds(r, S, stride=POLY), :] for r in range(POLY)]
    e = _dot(a_ref[...], jnp.concatenate([ph.astype(BF16) for ph in phases], axis=1))
    f = []
    for r in range(POLY):
        fre = fim = None
        for rp in range(POLY):
            kr = kr_ref[r - rp + POLY - 1]
            ki = ki_ref[r - rp + POLY - 1]
            er = e[0:S, rp * tc:(rp + 1) * tc]
            es = e[S:2 * S, rp * tc:(rp + 1) * tc]
            tre = kr * er + ki * es
            tim = kr * es - ki * er
            fre = tre if fre is None else fre + tre
            fim = tim if fim is None else fim + tim
        f.append(jnp.concatenate([fre, fim], axis=0).astype(BF16))
    y = _dot(ai_ref[...], jnp.concatenate(f, axis=1)) * scale
    for r in range(POLY):
        out = g_ref[pl.ds(r, S, stride=POLY), :] * (y[:, r * tc:(r + 1) * tc] + phases[r] * bias_ref[...])
        o_ref[pl.ds(r, S, stride=POLY), :] = out


def _poly_conv(fwd, inv, x, x_cb, g, g_cb, kre, kim, order, bias, *, nb, out_rows=None, tc=128, name="hy_conv"):
    L = MOD_ROWS
    S = L // POLY
    nct = D_MODEL // tc
    nrho = 2 * POLY - 1
    return pl.pallas_call(
        functools.partial(_poly_conv_kernel, scale=1.0 / S),
        grid=(nct, nb),
        in_specs=[pl.BlockSpec((2 * S, S), lambda c, b: (0, 0)),
                  pl.BlockSpec((S, 2 * S), lambda c, b: (0, 0)),
                  pl.BlockSpec((L, tc), lambda c, b: (b, x_cb * nct + c)),
                  pl.BlockSpec((nrho, S, tc), lambda c, b: (0, 0, order * nct + c)),
                  pl.BlockSpec((nrho, S, tc), lambda c, b: (0, 0, order * nct + c)),
                  pl.BlockSpec((L, tc), lambda c, b: (b, g_cb * nct + c)),
                  pl.BlockSpec((1, tc), lambda c, b: (0, c))],
        out_specs=pl.BlockSpec((L, tc), lambda c, b: (b, c)),
        out_shape=jax.ShapeDtypeStruct((out_rows or nb * L, D_MODEL), F32),
        compiler_params=_params("parallel", "parallel"),
        name=name,
    )(fwd, inv, x, kre, kim, g, bias.reshape(1, D_MODEL))


def _hyena_core_poly(u, nb, fparams, hy_bias, out_rows):
    S = MOD_ROWS // POLY
    nrho = 2 * POLY - 1
    cmat, smat, fwd, inv = _dft_mats(S, S)
    ksum, kdiff = _hyena_filters(MOD_ROWS, POLY, *fparams)
    kre = _lmm(cmat, ksum, nb=nrho, ncb=2, tm=S, n=D_MODEL, name="hy_lat_kr").reshape(nrho, S, 2 * D_MODEL)
    kim = _lmm(smat, kdiff, nb=nrho, ncb=2, tm=S, n=D_MODEL, name="hy_lat_ki").reshape(nrho, S, 2 * D_MODEL)
    z = _poly_conv(fwd, inv, u, 2, u, 0, kre, kim, 0, hy_bias[0], nb=nb, name="hy_lat_conv1")
    return _poly_conv(fwd, inv, z, 0, u, 1, kre, kim, 1, hy_bias[1], nb=nb, out_rows=out_rows, name="hy_lat_conv2")


def _softplus(x):
    return jnp.maximum(x, 0.0) + jnp.log(1.0 + jnp.exp(-jnp.abs(x)))


def _ssd_kernel(xs_f, b_f, c_f, dt_f, dtt_f, xs_b, b_b, c_b, dt_b, dtt_b, dtb_ref, dtbt_ref, al_ref, alt_ref,
                yf_ref, yb_ref, st_ref):
    @pl.when(pl.program_id(1) == 0)
    def _():
        st_ref[...] = jnp.zeros_like(st_ref)

    _ssd_chunk(0, xs_f, b_f, c_f, dt_f, dtt_f, dtb_ref[0], dtbt_ref[0], al_ref[0], alt_ref[0], yf_ref, st_ref.at[0])
    _ssd_chunk(1, xs_b, b_b, c_b, dt_b, dtt_b, dtb_ref[1], dtbt_ref[1], al_ref[1], alt_ref[1], yb_ref, st_ref.at[1])


def _ssd_chunk(d, xs_ref, b_ref, c_ref, dt_ref, dtt_ref, dt_bias, dt_bias_t, a_log, a_log_t, y_ref, st_ref):
    T = SSM_CHUNK
    N = SSM_STATE
    ri = lax.broadcasted_iota(jnp.int32, (T, T), 0)
    ci = lax.broadcasted_iota(jnp.int32, (T, T), 1)
    mask = (ri >= ci) if d == 0 else (ri <= ci)
    tri_col = mask.astype(F32)
    tri_row = ((ci >= ri) if d == 0 else (ci <= ri)).astype(F32)
    low = lax.broadcasted_iota(jnp.int32, (T, 2 * SSM_HEAD_DIM), 1) < SSM_HEAD_DIM

    a = _softplus(dt_ref[...] + dt_bias) * -jnp.exp(a_log)
    dtt = _softplus(dtt_ref[...] + dt_bias_t)
    at = dtt * -jnp.exp(a_log_t)
    cs_col = _dot(tri_col, a, precision=HIGHEST)
    cs_row = _dot(at, tri_row, precision=HIGHEST)
    tot = jnp.sum(at, axis=1, keepdims=True)
    row_dt = cs_row - jnp.log(dtt)
    w_out = dtt * jnp.exp(tot - cs_row)
    e_tot = jnp.exp(tot)

    for g in range(SSM_GROUPS):
        bt = jnp.transpose(b_ref[:, g * N:(g + 1) * N].astype(F32))
        cg = c_ref[:, g * N:(g + 1) * N].astype(F32)
        cb = _dot(cg.astype(BF16), bt.astype(BF16))
        for k in range(HEADS_PER_GROUP // 2):
            h0 = g * HEADS_PER_GROUP + 2 * k
            lanes = slice(h0 * SSM_HEAD_DIM, (h0 + 2) * SSM_HEAD_DIM)
            xs = xs_ref[:, lanes].astype(BF16)
            st = st_ref[g, :, k * 128:(k + 1) * 128]
            rhs = jnp.concatenate([xs, st.astype(BF16)], axis=0)
            ys, upds = [], []
            for h in (h0, h0 + 1):
                col = jnp.broadcast_to(cs_col[:, h:h + 1], (T, T))
                m = jnp.where(mask, jnp.exp(col - row_dt[h:h + 1, :]), 0.0) * cb
                lhs = jnp.concatenate([m, jnp.exp(col) * cg], axis=1).astype(BF16)
                ys.append(_dot(lhs, rhs))
                upds.append(_dot((bt * w_out[h:h + 1, :]).astype(BF16), xs))
            y_ref[:, lanes] = jnp.where(low, ys[0], ys[1]).astype(y_ref.dtype)
            decay = jnp.where(low, e_tot[h0:h0 + 1, :], e_tot[h0 + 1:h0 + 2, :])
            st_ref[g, :, k * 128:(k + 1) * 128] = st * decay + jnp.where(low, upds[0], upds[1])


def _ssd(xbc, dta, dtt, dt_bias, a_log, *, nb):
    m = xbc.shape[0]
    T = SSM_CHUNK
    H = SSM_HEADS
    lat_c = MOD_ROWS // T
    ctx_c = CTX_LEN // T
    ctx0 = nb * lat_c
    steps = lat_c + ctx_c

    def rb(b, d, t):
        ctx = ctx0 + ctx_c * b + jnp.where(d == 0, t, ctx_c - 1 - t)
        lat = lat_c * b + jnp.where(d == 0, t - ctx_c, steps - 1 - t)
        return jnp.where(t < ctx_c, ctx, lat)

    def direction(d):
        return [pl.BlockSpec((T, SSM_INNER), lambda b, t: (rb(b, d, t), 0)),
                pl.BlockSpec((T, SSM_BC_DIM), lambda b, t: (rb(b, d, t), 2)),
                pl.BlockSpec((T, SSM_BC_DIM), lambda b, t: (rb(b, d, t), 3)),
                pl.BlockSpec((None, T, H), lambda b, t: (d, rb(b, d, t), 0)),
                pl.BlockSpec((None, H, T), lambda b, t: (d, 0, rb(b, d, t)))]

    whole = lambda shape: pl.BlockSpec(shape, lambda b, t: (0, 0, 0))
    return pl.pallas_call(
        _ssd_kernel,
        grid=(nb, steps),
        in_specs=direction(0) + direction(1) + [whole((2, 1, H)), whole((2, H, 1)), whole((2, 1, H)), whole((2, H, 1))],
        out_specs=[pl.BlockSpec((T, SSM_INNER), lambda b, t: (rb(b, 0, t), 0)),
                   pl.BlockSpec((T, SSM_INNER), lambda b, t: (rb(b, 1, t), 0))],
        out_shape=[jax.ShapeDtypeStruct((m, SSM_INNER), BF16)] * 2,
        scratch_shapes=[pltpu.VMEM((2, SSM_GROUPS, SSM_STATE, HEADS_PER_GROUP * SSM_HEAD_DIM), F32)],
        compiler_params=_params("parallel", "arbitrary"),
        name="ssd_scan",
    )(xbc, xbc, xbc, dta, dtt, xbc, xbc, xbc, dta, dtt, dt_bias.reshape(2, 1, H), dt_bias.reshape(2, H, 1),
      a_log.reshape(2, 1, H), a_log.reshape(2, H, 1))


HEAD_W = 2 * MLA_NOPE
ATTN_SUB_ROWS = 256


def _attn_kernel(*refs, has_lat):
    if has_lat:
        q_ref, kvc_ref, krc_ref, kv_ref, kr_ref, tq_ref, tk_ref, o_ref, kc_scr, vc_scr, k_scr, v_scr = refs
    else:
        q_ref, kvc_ref, krc_ref, _, o_ref, kc_scr, vc_scr = refs
    R = MLA_ROPE
    kscale = MLA_SCALE * math.log2(math.e)

    def build_keys():
        krc = krc_ref[...].astype(F32)
        lane_c = lax.broadcasted_iota(jnp.int32, krc.shape, 1)
        pe_c = jnp.where((lane_c >= MLA_NOPE) & (lane_c < MLA_NOPE + R), pltpu.roll(krc, MLA_NOPE, 1), 0.0)
        for hh in range(2):
            kvh = kvc_ref[:, hh * HEAD_W:(hh + 1) * HEAD_W].astype(F32)
            kc_scr[hh] = (jnp.where(lane_c < MLA_NOPE, kvh, pe_c) * kscale).astype(BF16)
            vc_scr[hh] = jnp.where(lane_c < MLA_NOPE, 1.0, kvh).astype(BF16)
        if has_lat:
            t = kr_ref[...].astype(F32) * tk_ref[...]
            lane = lax.broadcasted_iota(jnp.int32, t.shape, 1)
            krot = jnp.where(lane < R, t + pltpu.roll(t, HEAD_W - R, 1), 0.0)
            krr = pltpu.roll(krot, MLA_NOPE, 1) + pltpu.roll(krot, MLA_NOPE + R, 1)
            for hh in range(2):
                kvh = kv_ref[:, hh * HEAD_W:(hh + 1) * HEAD_W].astype(F32)
                k_scr[hh] = (jnp.where(lane < MLA_NOPE, kvh, krr) * kscale).astype(BF16)
                v_scr[hh] = jnp.where(lane < MLA_NOPE, 1.0, kvh).astype(BF16)

    if has_lat:
        pl.when(pl.program_id(2) == 0)(build_keys)
    else:
        build_keys()

    sub = min(q_ref.shape[0], ATTN_SUB_ROWS)
    for r0 in range(0, q_ref.shape[0], sub):
        rows = slice(r0, r0 + sub)
        res = []
        for hh in range(2):
            q = q_ref[rows, hh * HEAD_W:(hh + 1) * HEAD_W]
            s_c = _dot_nt(q, kc_scr[hh])
            mx = jnp.max(s_c, axis=-1, keepdims=True)
            if has_lat:
                ql = (q.astype(F32) * tq_ref[rows, :]).astype(BF16)
                s_l = _dot_nt(ql, k_scr[hh])
                mx = jnp.maximum(mx, jnp.max(s_l, axis=-1, keepdims=True))
                acc = _dot(jnp.exp2(s_l - mx).astype(BF16), v_scr[hh])
                acc = acc + _dot(jnp.exp2(s_c - mx).astype(BF16), vc_scr[hh])
            else:
                acc = _dot(jnp.exp2(s_c - mx).astype(BF16), vc_scr[hh])
            res.append(acc / pltpu.roll(acc, MLA_V, 1))
        lane_o = lax.broadcasted_iota(jnp.int32, res[0].shape, 1)
        o_ref[rows, :] = jnp.where(lane_o < MLA_V, pltpu.roll(res[0], MLA_V, 1), res[1]).astype(o_ref.dtype)


def _attention(q, kv, dn, cs, *, nb, tq=1024):
    L = MOD_ROWS
    hp = MLA_HEADS // 2
    w = 2 * HEAD_W
    nq = L // tq
    cb = nb * (L // CTX_LEN)
    kr_cb = (MLA_Q_RANK + MLA_KV_RANK) // HEAD_W
    tab_q = jnp.concatenate([jnp.ones((L, MLA_NOPE), F32), cs], axis=1)
    tab_k = jnp.concatenate([cs, jnp.zeros((L, HEAD_W - 2 * MLA_ROPE), F32)], axis=1)
    lat = pl.pallas_call(
        functools.partial(_attn_kernel, has_lat=True),
        grid=(nb, hp, nq),
        in_specs=[pl.BlockSpec((tq, w), lambda b, p, i: (b * nq + i, p)),
                  pl.BlockSpec((CTX_LEN, w), lambda b, p, i: (cb + b, p)),
                  pl.BlockSpec((CTX_LEN, HEAD_W), lambda b, p, i: (cb + b, kr_cb)),
                  pl.BlockSpec((L, w), lambda b, p, i: (b, p)),
                  pl.BlockSpec((L, HEAD_W), lambda b, p, i: (b, kr_cb)),
                  pl.BlockSpec((tq, HEAD_W), lambda b, p, i: (i, 0)),
                  pl.BlockSpec((L, HEAD_W), lambda b, p, i: (0, 0))],
        out_specs=pl.BlockSpec((tq, 2 * MLA_V), lambda b, p, i: (b * nq + i, p)),
        out_shape=jax.ShapeDtypeStruct((nb * (L + CTX_LEN), MLA_HEADS * MLA_V), BF16),
        scratch_shapes=[pltpu.VMEM((2, CTX_LEN, HEAD_W), BF16)] * 2 + [pltpu.VMEM((2, L, HEAD_W), BF16)] * 2,
        compiler_params=_params("parallel", "parallel", "arbitrary"),
        name="mla_attn",
    )(q, kv, dn, kv, dn, tab_q, tab_k)
    return pl.pallas_call(
        functools.partial(_attn_kernel, has_lat=False),
        grid=(nb, hp),
        in_specs=[pl.BlockSpec((CTX_LEN, w), lambda b, p: (cb + b, p)),
                  pl.BlockSpec((CTX_LEN, w), lambda b, p: (cb + b, p)),
                  pl.BlockSpec((CTX_LEN, HEAD_W), lambda b, p: (cb + b, kr_cb)),
                  pl.BlockSpec(memory_space=pl.ANY)],
        out_specs=pl.BlockSpec((CTX_LEN, 2 * MLA_V), lambda b, p: (cb + b, p)),
        out_shape=jax.ShapeDtypeStruct(lat.shape, lat.dtype),
        scratch_shapes=[pltpu.VMEM((2, CTX_LEN, HEAD_W), BF16)] * 2,
        input_output_aliases={3: 0},
        compiler_params=_params("parallel", "parallel"),
        name="mla_attn_ctx",
    )(q, kv, dn, lat)


def _rot_cols(w):
    wp = w.reshape(w.shape[:-1] + (2, 2, ROPE_AXIS // 2))
    return jnp.stack([-wp[..., 1, :], wp[..., 0, :]], axis=-2).reshape(w.shape)


def _rope_table(n_lat):
    rows = n_lat // GRID_W
    row = jnp.repeat(jnp.arange(rows), GRID_W)
    col = jnp.tile(jnp.arange(GRID_W), rows)
    inv = 1.0 / (ROPE_BASE ** (jnp.arange(0, ROPE_AXIS, 2, dtype=F32) / ROPE_AXIS))
    ar = row.astype(F32)[:, None] * inv[None, :]
    ac = col.astype(F32)[:, None] * inv[None, :]
    ang = jnp.concatenate([ar, ar, ac, ac], axis=-1)
    return jnp.concatenate([jnp.cos(ang), jnp.sin(ang)], axis=-1)


def _conv_proj(h, w, bias, conv_w, conv_b, mod, rows, nb, epi, out_dtype, name):
    lat_rows = nb * MOD_ROWS
    common = dict(pro="adaln", mod=mod, slot=3, bias=bias, epi=epi, conv_w=conv_w, conv_b=conv_b, out_dtype=out_dtype)
    out = _mm(h, w, rows=lat_rows, tn=1024, seq_len=MOD_ROWS, out_rows=rows, name=name, **common)
    if rows > lat_rows:
        out = _mm(h, w, rows=rows - lat_rows, tn=w.shape[1], tm=CTX_LEN, seq_len=CTX_LEN, row0=lat_rows, dest=out,
                  name=name + "_ctx", **common)
    return out


def _hyena_layer(h, mod, rows, nb, j, with_ctx, p):
    u = _conv_proj(h, p["hy_in_w"][j].astype(BF16), p["hy_in_b"][j], p["hy_conv_w"][j], p["hy_conv_b"][j], mod,
                   rows, nb, "conv", F32, "hyena_in")
    fparams = (p["hy_pos_w1"][j], p["hy_pos_b1"][j], p["hy_freq"][j], p["hy_pos_w2"][j], p["hy_pos_b2"][j],
               p["hy_pos_w3"][j])
    y = _hyena_core_poly(u, nb, fparams, p["hy_bias"][j], rows)
    if with_ctx:
        y = _hyena_core(u, CTX_LEN, nb, nb * MOD_ROWS, fparams, p["hy_bias"][j], "hy_ctx", dest=y)
    return _mm(y, p["hy_out_w"][j].astype(BF16), rows=rows, tn=D_MODEL, epi="gres", mod=mod, gslot=5,
               bias=p["hy_out_b"][j], res=h, name="hyena_out")


def _mamba_layer(h, mod, rows, nb, j, p):
    w_in = p["mb_in_w"][j].astype(BF16)
    w_zdt = jnp.concatenate([w_in[:, :SSM_INNER], w_in[:, SSM_INNER + SSM_CONV_DIM:],
                             jnp.zeros((D_MODEL, SSM_ZDT_PAD - SSM_INNER - 2 * SSM_HEADS), BF16)], axis=1)
    zx = _mm(h, w_zdt, rows=rows, tn=SSM_ZDT_PAD, pro="adaln", mod=mod, slot=3, name="mamba_in_zdt")
    xbc = _conv_proj(h, w_in[:, SSM_INNER:SSM_INNER + SSM_CONV_DIM], None, p["mb_conv_w"][j], p["mb_conv_b"][j], mod,
                     rows, nb, "conv_silu", BF16, "mamba_in_xbc")
    dtr = zx[:, SSM_INNER:SSM_INNER + 2 * SSM_HEADS].reshape(rows, 2, SSM_HEADS)
    dta = jnp.transpose(dtr, (1, 0, 2))
    dtt = jnp.transpose(dtr, (1, 2, 0))
    yf, yb = _ssd(xbc, dta, dtt, p["mb_dt_bias"][j], p["mb_A_log"][j], nb=nb)
    d_skip = jnp.repeat(p["mb_D"][j], SSM_HEAD_DIM)
    return _mm(yf, p["mb_out_w"][j].astype(BF16), rows=rows, tn=D_MODEL, tm=512, pro="ssd",
               ssd=(yb, xbc, zx, d_skip, p["mb_norm_w"][j]), epi="gres", mod=mod, gslot=5, res=h, name="mamba_out")


def _mla_layer(h, mod, rows, nb, j, cs, p):
    wd = p["mla_w_down"][j]
    kpe_w = wd[:, MLA_Q_RANK + MLA_KV_RANK:]
    wd = jnp.concatenate([wd, _rot_cols(kpe_w),
                          jnp.zeros((D_MODEL, MLA_DOWN_PAD - wd.shape[1] - MLA_ROPE), F32)], axis=1).astype(BF16)
    wq = p["mla_w_uq"][j].reshape(MLA_Q_RANK, MLA_HEADS, MLA_QK)
    wq = jnp.concatenate([wq, _rot_cols(wq[..., MLA_NOPE:])], axis=-1).reshape(MLA_Q_RANK, -1).astype(BF16)
    dn = _mm(h, wd, rows=rows, tn=MLA_DOWN_PAD, pro="adaln", mod=mod, slot=3, out_dtype=F32, name="mla_down")
    q = _mm(dn, wq, rows=rows, tn=1024, pro="rms", nw=p["mla_q_norm"][j], out_dtype=BF16, name="mla_uq")
    kv = _mm(dn, p["mla_w_ukv"][j].astype(BF16), rows=rows, tn=1024, x_colblock=MLA_Q_RANK // MLA_KV_RANK,
             pro="rms", nw=p["mla_kv_norm"][j], out_dtype=BF16, name="mla_ukv")
    o = _attention(q, kv, dn, cs, nb=nb)
    return _mm(o, p["mla_w_o"][j].astype(BF16), rows=rows, tn=D_MODEL, epi="gres", mod=mod, gslot=5, res=h,
               name="mla_out")


def kernel(x, c, ctx, c_ctx, ada_w, ada_b, ffn_in, ffn_out, hy_in_w, hy_in_b, hy_conv_w, hy_conv_b, hy_pos_w1, hy_pos_b1, hy_freq, hy_pos_w2, hy_pos_b2, hy_pos_w3, hy_bias, hy_out_w, hy_out_b, mb_in_w, mb_conv_w, mb_conv_b, mb_dt_bias, mb_A_log, mb_D, mb_norm_w, mb_out_w, mla_w_down, mla_q_norm, mla_w_uq, mla_kv_norm, mla_w_ukv, mla_w_o, final_norm_w):
    p = dict(hy_in_w=hy_in_w, hy_in_b=hy_in_b, hy_conv_w=hy_conv_w, hy_conv_b=hy_conv_b, hy_pos_w1=hy_pos_w1,
             hy_pos_b1=hy_pos_b1, hy_freq=hy_freq, hy_pos_w2=hy_pos_w2, hy_pos_b2=hy_pos_b2, hy_pos_w3=hy_pos_w3,
             hy_bias=hy_bias, hy_out_w=hy_out_w, hy_out_b=hy_out_b, mb_in_w=mb_in_w, mb_conv_w=mb_conv_w,
             mb_conv_b=mb_conv_b, mb_dt_bias=mb_dt_bias, mb_A_log=mb_A_log, mb_D=mb_D, mb_norm_w=mb_norm_w,
             mb_out_w=mb_out_w, mla_w_down=mla_w_down, mla_q_norm=mla_q_norm, mla_w_uq=mla_w_uq,
             mla_kv_norm=mla_kv_norm, mla_w_ukv=mla_w_ukv, mla_w_o=mla_w_o)
    nb, n_lat, _ = x.shape
    assert n_lat == MOD_ROWS and nb * ctx.shape[1] == MOD_ROWS and ctx.shape[1] == CTX_LEN
    lat_rows = nb * n_lat
    all_rows = lat_rows + MOD_ROWS
    cs = _rope_table(n_lat)

    cc = jnp.concatenate([c, c_ctx[None], jnp.zeros((16 - nb - 1, D_MODEL), F32)], axis=0)
    mods = _mod_all(cc, ada_w, ada_b).reshape(DEPTH, 16, N_MOD, D_MODEL)
    w_ffn_in, w_ffn_out = ffn_in.astype(BF16), ffn_out.astype(BF16)
    h = None

    for i in range(DEPTH):
        kind, j, last = i % N_MIXERS, i // N_MIXERS, i == DEPTH - 1
        mod = mods[i]
        ctx_needed = not (last and kind == 0)
        ctx_out = not last
        rows = all_rows if ctx_needed else lat_rows
        if i == 0:
            h = _ffn(x.reshape(lat_rows, D_MODEL), mod, w_ffn_in, w_ffn_out, i, 0, rows=lat_rows, slot=0,
                     out_rows=rows)
            if ctx_needed:
                h = _ffn(ctx.reshape(MOD_ROWS, D_MODEL), mod, w_ffn_in, w_ffn_out, i, 0, rows=MOD_ROWS, slot=0,
                         out_row0=lat_rows, dest=h)
        else:
            h = _ffn(h, mod, w_ffn_in, w_ffn_out, i, 0, rows=rows, slot=0)
        if kind == 0:
            h = _hyena_layer(h, mod, rows, nb, j, ctx_out, p)
        elif kind == 1:
            h = _mamba_layer(h, mod, rows, nb, j, p)
        else:
            h = _mla_layer(h, mod, rows, nb, j, cs, p)
        rows = all_rows if ctx_out else lat_rows
        h = _ffn(h, mod, w_ffn_in, w_ffn_out, i, 1, rows=rows, slot=6, final_w=final_norm_w if last else None)
    return h[:lat_rows].reshape(nb, n_lat, D_MODEL)
```

```python
import functools
import math

import jax
import jax.numpy as jnp
from jax import lax
from jax.experimental import pallas as pl
from jax.experimental.pallas import tpu as pltpu

F32 = jnp.float32
BF16 = jnp.bfloat16
HIGHEST = lax.Precision.HIGHEST

D_MODEL = 1024
DEPTH = 4
GRID_W = 64
CTX_LEN = 256
N_MIXERS = 3
N_MOD = 9
FFN_DIM = 2816
EPS = 1e-6

HYENA_EMB = 33
HYENA_BANDS = (HYENA_EMB - 1) // 2
HYENA_FILTER_HIDDEN = 64
HYENA_FAST_DECAY = 0.3
HYENA_SLOW_DECAY = 1.5
HYENA_DECAY_TARGET = 1e-2

SSM_INNER = 2 * D_MODEL
SSM_HEAD_DIM = 64
SSM_HEADS = SSM_INNER // SSM_HEAD_DIM
SSM_GROUPS = 8
HEADS_PER_GROUP = SSM_HEADS // SSM_GROUPS
SSM_STATE = 128
SSM_CHUNK = 128
SSM_BC_DIM = SSM_GROUPS * SSM_STATE
SSM_CONV_DIM = SSM_INNER + 2 * SSM_BC_DIM
SSM_IN_DIM = SSM_INNER + SSM_CONV_DIM + 2 * SSM_HEADS
SSM_ZDT_PAD = SSM_INNER + 128

MLA_HEADS = 16
MLA_NOPE = 64
MLA_ROPE = 32
MLA_V = 64
MLA_Q_RANK = 768
MLA_KV_RANK = 256
MLA_QK = MLA_NOPE + MLA_ROPE
MLA_SCALE = MLA_QK ** -0.5
MLA_DOWN_PAD = 1152
ROPE_AXIS = MLA_ROPE // 2
ROPE_BASE = 10000.0

MOD_ROWS = 2048
VMEM_LIMIT_BYTES = 56 * 1024 * 1024


def _params(*sem):
    return pltpu.CompilerParams(dimension_semantics=sem, vmem_limit_bytes=VMEM_LIMIT_BYTES)


def _rms(x):
    return x * lax.rsqrt(jnp.mean(x * x, axis=-1, keepdims=True) + EPS)


def _silu(x):
    return x * jax.nn.sigmoid(x)


def _dot(a, b, **kw):
    return jnp.dot(a, b, preferred_element_type=F32, **kw)


def _dot_nt(a, b):
    return lax.dot_general(a, b, (((1,), (1,)), ((), ())), preferred_element_type=F32)


def _mod_kernel(x_ref, w_ref, b_ref, o_ref):
    a = _silu(x_ref[...]).astype(BF16)
    o_ref[...] = _dot(a, w_ref[...].astype(BF16)) + b_ref[...]


def _mod_all(cc, ada_w, ada_b):
    n = N_MOD * D_MODEL
    tn = 1024
    return pl.pallas_call(
        _mod_kernel,
        grid=(DEPTH, n // tn),
        in_specs=[pl.BlockSpec((16, D_MODEL), lambda l, j: (0, 0)),
                  pl.BlockSpec((None, D_MODEL, tn), lambda l, j: (l, 0, j)),
                  pl.BlockSpec((None, 1, tn), lambda l, j: (l, 0, j))],
        out_specs=pl.BlockSpec((None, 16, tn), lambda l, j: (l, 0, j)),
        out_shape=jax.ShapeDtypeStruct((DEPTH, 16, n), F32),
        compiler_params=_params("parallel", "parallel"),
        name="adaln_mod",
    )(cc, ada_w, ada_b.reshape(DEPTH, 1, n))


HALO = 16


def _mm_kernel(*refs, pro, epi, has_bias, slot, gslot, tile0, seq_len, has_dest):
    conv = epi in ("conv", "conv_silu")
    it = iter(refs)
    x_ref, w_ref = next(it), next(it)
    xp_ref, xn_ref = (next(it), next(it)) if conv else (None, None)
    mod_ref = next(it) if (pro == "adaln" or epi == "gres") else None
    ssd_refs = [next(it) for _ in range(5)] if pro == "ssd" else None
    b_ref = next(it) if has_bias else None
    res_ref = next(it) if epi == "gres" else None
    cw_ref, cb_ref = (next(it), next(it)) if conv else (None, None)
    if has_dest:
        next(it)
    o_ref, xa_ref = next(it), next(it)
    tm = x_ref.shape[0]

    def prologue(x):
        x = x.astype(F32)
        if pro == "adaln":
            x = _rms(x) * (1.0 + mod_ref[0, slot + 1:slot + 2, :]) + mod_ref[0, slot:slot + 1, :]
        elif pro == "ssd":
            yb_ref, xs_ref, z_ref, dsk_ref, gn_ref = ssd_refs
            y = (x + yb_ref[...].astype(F32) + dsk_ref[...] * xs_ref[...].astype(F32)) * _silu(z_ref[...])
            gw = SSM_INNER // SSM_GROUPS
            x = jnp.concatenate([_rms(y[:, g * gw:(g + 1) * gw]) * gn_ref[:, g * gw:(g + 1) * gw]
                                 for g in range(SSM_GROUPS)], axis=1)
        return x

    if conv:
        row0 = (tile0 + pl.program_id(0)) * tm
        keep_top = jnp.where((row0 & (seq_len - 1)) == 0, 0.0, 1.0)
        keep_bot = jnp.where(((row0 + tm) & (seq_len - 1)) == 0, 0.0, 1.0)

    @pl.when(pl.program_id(1) == 0)
    def _():
        if conv:
            xa_ref[0:HALO, :] = (prologue(xp_ref[...]) * keep_top).astype(BF16)
            xa_ref[HALO:HALO + tm, :] = prologue(x_ref[...]).astype(BF16)
            xa_ref[HALO + tm:2 * HALO + tm, :] = (prologue(xn_ref[...]) * keep_bot).astype(BF16)
        else:
            xa_ref[...] = prologue(x_ref[...]).astype(BF16)

    acc = _dot(xa_ref[...], w_ref[...])
    if conv:
        w0, w1, w2 = cw_ref[0:1, :], cw_ref[1:2, :], cw_ref[2:3, :]
        prev = pltpu.roll(acc, 1, 0)[HALO:HALO + tm]
        nxt = pltpu.roll(acc, tm + 2 * HALO - 1, 0)[HALO:HALO + tm]
        const = cb_ref[...] + (b_ref[...] * (w0 + w1 + w2) if has_bias else 0.0)
        y = w0 * prev + w1 * acc[HALO:HALO + tm] + w2 * nxt + const
        act = (lambda v: _silu(v.astype(o_ref.dtype))) if epi == "conv_silu" else (lambda v: v)
        o_ref[...] = act(y).astype(o_ref.dtype)
        if has_bias:
            o_ref[0:1, :] = act(y[0:1] - (1.0 - keep_top) * (b_ref[...] * w0)).astype(o_ref.dtype)
            o_ref[tm - 1:tm, :] = act(y[tm - 1:tm] - (1.0 - keep_bot) * (b_ref[...] * w2)).astype(o_ref.dtype)
        return
    if has_bias:
        acc = acc + b_ref[...]
    if epi == "gres":
        acc = res_ref[...] + mod_ref[0, gslot:gslot + 1, :] * acc
    o_ref[...] = acc.astype(o_ref.dtype)


def _mm(x, w, *, rows, tn, tm=1024, pro="none", epi="none", mod=None, slot=0, gslot=0,
        ssd=None, bias=None, res=None, conv_w=None, conv_b=None, seq_len=None, row0=0, out_rows=None,
        dest=None, out_dtype=F32, name="mm"):
    k, n = w.shape
    assert rows % tm == 0 and row0 % tm == 0 and n % tn == 0 and MOD_ROWS % tm == 0
    if epi == "gres":
        assert tn == n == D_MODEL
    conv = epi in ("conv", "conv_silu")
    per = MOD_ROWS // tm
    t0 = row0 // tm
    ins = [x, w]
    specs = [pl.BlockSpec((tm, k), lambda i, j: (t0 + i, 0)),
             pl.BlockSpec((k, tn), lambda i, j: (0, j))]
    if conv:
        assert seq_len % tm == 0 and (bias is None or out_dtype == F32)
        hb = tm // HALO
        last = (row0 + rows) // HALO - 1
        ins += [x, x]
        specs += [pl.BlockSpec((HALO, k), lambda i, j: (jnp.maximum((t0 + i) * hb - 1, 0), 0)),
                  pl.BlockSpec((HALO, k), lambda i, j: (jnp.minimum((t0 + i + 1) * hb, last), 0))]
    if pro == "adaln" or epi == "gres":
        ins.append(mod)
        specs.append(pl.BlockSpec((1, N_MOD, D_MODEL), lambda i, j: ((t0 + i) // per, 0, 0)))
    if pro == "ssd":
        yb, xs, z, d_skip, norm_w = ssd
        ins += [yb, xs, z, d_skip.reshape(1, k), norm_w.reshape(1, k)]
        specs += [pl.BlockSpec((tm, k), lambda i, j: (t0 + i, 0))] * 3 + [pl.BlockSpec((1, k), lambda i, j: (0, 0))] * 2
    if bias is not None:
        ins.append(bias.reshape(1, n))
        specs.append(pl.BlockSpec((1, tn), lambda i, j: (0, j)))
    if epi == "gres":
        ins.append(res)
        specs.append(pl.BlockSpec((tm, tn), lambda i, j: (t0 + i, j)))
    if conv:
        ins += [conv_w, conv_b.reshape(1, n)]
        specs += [pl.BlockSpec((3, tn), lambda i, j: (0, j)), pl.BlockSpec((1, tn), lambda i, j: (0, j))]
    aliases = {}
    out_shape = jax.ShapeDtypeStruct((out_rows or row0 + rows, n), out_dtype)
    if dest is not None:
        aliases = {len(ins): 0}
        ins.append(dest)
        specs.append(pl.BlockSpec(memory_space=pl.ANY))
        out_shape = jax.ShapeDtypeStruct(dest.shape, dest.dtype)
    return pl.pallas_call(
        functools.partial(_mm_kernel, pro=pro, epi=epi, has_bias=bias is not None, slot=slot, gslot=gslot,
                          tile0=t0, seq_len=seq_len, has_dest=dest is not None),
        grid=(rows // tm, n // tn),
        in_specs=specs,
        out_specs=pl.BlockSpec((tm, tn), lambda i, j: (t0 + i, j)),
        out_shape=out_shape,
        scratch_shapes=[pltpu.VMEM((tm + 2 * HALO if conv else tm, k), BF16)],
        input_output_aliases=aliases,
        compiler_params=_params("parallel", "arbitrary"),
        name=name,
    )(*ins)


def _ffn_kernel(*refs, slot, final, fc, has_dest):
    if has_dest:
        refs = refs[:-2] + refs[-1:]
    if final:
        x_ref, mod_ref, wi_ref, wo_ref, fw_ref, o_ref = refs
    else:
        x_ref, mod_ref, wi_ref, wo_ref, o_ref = refs
    x = x_ref[...]
    a = (_rms(x) * (1.0 + mod_ref[0, slot + 1:slot + 2, :]) + mod_ref[0, slot:slot + 1, :]).astype(BF16)
    acc = None
    for c in range(0, FFN_DIM, fc):
        g = _dot(a, wi_ref[:, c:c + fc])
        u = _dot(a, wi_ref[:, FFN_DIM + c:FFN_DIM + c + fc])
        t = _dot((_silu(g) * u).astype(BF16), wo_ref[c:c + fc, :])
        acc = t if acc is None else acc + t
    out = x + (0.5 * mod_ref[0, slot + 2:slot + 3, :]) * acc
    if final:
        out = _rms(out) * fw_ref[...]
    o_ref[...] = out


def _ffn(h, mod, w_in, w_out, layer, k, *, rows, slot, final_w=None, out_row0=0, out_rows=None, dest=None,
         tm=512, fc=FFN_DIM):
    per = MOD_ROWS // tm
    o0 = out_row0 // tm
    final = final_w is not None
    resident = pl.Buffered(1)
    ins = [h, mod, w_in, w_out]
    specs = [pl.BlockSpec((tm, D_MODEL), lambda i: (i, 0)),
             pl.BlockSpec((1, N_MOD, D_MODEL), lambda i: ((o0 + i) // per, 0, 0)),
             pl.BlockSpec((None, None, D_MODEL, 2 * FFN_DIM), lambda i: (layer, k, 0, 0), pipeline_mode=resident),
             pl.BlockSpec((None, None, FFN_DIM, D_MODEL), lambda i: (layer, k, 0, 0), pipeline_mode=resident)]
    if final:
        ins.append(final_w.reshape(1, D_MODEL))
        specs.append(pl.BlockSpec((1, D_MODEL), lambda i: (0, 0)))
    aliases = {}
    out_shape = jax.ShapeDtypeStruct((out_rows or rows, D_MODEL), F32)
    if dest is not None:
        aliases = {len(ins): 0}
        ins.append(dest)
        specs.append(pl.BlockSpec(memory_space=pl.ANY))
        out_shape = jax.ShapeDtypeStruct(dest.shape, dest.dtype)
    return pl.pallas_call(
        functools.partial(_ffn_kernel, slot=slot, final=final, fc=fc, has_dest=dest is not None),
        grid=(rows // tm,),
        in_specs=specs,
        out_specs=pl.BlockSpec((tm, D_MODEL), lambda i: (o0 + i, 0)),
        out_shape=out_shape,
        input_output_aliases=aliases,
        compiler_params=_params("parallel"),
        name="ffn",
    )(*ins)


FILTER_W = 128


def _filter_kernel(za_ref, zb_ref, w1_ref, b1_ref, fr_ref, w2_ref, b2_ref, w3f_ref, w3b_ref, dl_ref,
                   sum_ref, diff_ref):
    fr = fr_ref[...]

    def taps(z_ref, w3_ref):
        z = z_ref[...]
        hid = jnp.sin(fr * (_dot(z, w1_ref[...], precision=HIGHEST) + b1_ref[...]))
        hid = jnp.sin(fr * (_dot(hid, w2_ref[...], precision=HIGHEST) + b2_ref[...]))
        window = jnp.exp(-z[:, 0:1] * dl_ref[...]) * z[:, FILTER_W - 1:FILTER_W]
        return _dot(hid.astype(BF16), w3_ref[...].astype(BF16)) * window

    hf = taps(za_ref, w3f_ref)
    hb = taps(zb_ref, w3b_ref)
    sum_ref[...] = (hf + hb).astype(BF16)
    diff_ref[...] = (hb - hf).astype(BF16)


def _hyena_filters(L, poly, w1, b1, freq, w2, b2, w3):
    S = L // poly
    nrho = 2 * poly - 1
    jj = jnp.arange(S, dtype=jnp.int32)
    rho = jnp.arange(-(poly - 1), poly, dtype=jnp.int32)
    t_tab = jnp.linspace(0.0, 1.0, L, dtype=F32)
    bands = jnp.linspace(1e-4, HYENA_BANDS - 1, HYENA_BANDS, dtype=F32)

    def feats(pos, lowest):
        valid = (pos >= lowest).astype(F32)
        pc = jnp.maximum(pos, 0)
        ang = (2.0 * math.pi / L) * pc.astype(F32)[..., None] * bands
        z = jnp.concatenate([jnp.take(t_tab, pc)[..., None], jnp.cos(ang), -jnp.sin(ang)], axis=-1)
        z = jnp.pad(z, ((0, 0), (0, 0), (0, FILTER_W - 1 - HYENA_EMB)))
        return jnp.concatenate([z, valid[..., None]], axis=-1).reshape(nrho * S, FILTER_W)

    za = feats(poly * jj[None, :] + rho[:, None], 0)
    zb = feats(poly * jj[None, :] - rho[:, None], 1)
    ph = FILTER_W - HYENA_FILTER_HIDDEN
    w1 = jnp.pad(w1, ((0, FILTER_W - HYENA_EMB), (0, ph)))
    w2 = jnp.pad(w2, ((0, ph), (0, ph)))
    w3 = jnp.pad(w3, ((0, ph), (0, 0))).reshape(FILTER_W, 2, 2, D_MODEL)
    w3f = w3[:, :, 0].reshape(FILTER_W, 2 * D_MODEL)
    w3b = w3[:, :, 1].reshape(FILTER_W, 2 * D_MODEL)
    b1, b2, freq = (jnp.pad(t, (0, ph)).reshape(1, FILTER_W) for t in (b1, b2, freq))
    max_decay = math.log(HYENA_DECAY_TARGET) / HYENA_FAST_DECAY
    min_decay = math.log(HYENA_DECAY_TARGET) / HYENA_SLOW_DECAY
    deltas = jnp.abs(jnp.linspace(min_decay, max_decay, D_MODEL, dtype=F32))
    deltas = jnp.tile(deltas, 2).reshape(1, 2 * D_MODEL)
    full = lambda shape: pl.BlockSpec(shape, lambda i: (0, 0))
    sq = full((FILTER_W, FILTER_W))
    vec = full((1, FILTER_W))
    return pl.pallas_call(
        _filter_kernel,
        grid=(nrho,),
        in_specs=[pl.BlockSpec((S, FILTER_W), lambda i: (i, 0)), pl.BlockSpec((S, FILTER_W), lambda i: (i, 0)),
                  sq, vec, vec, sq, vec, full((FILTER_W, 2 * D_MODEL)), full((FILTER_W, 2 * D_MODEL)),
                  full((1, 2 * D_MODEL))],
        out_specs=[pl.BlockSpec((S, 2 * D_MODEL), lambda i: (i, 0))] * 2,
        out_shape=[jax.ShapeDtypeStruct((nrho * S, 2 * D_MODEL), BF16)] * 2,
        compiler_params=_params("parallel"),
        name="hyena_filter",
    )(za, zb, w1, b1, freq, w2, b2, w3f, w3b, deltas)


def _lmm_kernel(*refs, epi, tf, scale, has_dest):
    it = iter(refs)
    a_ref, x_ref = next(it), next(it)
    if epi == "spec":
        k_ref = next(it)
    elif epi == "gate":
        g_ref, v_ref, bias_ref = next(it), next(it), next(it)
    if has_dest:
        next(it)
    o_ref, xb_ref = next(it), next(it)

    @pl.when(pl.program_id(2) == 0)
    def _():
        xb_ref[...] = x_ref[...].astype(BF16)

    acc = _dot(a_ref[...], xb_ref[...])
    if epi == "spec":
        xr, xs = acc[:tf], acc[tf:]
        kr, ki = k_ref[0:tf, :], k_ref[tf:2 * tf, :]
        o_ref[0:tf, :] = ((xr * kr + xs * ki) * scale).astype(o_ref.dtype)
        o_ref[tf:2 * tf, :] = ((xs * kr - xr * ki) * scale).astype(o_ref.dtype)
    elif epi == "gate":
        o_ref[...] = (g_ref[...] * (acc + v_ref[...].astype(F32) * bias_ref[...])).astype(o_ref.dtype)
    else:
        o_ref[...] = acc.astype(o_ref.dtype)


def _lmm(a, x, *, nb, ncb, tm, n, x_row0=0, x_cb0=0, epi="none", tf=0, scale=1.0, kspec=None, k_cb=0,
         g=None, g_cb=0, g_row0=0, v=None, v_cb=0, v_row0=0, bias=None, out_dtype=F32, dest=None, dest_row0=0,
         name="lmm"):
    mo, k = a.shape
    nm = mo // tm
    xr0 = x_row0 // k
    or0 = dest_row0 // tm
    ins = [a, x]
    specs = [pl.BlockSpec((tm, k), lambda b, c, m: (m, 0)),
             pl.BlockSpec((k, n), lambda b, c, m: (xr0 + b, x_cb0 + c))]
    if epi == "spec":
        ins.append(kspec)
        specs.append(pl.BlockSpec((tm, n), lambda b, c, m: (m, k_cb)))
    elif epi == "gate":
        gr0, vr0 = g_row0 // tm, v_row0 // tm
        ins += [g, v, bias.reshape(1, n)]
        specs += [pl.BlockSpec((tm, n), lambda b, c, m: (gr0 + b * nm + m, g_cb)),
                  pl.BlockSpec((tm, n), lambda b, c, m: (vr0 + b * nm + m, v_cb)),
                  pl.BlockSpec((1, n), lambda b, c, m: (0, 0))]
    aliases = {}
    out_shape = jax.ShapeDtypeStruct((nb * mo, ncb * n), out_dtype)
    if dest is not None:
        aliases = {len(ins): 0}
        ins.append(dest)
        specs.append(pl.BlockSpec(memory_space=pl.ANY))
        out_shape = jax.ShapeDtypeStruct(dest.shape, dest.dtype)
    return pl.pallas_call(
        functools.partial(_lmm_kernel, epi=epi, tf=tf, scale=scale, has_dest=dest is not None),
        grid=(nb, ncb, nm),
        in_specs=specs,
        out_specs=pl.BlockSpec((tm, n), lambda b, c, m: (or0 + b * nm + m, c)),
        out_shape=out_shape,
        scratch_shapes=[pltpu.VMEM((k, n), BF16)],
        input_output_aliases=aliases,
        compiler_params=_params("parallel", "parallel", "arbitrary"),
        name=name,
    )(*ins)


def _dft_mats(L, tf):
    idx = jnp.arange(L, dtype=jnp.int32)
    ph = ((2 * idx[:, None] + 1) * idx[None, :]) % (4 * L)
    ang = ph.astype(F32) * (2.0 * math.pi / (4 * L))
    c, s = jnp.cos(ang), jnp.sin(ang)
    nf = L // tf
    fwd = jnp.stack([c.reshape(nf, tf, L), s.reshape(nf, tf, L)], axis=1).reshape(2 * L, L)
    inv = jnp.stack([c.T.reshape(L, nf, tf), s.T.reshape(L, nf, tf)], axis=2).reshape(L, 2 * L)
    return c.astype(BF16), s.astype(BF16), fwd.astype(BF16), inv.astype(BF16)


def _interleave(kr, ki, tf):
    L, n = kr.shape
    return jnp.stack([kr.reshape(L // tf, tf, n), ki.reshape(L // tf, tf, n)], axis=1).reshape(2 * L, n)


def _hyena_core(u, L, nb, row0, fparams, hy_bias, name, dest=None):
    tf = min(L, 256)
    tmi = min(L, 256)
    cmat, smat, fwd, inv = _dft_mats(L, tf)
    hsum, hdiff = _hyena_filters(L, 1, *fparams)
    kr = _lmm(cmat, hsum, nb=1, ncb=2, tm=tf, n=D_MODEL, name=name + "_kr")
    ki = _lmm(smat, hdiff, nb=1, ncb=2, tm=tf, n=D_MODEL, name=name + "_ki")
    kspec = _interleave(kr, ki, tf)
    scale = 1.0 / L
    y1 = _lmm(fwd, u, nb=nb, ncb=1, tm=2 * tf, n=D_MODEL, x_row0=row0, x_cb0=2, epi="spec", tf=tf, scale=scale,
              kspec=kspec, k_cb=0, out_dtype=BF16, name=name + "_fwd1")
    z = _lmm(inv, y1, nb=nb, ncb=1, tm=tmi, n=D_MODEL, epi="gate", g=u, g_cb=0, g_row0=row0, v=u, v_cb=2,
             v_row0=row0, bias=hy_bias[0], out_dtype=F32, name=name + "_inv1")
    y2 = _lmm(fwd, z, nb=nb, ncb=1, tm=2 * tf, n=D_MODEL, epi="spec", tf=tf, scale=scale,
              kspec=kspec, k_cb=1, out_dtype=BF16, name=name + "_fwd2")
    return _lmm(inv, y2, nb=nb, ncb=1, tm=tmi, n=D_MODEL, epi="gate", g=u, g_cb=1, g_row0=row0, v=z, v_cb=0,
                bias=hy_bias[1], out_dtype=F32, dest=dest, dest_row0=row0, name=name + "_inv2")


POLY = 4
POLY_SEQS = 2


def _poly_conv_kernel(a_ref, ai_ref, u_ref, kr_ref, ki_ref, g_ref, bias_ref, o_ref, *, scale):
    S = a_ref.shape[1]
    L = S * POLY
    tc = u_ref.shape[1]
    for row0 in range(0, u_ref.shape[0], L):
        phases = [u_ref[pl.ds(row0 + r, S, stride=POLY), :] for r in range(POLY)]
        e = _dot(a_ref[...], jnp.concatenate([ph.astype(BF16) for ph in phases], axis=1))
        f = []
        for r in range(POLY):
            fre = fim = None
            for rp in range(POLY):
                kr = kr_ref[r - rp + POLY - 1]
                ki = ki_ref[r - rp + POLY - 1]
                er = e[0:S, rp * tc:(rp + 1) * tc]
                es = e[S:2 * S, rp * tc:(rp + 1) * tc]
                tre = kr * er + ki * es
                tim = kr * es - ki * er
                fre = tre if fre is None else fre + tre
                fim = tim if fim is None else fim + tim
            f.append(jnp.concatenate([fre, fim], axis=0).astype(BF16))
        y = _dot(ai_ref[...], jnp.concatenate(f, axis=1)) * scale
        for r in range(POLY):
            rows = pl.ds(row0 + r, S, stride=POLY)
            o_ref[rows, :] = g_ref[rows, :] * (y[:, r * tc:(r + 1) * tc] + phases[r] * bias_ref[...])


def _poly_conv(fwd, inv, x, x_cb, g, g_cb, kre, kim, order, bias, *, nb, out_rows=None, tc=128, name="hy_conv"):
    L = MOD_ROWS
    S = L // POLY
    nct = D_MODEL // tc
    nrho = 2 * POLY - 1
    assert nb % POLY_SEQS == 0
    rb = POLY_SEQS * L
    return pl.pallas_call(
        functools.partial(_poly_conv_kernel, scale=1.0 / S),
        grid=(nct, nb // POLY_SEQS),
        in_specs=[pl.BlockSpec((2 * S, S), lambda c, b: (0, 0)),
                  pl.BlockSpec((S, 2 * S), lambda c, b: (0, 0)),
                  pl.BlockSpec((rb, tc), lambda c, b: (b, x_cb * nct + c)),
                  pl.BlockSpec((nrho, S, tc), lambda c, b: (0, 0, order * nct + c)),
                  pl.BlockSpec((nrho, S, tc), lambda c, b: (0, 0, order * nct + c)),
                  pl.BlockSpec((rb, tc), lambda c, b: (b, g_cb * nct + c)),
                  pl.BlockSpec((1, tc), lambda c, b: (0, c))],
        out_specs=pl.BlockSpec((rb, tc), lambda c, b: (b, c)),
        out_shape=jax.ShapeDtypeStruct((out_rows or nb * L, D_MODEL), F32),
        compiler_params=_params("parallel", "parallel"),
        name=name,
    )(fwd, inv, x, kre, kim, g, bias.reshape(1, D_MODEL))


def _hyena_core_poly(u, nb, fparams, hy_bias, out_rows):
    S = MOD_ROWS // POLY
    nrho = 2 * POLY - 1
    cmat, smat, fwd, inv = _dft_mats(S, S)
    ksum, kdiff = _hyena_filters(MOD_ROWS, POLY, *fparams)
    kre = _lmm(cmat, ksum, nb=nrho, ncb=2, tm=S, n=D_MODEL, name="hy_lat_kr").reshape(nrho, S, 2 * D_MODEL)
    kim = _lmm(smat, kdiff, nb=nrho, ncb=2, tm=S, n=D_MODEL, name="hy_lat_ki").reshape(nrho, S, 2 * D_MODEL)
    z = _poly_conv(fwd, inv, u, 2, u, 0, kre, kim, 0, hy_bias[0], nb=nb, name="hy_lat_conv1")
    return _poly_conv(fwd, inv, z, 0, u, 1, kre, kim, 1, hy_bias[1], nb=nb, out_rows=out_rows, name="hy_lat_conv2")


def _softplus(x):
    return jnp.maximum(x, 0.0) + jnp.log(1.0 + jnp.exp(-jnp.abs(x)))


def _ssd_kernel(xs_f, b_f, c_f, dt_f, dtt_f, xs_b, b_b, c_b, dt_b, dtt_b, dtb_ref, dtbt_ref, al_ref, alt_ref,
                yf_ref, yb_ref, st_ref):
    @pl.when(pl.program_id(1) == 0)
    def _():
        st_ref[...] = jnp.zeros_like(st_ref)

    _ssd_chunk(0, xs_f, b_f, c_f, dt_f, dtt_f, dtb_ref[0], dtbt_ref[0], al_ref[0], alt_ref[0], yf_ref, st_ref.at[0])
    _ssd_chunk(1, xs_b, b_b, c_b, dt_b, dtt_b, dtb_ref[1], dtbt_ref[1], al_ref[1], alt_ref[1], yb_ref, st_ref.at[1])


def _ssd_chunk(d, xs_ref, b_ref, c_ref, dt_ref, dtt_ref, dt_bias, dt_bias_t, a_log, a_log_t, y_ref, st_ref):
    T = SSM_CHUNK
    N = SSM_STATE
    ri = lax.broadcasted_iota(jnp.int32, (T, T), 0)
    ci = lax.broadcasted_iota(jnp.int32, (T, T), 1)
    mask = (ri >= ci) if d == 0 else (ri <= ci)
    tri_col = mask.astype(F32)
    tri_row = ((ci >= ri) if d == 0 else (ci <= ri)).astype(F32)
    low = lax.broadcasted_iota(jnp.int32, (T, 2 * SSM_HEAD_DIM), 1) < SSM_HEAD_DIM

    a = _softplus(dt_ref[...] + dt_bias) * -jnp.exp(a_log)
    dtt = _softplus(dtt_ref[...] + dt_bias_t)
    at = dtt * -jnp.exp(a_log_t)
    cs_col = _dot(tri_col, a, precision=HIGHEST)
    cs_row = _dot(at, tri_row, precision=HIGHEST)
    tot = jnp.sum(at, axis=1, keepdims=True)
    row_dt = cs_row - jnp.log(dtt)
    w_out = dtt * jnp.exp(tot - cs_row)
    e_tot = jnp.exp(tot)

    for g in range(SSM_GROUPS):
        bt = jnp.transpose(b_ref[:, g * N:(g + 1) * N].astype(F32))
        cg = c_ref[:, g * N:(g + 1) * N].astype(F32)
        cb = _dot(cg.astype(BF16), bt.astype(BF16))
        for k in range(HEADS_PER_GROUP // 2):
            h0 = g * HEADS_PER_GROUP + 2 * k
            lanes = slice(h0 * SSM_HEAD_DIM, (h0 + 2) * SSM_HEAD_DIM)
            xs = xs_ref[:, lanes].astype(BF16)
            st = st_ref[g, :, k * 128:(k + 1) * 128]
            rhs = jnp.concatenate([xs, st.astype(BF16)], axis=0)
            ys, upds = [], []
            for h in (h0, h0 + 1):
                col = jnp.broadcast_to(cs_col[:, h:h + 1], (T, T))
                m = jnp.where(mask, jnp.exp(col - row_dt[h:h + 1, :]), 0.0) * cb
                lhs = jnp.concatenate([m, jnp.exp(col) * cg], axis=1).astype(BF16)
                ys.append(_dot(lhs, rhs))
                upds.append(_dot((bt * w_out[h:h + 1, :]).astype(BF16), xs))
            y_ref[:, lanes] = jnp.where(low, ys[0], ys[1]).astype(y_ref.dtype)
            decay = jnp.where(low, e_tot[h0:h0 + 1, :], e_tot[h0 + 1:h0 + 2, :])
            st_ref[g, :, k * 128:(k + 1) * 128] = st * decay + jnp.where(low, upds[0], upds[1])


def _ssd(xbc, dta, dtt, dt_bias, a_log, *, nb):
    m = xbc.shape[0]
    T = SSM_CHUNK
    H = SSM_HEADS
    lat_c = MOD_ROWS // T
    ctx_c = CTX_LEN // T
    ctx0 = nb * lat_c
    steps = lat_c + ctx_c

    def rb(b, d, t):
        ctx = ctx0 + ctx_c * b + jnp.where(d == 0, t, ctx_c - 1 - t)
        lat = lat_c * b + jnp.where(d == 0, t - ctx_c, steps - 1 - t)
        return jnp.where(t < ctx_c, ctx, lat)

    def direction(d):
        return [pl.BlockSpec((T, SSM_INNER), lambda b, t: (rb(b, d, t), 0)),
                pl.BlockSpec((T, SSM_BC_DIM), lambda b, t: (rb(b, d, t), 2)),
                pl.BlockSpec((T, SSM_BC_DIM), lambda b, t: (rb(b, d, t), 3)),
                pl.BlockSpec((None, T, H), lambda b, t: (d, rb(b, d, t), 0)),
                pl.BlockSpec((None, H, T), lambda b, t: (d, 0, rb(b, d, t)))]

    whole = lambda shape: pl.BlockSpec(shape, lambda b, t: (0, 0, 0))
    return pl.pallas_call(
        _ssd_kernel,
        grid=(nb, steps),
        in_specs=direction(0) + direction(1) + [whole((2, 1, H)), whole((2, H, 1)), whole((2, 1, H)), whole((2, H, 1))],
        out_specs=[pl.BlockSpec((T, SSM_INNER), lambda b, t: (rb(b, 0, t), 0)),
                   pl.BlockSpec((T, SSM_INNER), lambda b, t: (rb(b, 1, t), 0))],
        out_shape=[jax.ShapeDtypeStruct((m, SSM_INNER), BF16)] * 2,
        scratch_shapes=[pltpu.VMEM((2, SSM_GROUPS, SSM_STATE, HEADS_PER_GROUP * SSM_HEAD_DIM), F32)],
        compiler_params=_params("parallel", "arbitrary"),
        name="ssd_scan",
    )(xbc, xbc, xbc, dta, dtt, xbc, xbc, xbc, dta, dtt, dt_bias.reshape(2, 1, H), dt_bias.reshape(2, H, 1),
      a_log.reshape(2, 1, H), a_log.reshape(2, H, 1))


HEAD_W = 2 * MLA_NOPE
ATTN_SUB_ROWS = 256


def _attn_kernel(*refs, has_lat):
    if has_lat:
        q_ref, kvc_ref, krc_ref, kv_ref, kr_ref, tq_ref, tk_ref, o_ref, kc_scr, vc_scr, k_scr, v_scr = refs
    else:
        q_ref, kvc_ref, krc_ref, _, o_ref, kc_scr, vc_scr = refs
    R = MLA_ROPE
    kscale = MLA_SCALE * math.log2(math.e)

    def build_keys():
        krc = krc_ref[...].astype(F32)
        lane_c = lax.broadcasted_iota(jnp.int32, krc.shape, 1)
        pe_c = jnp.where((lane_c >= MLA_NOPE) & (lane_c < MLA_NOPE + R), pltpu.roll(krc, MLA_NOPE, 1), 0.0)
        for hh in range(2):
            kvh = kvc_ref[:, hh * HEAD_W:(hh + 1) * HEAD_W].astype(F32)
            kc_scr[hh] = (jnp.where(lane_c < MLA_NOPE, kvh, pe_c) * kscale).astype(BF16)
            vc_scr[hh] = jnp.where(lane_c < MLA_NOPE, 1.0, kvh).astype(BF16)
        if has_lat:
            t = kr_ref[...].astype(F32) * tk_ref[...]
            lane = lax.broadcasted_iota(jnp.int32, t.shape, 1)
            krot = jnp.where(lane < R, t + pltpu.roll(t, HEAD_W - R, 1), 0.0)
            krr = pltpu.roll(krot, MLA_NOPE, 1) + pltpu.roll(krot, MLA_NOPE + R, 1)
            for hh in range(2):
                kvh = kv_ref[:, hh * HEAD_W:(hh + 1) * HEAD_W].astype(F32)
                k_scr[hh] = (jnp.where(lane < MLA_NOPE, kvh, krr) * kscale).astype(BF16)
                v_scr[hh] = jnp.where(lane < MLA_NOPE, 1.0, kvh).astype(BF16)

    if has_lat:
        pl.when(pl.program_id(2) == 0)(build_keys)
    else:
        build_keys()

    sub = min(q_ref.shape[0], ATTN_SUB_ROWS)
    for r0 in range(0, q_ref.shape[0], sub):
        rows = slice(r0, r0 + sub)
        res = []
        for hh in range(2):
            q = q_ref[rows, hh * HEAD_W:(hh + 1) * HEAD_W]
            s_c = _dot_nt(q, kc_scr[hh])
            mx = jnp.max(s_c, axis=-1, keepdims=True)
            if has_lat:
                ql = (q.astype(F32) * tq_ref[rows, :]).astype(BF16)
                s_l = _dot_nt(ql, k_scr[hh])
                mx = jnp.maximum(mx, jnp.max(s_l, axis=-1, keepdims=True))
                acc = _dot(jnp.exp2(s_l - mx).astype(BF16), v_scr[hh])
                acc = acc + _dot(jnp.exp2(s_c - mx).astype(BF16), vc_scr[hh])
            else:
                acc = _dot(jnp.exp2(s_c - mx).astype(BF16), vc_scr[hh])
            res.append(acc / pltpu.roll(acc, MLA_V, 1))
        lane_o = lax.broadcasted_iota(jnp.int32, res[0].shape, 1)
        o_ref[rows, :] = jnp.where(lane_o < MLA_V, pltpu.roll(res[0], MLA_V, 1), res[1]).astype(o_ref.dtype)


def _attention(q, kv, dn, cs, *, nb, tq=1024):
    L = MOD_ROWS
    hp = MLA_HEADS // 2
    w = 2 * HEAD_W
    nq = L // tq
    cb = nb * (L // CTX_LEN)
    kr_cb = 0
    tab_q = jnp.concatenate([jnp.ones((L, MLA_NOPE), F32), cs], axis=1)
    tab_k = jnp.concatenate([cs, jnp.zeros((L, HEAD_W - 2 * MLA_ROPE), F32)], axis=1)
    lat = pl.pallas_call(
        functools.partial(_attn_kernel, has_lat=True),
        grid=(nb, hp, nq),
        in_specs=[pl.BlockSpec((tq, w), lambda b, p, i: (b * nq + i, p)),
                  pl.BlockSpec((CTX_LEN, w), lambda b, p, i: (cb + b, p)),
                  pl.BlockSpec((CTX_LEN, HEAD_W), lambda b, p, i: (cb + b, kr_cb)),
                  pl.BlockSpec((L, w), lambda b, p, i: (b, p)),
                  pl.BlockSpec((L, HEAD_W), lambda b, p, i: (b, kr_cb)),
                  pl.BlockSpec((tq, HEAD_W), lambda b, p, i: (i, 0)),
                  pl.BlockSpec((L, HEAD_W), lambda b, p, i: (0, 0))],
        out_specs=pl.BlockSpec((tq, 2 * MLA_V), lambda b, p, i: (b * nq + i, p)),
        out_shape=jax.ShapeDtypeStruct((nb * (L + CTX_LEN), MLA_HEADS * MLA_V), BF16),
        scratch_shapes=[pltpu.VMEM((2, CTX_LEN, HEAD_W), BF16)] * 2 + [pltpu.VMEM((2, L, HEAD_W), BF16)] * 2,
        compiler_params=_params("parallel", "parallel", "arbitrary"),
        name="mla_attn",
    )(q, kv, dn, kv, dn, tab_q, tab_k)
    return pl.pallas_call(
        functools.partial(_attn_kernel, has_lat=False),
        grid=(nb, hp),
        in_specs=[pl.BlockSpec((CTX_LEN, w), lambda b, p: (cb + b, p)),
                  pl.BlockSpec((CTX_LEN, w), lambda b, p: (cb + b, p)),
                  pl.BlockSpec((CTX_LEN, HEAD_W), lambda b, p: (cb + b, kr_cb)),
                  pl.BlockSpec(memory_space=pl.ANY)],
        out_specs=pl.BlockSpec((CTX_LEN, 2 * MLA_V), lambda b, p: (cb + b, p)),
        out_shape=jax.ShapeDtypeStruct(lat.shape, lat.dtype),
        scratch_shapes=[pltpu.VMEM((2, CTX_LEN, HEAD_W), BF16)] * 2,
        input_output_aliases={3: 0},
        compiler_params=_params("parallel", "parallel"),
        name="mla_attn_ctx",
    )(q, kv, dn, lat)


def _rot_cols(w):
    wp = w.reshape(w.shape[:-1] + (2, 2, ROPE_AXIS // 2))
    return jnp.stack([-wp[..., 1, :], wp[..., 0, :]], axis=-2).reshape(w.shape)


def _rope_table(n_lat):
    rows = n_lat // GRID_W
    row = jnp.repeat(jnp.arange(rows), GRID_W)
    col = jnp.tile(jnp.arange(GRID_W), rows)
    inv = 1.0 / (ROPE_BASE ** (jnp.arange(0, ROPE_AXIS, 2, dtype=F32) / ROPE_AXIS))
    ar = row.astype(F32)[:, None] * inv[None, :]
    ac = col.astype(F32)[:, None] * inv[None, :]
    ang = jnp.concatenate([ar, ar, ac, ac], axis=-1)
    return jnp.concatenate([jnp.cos(ang), jnp.sin(ang)], axis=-1)


def _conv_proj(h, w, bias, conv_w, conv_b, mod, rows, nb, epi, out_dtype, name):
    lat_rows = nb * MOD_ROWS
    common = dict(pro="adaln", mod=mod, slot=3, bias=bias, epi=epi, conv_w=conv_w, conv_b=conv_b, out_dtype=out_dtype)
    out = _mm(h, w, rows=lat_rows, tn=1024, seq_len=MOD_ROWS, out_rows=rows, name=name, **common)
    if rows > lat_rows:
        out = _mm(h, w, rows=rows - lat_rows, tn=w.shape[1], tm=CTX_LEN, seq_len=CTX_LEN, row0=lat_rows, dest=out,
                  name=name + "_ctx", **common)
    return out


def _hyena_layer(h, mod, rows, nb, j, with_ctx, p):
    u = _conv_proj(h, p["hy_in_w"][j].astype(BF16), p["hy_in_b"][j], p["hy_conv_w"][j], p["hy_conv_b"][j], mod,
                   rows, nb, "conv", F32, "hyena_in")
    fparams = (p["hy_pos_w1"][j], p["hy_pos_b1"][j], p["hy_freq"][j], p["hy_pos_w2"][j], p["hy_pos_b2"][j],
               p["hy_pos_w3"][j])
    y = _hyena_core_poly(u, nb, fparams, p["hy_bias"][j], rows)
    if with_ctx:
        y = _hyena_core(u, CTX_LEN, nb, nb * MOD_ROWS, fparams, p["hy_bias"][j], "hy_ctx", dest=y)
    return _mm(y, p["hy_out_w"][j].astype(BF16), rows=rows, tn=D_MODEL, epi="gres", mod=mod, gslot=5,
               bias=p["hy_out_b"][j], res=h, name="hyena_out")


def _mamba_layer(h, mod, rows, nb, j, p):
    w_in = p["mb_in_w"][j].astype(BF16)
    w_zdt = jnp.concatenate([w_in[:, :SSM_INNER], w_in[:, SSM_INNER + SSM_CONV_DIM:],
                             jnp.zeros((D_MODEL, SSM_ZDT_PAD - SSM_INNER - 2 * SSM_HEADS), BF16)], axis=1)
    zx = _mm(h, w_zdt, rows=rows, tn=SSM_ZDT_PAD, pro="adaln", mod=mod, slot=3, name="mamba_in_zdt")
    xbc = _conv_proj(h, w_in[:, SSM_INNER:SSM_INNER + SSM_CONV_DIM], None, p["mb_conv_w"][j], p["mb_conv_b"][j], mod,
                     rows, nb, "conv_silu", BF16, "mamba_in_xbc")
    dtr = zx[:, SSM_INNER:SSM_INNER + 2 * SSM_HEADS].reshape(rows, 2, SSM_HEADS)
    dta = jnp.transpose(dtr, (1, 0, 2))
    dtt = jnp.transpose(dtr, (1, 2, 0))
    yf, yb = _ssd(xbc, dta, dtt, p["mb_dt_bias"][j], p["mb_A_log"][j], nb=nb)
    d_skip = jnp.repeat(p["mb_D"][j], SSM_HEAD_DIM)
    return _mm(yf, p["mb_out_w"][j].astype(BF16), rows=rows, tn=D_MODEL, tm=512, pro="ssd",
               ssd=(yb, xbc, zx, d_skip, p["mb_norm_w"][j]), epi="gres", mod=mod, gslot=5, res=h, name="mamba_out")


def _mla_proj_kernel(x_ref, mod_ref, wd_ref, qn_ref, kvn_ref, wq_ref, wkv_ref, q_ref, kv_ref, kr_ref):
    a = _rms(x_ref[...]) * (1.0 + mod_ref[0, 4:5, :]) + mod_ref[0, 3:4, :]
    dn = _dot(a.astype(BF16), wd_ref[...])
    cq = _rms(dn[:, :MLA_Q_RANK]) * qn_ref[...]
    ckv = _rms(dn[:, MLA_Q_RANK:MLA_Q_RANK + MLA_KV_RANK]) * kvn_ref[...]
    q_ref[...] = _dot(cq.astype(BF16), wq_ref[...]).astype(q_ref.dtype)
    kv_ref[...] = _dot(ckv.astype(BF16), wkv_ref[...]).astype(kv_ref.dtype)
    kr_ref[...] = dn[:, MLA_Q_RANK + MLA_KV_RANK:]


def _mla_proj(h, mod, wd, q_norm, kv_norm, wq, wkv, *, rows, tm=512):
    per = MOD_ROWS // tm
    nq, nkv = wq.shape[1], wkv.shape[1]
    nr = MLA_DOWN_PAD - MLA_Q_RANK - MLA_KV_RANK
    whole = lambda a: pl.BlockSpec(a.shape, lambda i: (0, 0))
    qn, kvn = q_norm.reshape(1, MLA_Q_RANK), kv_norm.reshape(1, MLA_KV_RANK)
    return pl.pallas_call(
        _mla_proj_kernel,
        grid=(rows // tm,),
        in_specs=[pl.BlockSpec((tm, D_MODEL), lambda i: (i, 0)),
                  pl.BlockSpec((1, N_MOD, D_MODEL), lambda i: (i // per, 0, 0)),
                  whole(wd), whole(qn), whole(kvn), whole(wq), whole(wkv)],
        out_specs=[pl.BlockSpec((tm, nq), lambda i: (i, 0)), pl.BlockSpec((tm, nkv), lambda i: (i, 0)),
                   pl.BlockSpec((tm, nr), lambda i: (i, 0))],
        out_shape=[jax.ShapeDtypeStruct((rows, nq), BF16), jax.ShapeDtypeStruct((rows, nkv), BF16),
                   jax.ShapeDtypeStruct((rows, nr), F32)],
        compiler_params=_params("parallel"),
        name="mla_proj",
    )(h, mod, wd, qn, kvn, wq, wkv)


def _mla_layer(h, mod, rows, nb, j, cs, p):
    wd = p["mla_w_down"][j]
    kpe_w = wd[:, MLA_Q_RANK + MLA_KV_RANK:]
    wd = jnp.concatenate([wd, _rot_cols(kpe_w),
                          jnp.zeros((D_MODEL, MLA_DOWN_PAD - wd.shape[1] - MLA_ROPE), F32)], axis=1).astype(BF16)
    wq = p["mla_w_uq"][j].reshape(MLA_Q_RANK, MLA_HEADS, MLA_QK)
    wq = jnp.concatenate([wq, _rot_cols(wq[..., MLA_NOPE:])], axis=-1).reshape(MLA_Q_RANK, -1).astype(BF16)
    q, kv, kr = _mla_proj(h, mod, wd, p["mla_q_norm"][j], p["mla_kv_norm"][j], wq, p["mla_w_ukv"][j].astype(BF16),
                          rows=rows)
    o = _attention(q, kv, kr, cs, nb=nb)
    return _mm(o, p["mla_w_o"][j].astype(BF16), rows=rows, tn=D_MODEL, epi="gres", mod=mod, gslot=5, res=h,
               name="mla_out")


def kernel(x, c, ctx, c_ctx, ada_w, ada_b, ffn_in, ffn_out, hy_in_w, hy_in_b, hy_conv_w, hy_conv_b, hy_pos_w1, hy_pos_b1, hy_freq, hy_pos_w2, hy_pos_b2, hy_pos_w3, hy_bias, hy_out_w, hy_out_b, mb_in_w, mb_conv_w, mb_conv_b, mb_dt_bias, mb_A_log, mb_D, mb_norm_w, mb_out_w, mla_w_down, mla_q_norm, mla_w_uq, mla_kv_norm, mla_w_ukv, mla_w_o, final_norm_w):
    p = dict(hy_in_w=hy_in_w, hy_in_b=hy_in_b, hy_conv_w=hy_conv_w, hy_conv_b=hy_conv_b, hy_pos_w1=hy_pos_w1,
             hy_pos_b1=hy_pos_b1, hy_freq=hy_freq, hy_pos_w2=hy_pos_w2, hy_pos_b2=hy_pos_b2, hy_pos_w3=hy_pos_w3,
             hy_bias=hy_bias, hy_out_w=hy_out_w, hy_out_b=hy_out_b, mb_in_w=mb_in_w, mb_conv_w=mb_conv_w,
             mb_conv_b=mb_conv_b, mb_dt_bias=mb_dt_bias, mb_A_log=mb_A_log, mb_D=mb_D, mb_norm_w=mb_norm_w,
             mb_out_w=mb_out_w, mla_w_down=mla_w_down, mla_q_norm=mla_q_norm, mla_w_uq=mla_w_uq,
             mla_kv_norm=mla_kv_norm, mla_w_ukv=mla_w_ukv, mla_w_o=mla_w_o)
    nb, n_lat, _ = x.shape
    assert n_lat == MOD_ROWS and nb * ctx.shape[1] == MOD_ROWS and ctx.shape[1] == CTX_LEN
    lat_rows = nb * n_lat
    all_rows = lat_rows + MOD_ROWS
    cs = _rope_table(n_lat)

    cc = jnp.concatenate([c, c_ctx[None], jnp.zeros((16 - nb - 1, D_MODEL), F32)], axis=0)
    mods = _mod_all(cc, ada_w, ada_b).reshape(DEPTH, 16, N_MOD, D_MODEL)
    w_ffn_in, w_ffn_out = ffn_in.astype(BF16), ffn_out.astype(BF16)
    h = None

    for i in range(DEPTH):
        kind, j, last = i % N_MIXERS, i // N_MIXERS, i == DEPTH - 1
        mod = mods[i]
        ctx_needed = not (last and kind == 0)
        ctx_out = not last
        rows = all_rows if ctx_needed else lat_rows
        if i == 0:
            h = _ffn(x.reshape(lat_rows, D_MODEL), mod, w_ffn_in, w_ffn_out, i, 0, rows=lat_rows, slot=0,
                     out_rows=rows)
            if ctx_needed:
                h = _ffn(ctx.reshape(MOD_ROWS, D_MODEL), mod, w_ffn_in, w_ffn_out, i, 0, rows=MOD_ROWS, slot=0,
                         out_row0=lat_rows, dest=h)
        else:
            h = _ffn(h, mod, w_ffn_in, w_ffn_out, i, 0, rows=rows, slot=0)
        if kind == 0:
            h = _hyena_layer(h, mod, rows, nb, j, ctx_out, p)
        elif kind == 1:
            h = _mamba_layer(h, mod, rows, nb, j, p)
        else:
            h = _mla_layer(h, mod, rows, nb, j, cs, p)
        rows = all_rows if ctx_out else lat_rows
        h = _ffn(h, mod, w_ffn_in, w_ffn_out, i, 1, rows=rows, slot=6, final_w=final_norm_w if last else None)
    return h[:lat_rows].reshape(nb, n_lat, D_MODEL)
```

```python
import functools
import math

import jax
import jax.numpy as jnp
from jax import lax
from jax.experimental import pallas as pl
from jax.experimental.pallas import tpu as pltpu

F32 = jnp.float32
BF16 = jnp.bfloat16
HIGHEST = lax.Precision.HIGHEST

D_MODEL = 1024
DEPTH = 4
GRID_W = 64
CTX_LEN = 256
N_MIXERS = 3
N_MOD = 9
FFN_DIM = 2816
EPS = 1e-6

HYENA_EMB = 33
HYENA_BANDS = (HYENA_EMB - 1) // 2
HYENA_FILTER_HIDDEN = 64
HYENA_FAST_DECAY = 0.3
HYENA_SLOW_DECAY = 1.5
HYENA_DECAY_TARGET = 1e-2

SSM_INNER = 2 * D_MODEL
SSM_HEAD_DIM = 64
SSM_HEADS = SSM_INNER // SSM_HEAD_DIM
SSM_GROUPS = 8
HEADS_PER_GROUP = SSM_HEADS // SSM_GROUPS
SSM_STATE = 128
SSM_CHUNK = 128
SSM_BC_DIM = SSM_GROUPS * SSM_STATE
SSM_CONV_DIM = SSM_INNER + 2 * SSM_BC_DIM
SSM_IN_DIM = SSM_INNER + SSM_CONV_DIM + 2 * SSM_HEADS
SSM_ZDT_PAD = SSM_INNER + 128

MLA_HEADS = 16
MLA_NOPE = 64
MLA_ROPE = 32
MLA_V = 64
MLA_Q_RANK = 768
MLA_KV_RANK = 256
MLA_QK = MLA_NOPE + MLA_ROPE
MLA_SCALE = MLA_QK ** -0.5
MLA_DOWN_PAD = 1152
ROPE_AXIS = MLA_ROPE // 2
ROPE_BASE = 10000.0

MOD_ROWS = 2048
VMEM_LIMIT_BYTES = 56 * 1024 * 1024


def _params(*sem):
    return pltpu.CompilerParams(dimension_semantics=sem, vmem_limit_bytes=VMEM_LIMIT_BYTES)


def _rms(x):
    return x * lax.rsqrt(jnp.mean(x * x, axis=-1, keepdims=True) + EPS)


def _silu(x):
    return x * jax.nn.sigmoid(x)


def _dot(a, b, **kw):
    return jnp.dot(a, b, preferred_element_type=F32, **kw)


def _dot_nt(a, b):
    return lax.dot_general(a, b, (((1,), (1,)), ((), ())), preferred_element_type=F32)


def _mod_kernel(x_ref, w_ref, b_ref, o_ref):
    a = _silu(x_ref[...]).astype(BF16)
    o_ref[...] = _dot(a, w_ref[...].astype(BF16)) + b_ref[...]


def _mod_all(cc, ada_w, ada_b):
    n = N_MOD * D_MODEL
    tn = 1024
    return pl.pallas_call(
        _mod_kernel,
        grid=(DEPTH, n // tn),
        in_specs=[pl.BlockSpec((16, D_MODEL), lambda l, j: (0, 0)),
                  pl.BlockSpec((None, D_MODEL, tn), lambda l, j: (l, 0, j)),
                  pl.BlockSpec((None, 1, tn), lambda l, j: (l, 0, j))],
        out_specs=pl.BlockSpec((None, 16, tn), lambda l, j: (l, 0, j)),
        out_shape=jax.ShapeDtypeStruct((DEPTH, 16, n), F32),
        compiler_params=_params("parallel", "parallel"),
        name="adaln_mod",
    )(cc, ada_w, ada_b.reshape(DEPTH, 1, n))


HALO = 16


def _mm_kernel(*refs, pro, epi, has_bias, slot, gslot, tile0, seq_len, has_dest):
    conv = epi in ("conv", "conv_silu")
    it = iter(refs)
    x_ref, w_ref = next(it), next(it)
    xp_ref, xn_ref = (next(it), next(it)) if conv else (None, None)
    mod_ref = next(it) if (pro == "adaln" or epi == "gres") else None
    ssd_refs = [next(it) for _ in range(5)] if pro == "ssd" else None
    b_ref = next(it) if has_bias else None
    res_ref = next(it) if epi == "gres" else None
    cw_ref, cb_ref = (next(it), next(it)) if conv else (None, None)
    if has_dest:
        next(it)
    o_ref, xa_ref = next(it), next(it)
    tm = x_ref.shape[0]

    def prologue(x):
        x = x.astype(F32)
        if pro == "adaln":
            x = _rms(x) * (1.0 + mod_ref[0, slot + 1:slot + 2, :]) + mod_ref[0, slot:slot + 1, :]
        elif pro == "ssd":
            yb_ref, xs_ref, z_ref, dsk_ref, gn_ref = ssd_refs
            y = (x + yb_ref[...].astype(F32) + dsk_ref[...] * xs_ref[...].astype(F32)) * _silu(z_ref[...])
            gw = SSM_INNER // SSM_GROUPS
            x = jnp.concatenate([_rms(y[:, g * gw:(g + 1) * gw]) * gn_ref[:, g * gw:(g + 1) * gw]
                                 for g in range(SSM_GROUPS)], axis=1)
        return x

    if conv:
        row0 = (tile0 + pl.program_id(0)) * tm
        keep_top = jnp.where((row0 & (seq_len - 1)) == 0, 0.0, 1.0)
        keep_bot = jnp.where(((row0 + tm) & (seq_len - 1)) == 0, 0.0, 1.0)

    @pl.when(pl.program_id(1) == 0)
    def _():
        if conv:
            xa_ref[0:HALO, :] = (prologue(xp_ref[...]) * keep_top).astype(BF16)
            xa_ref[HALO:HALO + tm, :] = prologue(x_ref[...]).astype(BF16)
            xa_ref[HALO + tm:2 * HALO + tm, :] = (prologue(xn_ref[...]) * keep_bot).astype(BF16)
        else:
            xa_ref[...] = prologue(x_ref[...]).astype(BF16)

    acc = _dot(xa_ref[...], w_ref[...])
    if conv:
        w0, w1, w2 = cw_ref[0:1, :], cw_ref[1:2, :], cw_ref[2:3, :]
        prev = pltpu.roll(acc, 1, 0)[HALO:HALO + tm]
        nxt = pltpu.roll(acc, tm + 2 * HALO - 1, 0)[HALO:HALO + tm]
        const = cb_ref[...] + (b_ref[...] * (w0 + w1 + w2) if has_bias else 0.0)
        y = w0 * prev + w1 * acc[HALO:HALO + tm] + w2 * nxt + const
        act = (lambda v: _silu(v.astype(o_ref.dtype))) if epi == "conv_silu" else (lambda v: v)
        o_ref[...] = act(y).astype(o_ref.dtype)
        if has_bias:
            o_ref[0:1, :] = act(y[0:1] - (1.0 - keep_top) * (b_ref[...] * w0)).astype(o_ref.dtype)
            o_ref[tm - 1:tm, :] = act(y[tm - 1:tm] - (1.0 - keep_bot) * (b_ref[...] * w2)).astype(o_ref.dtype)
        return
    if has_bias:
        acc = acc + b_ref[...]
    if epi == "gres":
        acc = res_ref[...] + mod_ref[0, gslot:gslot + 1, :] * acc
    o_ref[...] = acc.astype(o_ref.dtype)


def _mm(x, w, *, rows, tn, tm=1024, pro="none", epi="none", mod=None, slot=0, gslot=0,
        ssd=None, bias=None, res=None, conv_w=None, conv_b=None, seq_len=None, row0=0, out_rows=None,
        dest=None, out_dtype=F32, name="mm"):
    k, n = w.shape
    assert rows % tm == 0 and row0 % tm == 0 and n % tn == 0 and MOD_ROWS % tm == 0
    if epi == "gres":
        assert tn == n == D_MODEL
    conv = epi in ("conv", "conv_silu")
    per = MOD_ROWS // tm
    t0 = row0 // tm
    ins = [x, w]
    specs = [pl.BlockSpec((tm, k), lambda i, j: (t0 + i, 0)),
             pl.BlockSpec((k, tn), lambda i, j: (0, j))]
    if conv:
        assert seq_len % tm == 0 and (bias is None or out_dtype == F32)
        hb = tm // HALO
        last = (row0 + rows) // HALO - 1
        ins += [x, x]
        specs += [pl.BlockSpec((HALO, k), lambda i, j: (jnp.maximum((t0 + i) * hb - 1, 0), 0)),
                  pl.BlockSpec((HALO, k), lambda i, j: (jnp.minimum((t0 + i + 1) * hb, last), 0))]
    if pro == "adaln" or epi == "gres":
        ins.append(mod)
        specs.append(pl.BlockSpec((1, N_MOD, D_MODEL), lambda i, j: ((t0 + i) // per, 0, 0)))
    if pro == "ssd":
        yb, xs, z, d_skip, norm_w = ssd
        ins += [yb, xs, z, d_skip.reshape(1, k), norm_w.reshape(1, k)]
        specs += [pl.BlockSpec((tm, k), lambda i, j: (t0 + i, 0))] * 3 + [pl.BlockSpec((1, k), lambda i, j: (0, 0))] * 2
    if bias is not None:
        ins.append(bias.reshape(1, n))
        specs.append(pl.BlockSpec((1, tn), lambda i, j: (0, j)))
    if epi == "gres":
        ins.append(res)
        specs.append(pl.BlockSpec((tm, tn), lambda i, j: (t0 + i, j)))
    if conv:
        ins += [conv_w, conv_b.reshape(1, n)]
        specs += [pl.BlockSpec((3, tn), lambda i, j: (0, j)), pl.BlockSpec((1, tn), lambda i, j: (0, j))]
    aliases = {}
    out_shape = jax.ShapeDtypeStruct((out_rows or row0 + rows, n), out_dtype)
    if dest is not None:
        aliases = {len(ins): 0}
        ins.append(dest)
        specs.append(pl.BlockSpec(memory_space=pl.ANY))
        out_shape = jax.ShapeDtypeStruct(dest.shape, dest.dtype)
    return pl.pallas_call(
        functools.partial(_mm_kernel, pro=pro, epi=epi, has_bias=bias is not None, slot=slot, gslot=gslot,
                          tile0=t0, seq_len=seq_len, has_dest=dest is not None),
        grid=(rows // tm, n // tn),
        in_specs=specs,
        out_specs=pl.BlockSpec((tm, tn), lambda i, j: (t0 + i, j)),
        out_shape=out_shape,
        scratch_shapes=[pltpu.VMEM((tm + 2 * HALO if conv else tm, k), BF16)],
        input_output_aliases=aliases,
        compiler_params=_params("parallel", "arbitrary"),
        name=name,
    )(*ins)


def _ffn_kernel(*refs, slot, final, fc, has_dest):
    if has_dest:
        refs = refs[:-2] + refs[-1:]
    if final:
        x_ref, mod_ref, wi_ref, wo_ref, fw_ref, o_ref = refs
    else:
        x_ref, mod_ref, wi_ref, wo_ref, o_ref = refs
    x = x_ref[...]
    a = (_rms(x) * (1.0 + mod_ref[0, slot + 1:slot + 2, :]) + mod_ref[0, slot:slot + 1, :]).astype(BF16)
    acc = None
    for c in range(0, FFN_DIM, fc):
        g = _dot(a, wi_ref[:, c:c + fc])
        u = _dot(a, wi_ref[:, FFN_DIM + c:FFN_DIM + c + fc])
        t = _dot((_silu(g) * u).astype(BF16), wo_ref[c:c + fc, :])
        acc = t if acc is None else acc + t
    out = x + (0.5 * mod_ref[0, slot + 2:slot + 3, :]) * acc
    if final:
        out = _rms(out) * fw_ref[...]
    o_ref[...] = out


def _ffn(h, mod, w_in, w_out, layer, k, *, rows, slot, final_w=None, out_row0=0, out_rows=None, dest=None,
         tm=512, fc=FFN_DIM):
    per = MOD_ROWS // tm
    o0 = out_row0 // tm
    final = final_w is not None
    resident = pl.Buffered(1)
    ins = [h, mod, w_in, w_out]
    specs = [pl.BlockSpec((tm, D_MODEL), lambda i: (i, 0)),
             pl.BlockSpec((1, N_MOD, D_MODEL), lambda i: ((o0 + i) // per, 0, 0)),
             pl.BlockSpec((None, None, D_MODEL, 2 * FFN_DIM), lambda i: (layer, k, 0, 0), pipeline_mode=resident),
             pl.BlockSpec((None, None, FFN_DIM, D_MODEL), lambda i: (layer, k, 0, 0), pipeline_mode=resident)]
    if final:
        ins.append(final_w.reshape(1, D_MODEL))
        specs.append(pl.BlockSpec((1, D_MODEL), lambda i: (0, 0)))
    aliases = {}
    out_shape = jax.ShapeDtypeStruct((out_rows or rows, D_MODEL), F32)
    if dest is not None:
        aliases = {len(ins): 0}
        ins.append(dest)
        specs.append(pl.BlockSpec(memory_space=pl.ANY))
        out_shape = jax.ShapeDtypeStruct(dest.shape, dest.dtype)
    return pl.pallas_call(
        functools.partial(_ffn_kernel, slot=slot, final=final, fc=fc, has_dest=dest is not None),
        grid=(rows // tm,),
        in_specs=specs,
        out_specs=pl.BlockSpec((tm, D_MODEL), lambda i: (o0 + i, 0)),
        out_shape=out_shape,
        input_output_aliases=aliases,
        compiler_params=_params("parallel"),
        name="ffn",
    )(*ins)


FILTER_W = 128


def _filter_kernel(za_ref, zb_ref, w1_ref, b1_ref, fr_ref, w2_ref, b2_ref, w3f_ref, w3b_ref, dl_ref,
                   sum_ref, diff_ref):
    fr = fr_ref[...]

    def taps(z_ref, w3_ref):
        z = z_ref[...]
        hid = jnp.sin(fr * (_dot(z, w1_ref[...], precision=HIGHEST) + b1_ref[...]))
        hid = jnp.sin(fr * (_dot(hid, w2_ref[...], precision=HIGHEST) + b2_ref[...]))
        window = jnp.exp(-z[:, 0:1] * dl_ref[...]) * z[:, FILTER_W - 1:FILTER_W]
        h = _dot(hid.astype(BF16), w3_ref[...].astype(BF16))
        return jnp.concatenate([h[:, :D_MODEL] * window, h[:, D_MODEL:] * window], axis=1)

    hf = taps(za_ref, w3f_ref)
    hb = taps(zb_ref, w3b_ref)
    sum_ref[...] = (hf + hb).astype(BF16)
    diff_ref[...] = (hb - hf).astype(BF16)


def _hyena_filters(L, poly, w1, b1, freq, w2, b2, w3):
    S = L // poly
    nrho = 2 * poly - 1
    jj = jnp.arange(S, dtype=jnp.int32)
    rho = jnp.arange(-(poly - 1), poly, dtype=jnp.int32)
    t_tab = jnp.linspace(0.0, 1.0, L, dtype=F32)
    bands = jnp.linspace(1e-4, HYENA_BANDS - 1, HYENA_BANDS, dtype=F32)

    def feats(pos, lowest):
        valid = (pos >= lowest).astype(F32)
        pc = jnp.maximum(pos, 0)
        ang = (2.0 * math.pi / L) * pc.astype(F32)[..., None] * bands
        z = jnp.concatenate([jnp.take(t_tab, pc)[..., None], jnp.cos(ang), -jnp.sin(ang)], axis=-1)
        z = jnp.pad(z, ((0, 0), (0, 0), (0, FILTER_W - 1 - HYENA_EMB)))
        return jnp.concatenate([z, valid[..., None]], axis=-1).reshape(nrho * S, FILTER_W)

    za = feats(poly * jj[None, :] + rho[:, None], 0)
    zb = feats(poly * jj[None, :] - rho[:, None], 1)
    ph = FILTER_W - HYENA_FILTER_HIDDEN
    w1 = jnp.pad(w1, ((0, FILTER_W - HYENA_EMB), (0, ph)))
    w2 = jnp.pad(w2, ((0, ph), (0, ph)))
    w3 = jnp.pad(w3, ((0, ph), (0, 0))).reshape(FILTER_W, 2, 2, D_MODEL)
    w3f = w3[:, :, 0].reshape(FILTER_W, 2 * D_MODEL)
    w3b = w3[:, :, 1].reshape(FILTER_W, 2 * D_MODEL)
    b1, b2, freq = (jnp.pad(t, (0, ph)).reshape(1, FILTER_W) for t in (b1, b2, freq))
    max_decay = math.log(HYENA_DECAY_TARGET) / HYENA_FAST_DECAY
    min_decay = math.log(HYENA_DECAY_TARGET) / HYENA_SLOW_DECAY
    deltas = jnp.abs(jnp.linspace(min_decay, max_decay, D_MODEL, dtype=F32))
    deltas = deltas.reshape(1, D_MODEL)
    full = lambda shape: pl.BlockSpec(shape, lambda i: (0, 0))
    sq = full((FILTER_W, FILTER_W))
    vec = full((1, FILTER_W))
    return pl.pallas_call(
        _filter_kernel,
        grid=(nrho,),
        in_specs=[pl.BlockSpec((S, FILTER_W), lambda i: (i, 0)), pl.BlockSpec((S, FILTER_W), lambda i: (i, 0)),
                  sq, vec, vec, sq, vec, full((FILTER_W, 2 * D_MODEL)), full((FILTER_W, 2 * D_MODEL)),
                  full((1, D_MODEL))],
        out_specs=[pl.BlockSpec((S, 2 * D_MODEL), lambda i: (i, 0))] * 2,
        out_shape=[jax.ShapeDtypeStruct((nrho * S, 2 * D_MODEL), BF16)] * 2,
        compiler_params=_params("parallel"),
        name="hyena_filter",
    )(za, zb, w1, b1, freq, w2, b2, w3f, w3b, deltas)


def _lmm_kernel(*refs, epi, tf, scale, has_dest):
    it = iter(refs)
    a_ref, x_ref = next(it), next(it)
    if epi == "spec":
        k_ref = next(it)
    elif epi == "gate":
        g_ref, v_ref, bias_ref = next(it), next(it), next(it)
    if has_dest:
        next(it)
    o_ref, xb_ref = next(it), next(it)

    @pl.when(pl.program_id(2) == 0)
    def _():
        xb_ref[...] = x_ref[...].astype(BF16)

    acc = _dot(a_ref[...], xb_ref[...])
    if epi == "spec":
        xr, xs = acc[:tf], acc[tf:]
        kr, ki = k_ref[0:tf, :], k_ref[tf:2 * tf, :]
        o_ref[0:tf, :] = ((xr * kr + xs * ki) * scale).astype(o_ref.dtype)
        o_ref[tf:2 * tf, :] = ((xs * kr - xr * ki) * scale).astype(o_ref.dtype)
    elif epi == "gate":
        o_ref[...] = (g_ref[...] * (acc + v_ref[...].astype(F32) * bias_ref[...])).astype(o_ref.dtype)
    else:
        o_ref[...] = acc.astype(o_ref.dtype)


def _lmm(a, x, *, nb, ncb, tm, n, x_row0=0, x_cb0=0, epi="none", tf=0, scale=1.0, kspec=None, k_cb=0,
         g=None, g_cb=0, g_row0=0, v=None, v_cb=0, v_row0=0, bias=None, out_dtype=F32, dest=None, dest_row0=0,
         name="lmm"):
    mo, k = a.shape
    nm = mo // tm
    xr0 = x_row0 // k
    or0 = dest_row0 // tm
    ins = [a, x]
    specs = [pl.BlockSpec((tm, k), lambda b, c, m: (m, 0)),
             pl.BlockSpec((k, n), lambda b, c, m: (xr0 + b, x_cb0 + c))]
    if epi == "spec":
        ins.append(kspec)
        specs.append(pl.BlockSpec((tm, n), lambda b, c, m: (m, k_cb)))
    elif epi == "gate":
        gr0, vr0 = g_row0 // tm, v_row0 // tm
        ins += [g, v, bias.reshape(1, n)]
        specs += [pl.BlockSpec((tm, n), lambda b, c, m: (gr0 + b * nm + m, g_cb)),
                  pl.BlockSpec((tm, n), lambda b, c, m: (vr0 + b * nm + m, v_cb)),
                  pl.BlockSpec((1, n), lambda b, c, m: (0, 0))]
    aliases = {}
    out_shape = jax.ShapeDtypeStruct((nb * mo, ncb * n), out_dtype)
    if dest is not None:
        aliases = {len(ins): 0}
        ins.append(dest)
        specs.append(pl.BlockSpec(memory_space=pl.ANY))
        out_shape = jax.ShapeDtypeStruct(dest.shape, dest.dtype)
    return pl.pallas_call(
        functools.partial(_lmm_kernel, epi=epi, tf=tf, scale=scale, has_dest=dest is not None),
        grid=(nb, ncb, nm),
        in_specs=specs,
        out_specs=pl.BlockSpec((tm, n), lambda b, c, m: (or0 + b * nm + m, c)),
        out_shape=out_shape,
        scratch_shapes=[pltpu.VMEM((k, n), BF16)],
        input_output_aliases=aliases,
        compiler_params=_params("parallel", "parallel", "arbitrary"),
        name=name,
    )(*ins)


def _dft_mats(L, tf):
    idx = jnp.arange(L, dtype=jnp.int32)
    ph = ((2 * idx[:, None] + 1) * idx[None, :]) % (4 * L)
    ang = ph.astype(F32) * (2.0 * math.pi / (4 * L))
    c, s = jnp.cos(ang), jnp.sin(ang)
    nf = L // tf
    fwd = jnp.stack([c.reshape(nf, tf, L), s.reshape(nf, tf, L)], axis=1).reshape(2 * L, L)
    inv = jnp.stack([c.T.reshape(L, nf, tf), s.T.reshape(L, nf, tf)], axis=2).reshape(L, 2 * L)
    return c.astype(BF16), s.astype(BF16), fwd.astype(BF16), inv.astype(BF16)


def _interleave(kr, ki, tf):
    L, n = kr.shape
    return jnp.stack([kr.reshape(L // tf, tf, n), ki.reshape(L // tf, tf, n)], axis=1).reshape(2 * L, n)


def _hyena_core(u, L, nb, row0, fparams, hy_bias, name, dest=None):
    tf = min(L, 256)
    tmi = min(L, 256)
    cmat, smat, fwd, inv = _dft_mats(L, tf)
    hsum, hdiff = _hyena_filters(L, 1, *fparams)
    kr = _lmm(cmat, hsum, nb=1, ncb=2, tm=tf, n=D_MODEL, name=name + "_kr")
    ki = _lmm(smat, hdiff, nb=1, ncb=2, tm=tf, n=D_MODEL, name=name + "_ki")
    kspec = _interleave(kr, ki, tf)
    scale = 1.0 / L
    y1 = _lmm(fwd, u, nb=nb, ncb=1, tm=2 * tf, n=D_MODEL, x_row0=row0, x_cb0=2, epi="spec", tf=tf, scale=scale,
              kspec=kspec, k_cb=0, out_dtype=BF16, name=name + "_fwd1")
    z = _lmm(inv, y1, nb=nb, ncb=1, tm=tmi, n=D_MODEL, epi="gate", g=u, g_cb=0, g_row0=row0, v=u, v_cb=2,
             v_row0=row0, bias=hy_bias[0], out_dtype=F32, name=name + "_inv1")
    y2 = _lmm(fwd, z, nb=nb, ncb=1, tm=2 * tf, n=D_MODEL, epi="spec", tf=tf, scale=scale,
              kspec=kspec, k_cb=1, out_dtype=BF16, name=name + "_fwd2")
    return _lmm(inv, y2, nb=nb, ncb=1, tm=tmi, n=D_MODEL, epi="gate", g=u, g_cb=1, g_row0=row0, v=z, v_cb=0,
                bias=hy_bias[1], out_dtype=F32, dest=dest, dest_row0=row0, name=name + "_inv2")


POLY = 4
POLY_SEQS = 2


def _poly_conv_kernel(a_ref, ai_ref, u_ref, kr_ref, ki_ref, g_ref, bias_ref, o_ref, *, scale):
    S = a_ref.shape[1]
    L = S * POLY
    tc = u_ref.shape[1]
    for row0 in range(0, u_ref.shape[0], L):
        phases = [u_ref[pl.ds(row0 + r, S, stride=POLY), :] for r in range(POLY)]
        e = _dot(a_ref[...], jnp.concatenate([ph.astype(BF16) for ph in phases], axis=1))
        f = []
        for r in range(POLY):
            fre = fim = None
            for rp in range(POLY):
                kr = kr_ref[r - rp + POLY - 1]
                ki = ki_ref[r - rp + POLY - 1]
                er = e[0:S, rp * tc:(rp + 1) * tc]
                es = e[S:2 * S, rp * tc:(rp + 1) * tc]
                tre = kr * er + ki * es
                tim = kr * es - ki * er
                fre = tre if fre is None else fre + tre
                fim = tim if fim is None else fim + tim
            f.append(jnp.concatenate([fre, fim], axis=0).astype(BF16))
        y = _dot(ai_ref[...], jnp.concatenate(f, axis=1)) * scale
        for r in range(POLY):
            rows = pl.ds(row0 + r, S, stride=POLY)
            o_ref[rows, :] = g_ref[rows, :] * (y[:, r * tc:(r + 1) * tc] + phases[r] * bias_ref[...])


def _poly_conv(fwd, inv, x, x_cb, g, g_cb, kre, kim, order, bias, *, nb, out_rows=None, tc=128, name="hy_conv"):
    L = MOD_ROWS
    S = L // POLY
    nct = D_MODEL // tc
    nrho = 2 * POLY - 1
    assert nb % POLY_SEQS == 0
    rb = POLY_SEQS * L
    return pl.pallas_call(
        functools.partial(_poly_conv_kernel, scale=1.0 / S),
        grid=(nct, nb // POLY_SEQS),
        in_specs=[pl.BlockSpec((2 * S, S), lambda c, b: (0, 0)),
                  pl.BlockSpec((S, 2 * S), lambda c, b: (0, 0)),
                  pl.BlockSpec((rb, tc), lambda c, b: (b, x_cb * nct + c)),
                  pl.BlockSpec((nrho, S, tc), lambda c, b: (0, 0, order * nct + c)),
                  pl.BlockSpec((nrho, S, tc), lambda c, b: (0, 0, order * nct + c)),
                  pl.BlockSpec((rb, tc), lambda c, b: (b, g_cb * nct + c)),
                  pl.BlockSpec((1, tc), lambda c, b: (0, c))],
        out_specs=pl.BlockSpec((rb, tc), lambda c, b: (b, c)),
        out_shape=jax.ShapeDtypeStruct((out_rows or nb * L, D_MODEL), F32),
        compiler_params=_params("parallel", "parallel"),
        name=name,
    )(fwd, inv, x, kre, kim, g, bias.reshape(1, D_MODEL))


def _hyena_core_poly(u, nb, fparams, hy_bias, out_rows):
    S = MOD_ROWS // POLY
    nrho = 2 * POLY - 1
    cmat, smat, fwd, inv = _dft_mats(S, S)
    ksum, kdiff = _hyena_filters(MOD_ROWS, POLY, *fparams)
    kre = _lmm(cmat, ksum, nb=nrho, ncb=2, tm=S, n=D_MODEL, name="hy_lat_kr").reshape(nrho, S, 2 * D_MODEL)
    kim = _lmm(smat, kdiff, nb=nrho, ncb=2, tm=S, n=D_MODEL, name="hy_lat_ki").reshape(nrho, S, 2 * D_MODEL)
    z = _poly_conv(fwd, inv, u, 2, u, 0, kre, kim, 0, hy_bias[0], nb=nb, name="hy_lat_conv1")
    return _poly_conv(fwd, inv, z, 0, u, 1, kre, kim, 1, hy_bias[1], nb=nb, out_rows=out_rows, name="hy_lat_conv2")


def _softplus(x):
    return jnp.maximum(x, 0.0) + jnp.log(1.0 + jnp.exp(-jnp.abs(x)))


def _ssd_kernel(xbc_f, dt_f, dtt_f, xbc_b, dt_b, dtt_b, dtb_ref, dtbt_ref, al_ref, alt_ref, yf_ref, yb_ref, st_ref):
    @pl.when(pl.program_id(1) == 0)
    def _():
        st_ref[...] = jnp.zeros_like(st_ref)

    _ssd_chunk(0, xbc_f, dt_f, dtt_f, dtb_ref[0], dtbt_ref[0], al_ref[0], alt_ref[0], yf_ref, st_ref.at[0])
    _ssd_chunk(1, xbc_b, dt_b, dtt_b, dtb_ref[1], dtbt_ref[1], al_ref[1], alt_ref[1], yb_ref, st_ref.at[1])


def _ssd_chunk(d, xbc_ref, dt_ref, dtt_ref, dt_bias, dt_bias_t, a_log, a_log_t, y_ref, st_ref):
    T = SSM_CHUNK
    N = SSM_STATE
    ri = lax.broadcasted_iota(jnp.int32, (T, T), 0)
    ci = lax.broadcasted_iota(jnp.int32, (T, T), 1)
    mask = (ri >= ci) if d == 0 else (ri <= ci)
    tri_col = mask.astype(F32)
    tri_row = ((ci >= ri) if d == 0 else (ci <= ri)).astype(F32)
    low = lax.broadcasted_iota(jnp.int32, (T, 2 * SSM_HEAD_DIM), 1) < SSM_HEAD_DIM

    a = _softplus(dt_ref[...] + dt_bias) * -jnp.exp(a_log)
    dtt = _softplus(dtt_ref[...] + dt_bias_t)
    at = dtt * -jnp.exp(a_log_t)
    cs_col = _dot(tri_col, a, precision=HIGHEST)
    cs_row = _dot(at, tri_row, precision=HIGHEST)
    tot = jnp.sum(at, axis=1, keepdims=True)
    row_dt = cs_row - jnp.log(dtt)
    w_out = dtt * jnp.exp(tot - cs_row)
    e_tot = jnp.exp(tot)

    for g in range(SSM_GROUPS):
        b0 = SSM_INNER + g * N
        c0 = SSM_INNER + SSM_BC_DIM + g * N
        bt = jnp.transpose(xbc_ref[:, b0:b0 + N].astype(F32))
        cg = xbc_ref[:, c0:c0 + N].astype(F32)
        cb = _dot(cg.astype(BF16), bt.astype(BF16))
        for k in range(HEADS_PER_GROUP // 2):
            h0 = g * HEADS_PER_GROUP + 2 * k
            lanes = slice(h0 * SSM_HEAD_DIM, (h0 + 2) * SSM_HEAD_DIM)
            xs = xbc_ref[:, lanes].astype(BF16)
            st = st_ref[g, :, k * 128:(k + 1) * 128]
            rhs = jnp.concatenate([xs, st.astype(BF16)], axis=0)
            ys, upds = [], []
            for h in (h0, h0 + 1):
                col = jnp.broadcast_to(cs_col[:, h:h + 1], (T, T))
                m = jnp.where(mask, jnp.exp(col - row_dt[h:h + 1, :]), 0.0) * cb
                lhs = jnp.concatenate([m, jnp.exp(col) * cg], axis=1).astype(BF16)
                ys.append(_dot(lhs, rhs))
                upds.append(_dot((bt * w_out[h:h + 1, :]).astype(BF16), xs))
            y_ref[:, lanes] = jnp.where(low, ys[0], ys[1]).astype(y_ref.dtype)
            decay = jnp.where(low, e_tot[h0:h0 + 1, :], e_tot[h0 + 1:h0 + 2, :])
            st_ref[g, :, k * 128:(k + 1) * 128] = st * decay + jnp.where(low, upds[0], upds[1])


def _ssd(xbc, dta, dtt, dt_bias, a_log, *, nb):
    m = xbc.shape[0]
    T = SSM_CHUNK
    H = SSM_HEADS
    lat_c = MOD_ROWS // T
    ctx_c = CTX_LEN // T
    ctx0 = nb * lat_c
    steps = lat_c + ctx_c

    def rb(b, d, t):
        ctx = ctx0 + ctx_c * b + jnp.where(d == 0, t, ctx_c - 1 - t)
        lat = lat_c * b + jnp.where(d == 0, t - ctx_c, steps - 1 - t)
        return jnp.where(t < ctx_c, ctx, lat)

    def direction(d):
        return [pl.BlockSpec((T, SSM_CONV_DIM), lambda b, t: (rb(b, d, t), 0)),
                pl.BlockSpec((None, T, H), lambda b, t: (d, rb(b, d, t), 0)),
                pl.BlockSpec((None, H, T), lambda b, t: (d, 0, rb(b, d, t)))]

    whole = lambda shape: pl.BlockSpec(shape, lambda b, t: (0, 0, 0))
    return pl.pallas_call(
        _ssd_kernel,
        grid=(nb, steps),
        in_specs=direction(0) + direction(1) + [whole((2, 1, H)), whole((2, H, 1)), whole((2, 1, H)), whole((2, H, 1))],
        out_specs=[pl.BlockSpec((T, SSM_INNER), lambda b, t: (rb(b, 0, t), 0)),
                   pl.BlockSpec((T, SSM_INNER), lambda b, t: (rb(b, 1, t), 0))],
        out_shape=[jax.ShapeDtypeStruct((m, SSM_INNER), BF16)] * 2,
        scratch_shapes=[pltpu.VMEM((2, SSM_GROUPS, SSM_STATE, HEADS_PER_GROUP * SSM_HEAD_DIM), F32)],
        compiler_params=_params("parallel", "arbitrary"),
        name="ssd_scan",
    )(xbc, dta, dtt, xbc, dta, dtt, dt_bias.reshape(2, 1, H), dt_bias.reshape(2, H, 1),
      a_log.reshape(2, 1, H), a_log.reshape(2, H, 1))


HEAD_W = 2 * MLA_NOPE
ATTN_SUB_ROWS = 256


def _attn_kernel(*refs, has_lat):
    if has_lat:
        q_ref, kvc_ref, krc_ref, kv_ref, kr_ref, tq_ref, tk_ref, o_ref, kc_scr, vc_scr, k_scr, v_scr = refs
    else:
        q_ref, kvc_ref, krc_ref, _, o_ref, kc_scr, vc_scr = refs
    R = MLA_ROPE
    kscale = MLA_SCALE * math.log2(math.e)

    def build_keys():
        krc = krc_ref[...].astype(F32)
        lane_c = lax.broadcasted_iota(jnp.int32, krc.shape, 1)
        pe_c = jnp.where((lane_c >= MLA_NOPE) & (lane_c < MLA_NOPE + R), pltpu.roll(krc, MLA_NOPE, 1), 0.0)
        for hh in range(2):
            kvh = kvc_ref[:, hh * HEAD_W:(hh + 1) * HEAD_W].astype(F32)
            kc_scr[hh] = (jnp.where(lane_c < MLA_NOPE, kvh, pe_c) * kscale).astype(BF16)
            vc_scr[hh] = jnp.where(lane_c < MLA_NOPE, 1.0, kvh).astype(BF16)
        if has_lat:
            t = kr_ref[...].astype(F32) * tk_ref[...]
            lane = lax.broadcasted_iota(jnp.int32, t.shape, 1)
            krot = jnp.where(lane < R, t + pltpu.roll(t, HEAD_W - R, 1), 0.0)
            krr = pltpu.roll(krot, MLA_NOPE, 1) + pltpu.roll(krot, MLA_NOPE + R, 1)
            for hh in range(2):
                kvh = kv_ref[:, hh * HEAD_W:(hh + 1) * HEAD_W].astype(F32)
                k_scr[hh] = (jnp.where(lane < MLA_NOPE, kvh, krr) * kscale).astype(BF16)
                v_scr[hh] = jnp.where(lane < MLA_NOPE, 1.0, kvh).astype(BF16)

    if has_lat:
        pl.when(pl.program_id(2) == 0)(build_keys)
    else:
        build_keys()

    sub = min(q_ref.shape[0], ATTN_SUB_ROWS)
    for r0 in range(0, q_ref.shape[0], sub):
        rows = slice(r0, r0 + sub)
        res = []
        for hh in range(2):
            q = q_ref[rows, hh * HEAD_W:(hh + 1) * HEAD_W]
            s_c = _dot_nt(q, kc_scr[hh])
            mx = jnp.max(s_c, axis=-1, keepdims=True)
            if has_lat:
                ql = (q.astype(F32) * tq_ref[rows, :]).astype(BF16)
                s_l = _dot_nt(ql, k_scr[hh])
                mx = jnp.maximum(mx, jnp.max(s_l, axis=-1, keepdims=True))
                acc = _dot(jnp.exp2(s_l - mx).astype(BF16), v_scr[hh])
                acc = acc + _dot(jnp.exp2(s_c - mx).astype(BF16), vc_scr[hh])
            else:
                acc = _dot(jnp.exp2(s_c - mx).astype(BF16), vc_scr[hh])
            res.append(acc / pltpu.roll(acc, MLA_V, 1))
        lane_o = lax.broadcasted_iota(jnp.int32, res[0].shape, 1)
        o_ref[rows, :] = jnp.where(lane_o < MLA_V, pltpu.roll(res[0], MLA_V, 1), res[1]).astype(o_ref.dtype)


def _attention(q, kv, dn, cs, *, nb, tq=1024):
    L = MOD_ROWS
    hp = MLA_HEADS // 2
    w = 2 * HEAD_W
    nq = L // tq
    cb = nb * (L // CTX_LEN)
    kr_cb = 0
    tab_q = jnp.concatenate([jnp.ones((L, MLA_NOPE), F32), cs], axis=1)
    tab_k = jnp.concatenate([cs, jnp.zeros((L, HEAD_W - 2 * MLA_ROPE), F32)], axis=1)
    lat = pl.pallas_call(
        functools.partial(_attn_kernel, has_lat=True),
        grid=(nb, hp, nq),
        in_specs=[pl.BlockSpec((tq, w), lambda b, p, i: (b * nq + i, p)),
                  pl.BlockSpec((CTX_LEN, w), lambda b, p, i: (cb + b, p)),
                  pl.BlockSpec((CTX_LEN, HEAD_W), lambda b, p, i: (cb + b, kr_cb)),
                  pl.BlockSpec((L, w), lambda b, p, i: (b, p)),
                  pl.BlockSpec((L, HEAD_W), lambda b, p, i: (b, kr_cb)),
                  pl.BlockSpec((tq, HEAD_W), lambda b, p, i: (i, 0)),
                  pl.BlockSpec((L, HEAD_W), lambda b, p, i: (0, 0))],
        out_specs=pl.BlockSpec((tq, 2 * MLA_V), lambda b, p, i: (b * nq + i, p)),
        out_shape=jax.ShapeDtypeStruct((nb * (L + CTX_LEN), MLA_HEADS * MLA_V), BF16),
        scratch_shapes=[pltpu.VMEM((2, CTX_LEN, HEAD_W), BF16)] * 2 + [pltpu.VMEM((2, L, HEAD_W), BF16)] * 2,
        compiler_params=_params("parallel", "parallel", "arbitrary"),
        name="mla_attn",
    )(q, kv, dn, kv, dn, tab_q, tab_k)
    return pl.pallas_call(
        functools.partial(_attn_kernel, has_lat=False),
        grid=(nb, hp),
        in_specs=[pl.BlockSpec((CTX_LEN, w), lambda b, p: (cb + b, p)),
                  pl.BlockSpec((CTX_LEN, w), lambda b, p: (cb + b, p)),
                  pl.BlockSpec((CTX_LEN, HEAD_W), lambda b, p: (cb + b, kr_cb)),
                  pl.BlockSpec(memory_space=pl.ANY)],
        out_specs=pl.BlockSpec((CTX_LEN, 2 * MLA_V), lambda b, p: (cb + b, p)),
        out_shape=jax.ShapeDtypeStruct(lat.shape, lat.dtype),
        scratch_shapes=[pltpu.VMEM((2, CTX_LEN, HEAD_W), BF16)] * 2,
        input_output_aliases={3: 0},
        compiler_params=_params("parallel", "parallel"),
        name="mla_attn_ctx",
    )(q, kv, dn, lat)


def _rot_cols(w):
    wp = w.reshape(w.shape[:-1] + (2, 2, ROPE_AXIS // 2))
    return jnp.stack([-wp[..., 1, :], wp[..., 0, :]], axis=-2).reshape(w.shape)


def _rope_table(n_lat):
    rows = n_lat // GRID_W
    row = jnp.repeat(jnp.arange(rows), GRID_W)
    col = jnp.tile(jnp.arange(GRID_W), rows)
    inv = 1.0 / (ROPE_BASE ** (jnp.arange(0, ROPE_AXIS, 2, dtype=F32) / ROPE_AXIS))
    ar = row.astype(F32)[:, None] * inv[None, :]
    ac = col.astype(F32)[:, None] * inv[None, :]
    ang = jnp.concatenate([ar, ar, ac, ac], axis=-1)
    return jnp.concatenate([jnp.cos(ang), jnp.sin(ang)], axis=-1)


def _conv_proj(h, w, bias, conv_w, conv_b, mod, rows, nb, epi, out_dtype, name):
    lat_rows = nb * MOD_ROWS
    common = dict(pro="adaln", mod=mod, slot=3, bias=bias, epi=epi, conv_w=conv_w, conv_b=conv_b, out_dtype=out_dtype)
    out = _mm(h, w, rows=lat_rows, tn=1024, seq_len=MOD_ROWS, out_rows=rows, name=name, **common)
    if rows > lat_rows:
        out = _mm(h, w, rows=rows - lat_rows, tn=w.shape[1], tm=CTX_LEN, seq_len=CTX_LEN, row0=lat_rows, dest=out,
                  name=name + "_ctx", **common)
    return out


def _hyena_layer(h, mod, rows, nb, j, with_ctx, p):
    u = _conv_proj(h, p["hy_in_w"][j].astype(BF16), p["hy_in_b"][j], p["hy_conv_w"][j], p["hy_conv_b"][j], mod,
                   rows, nb, "conv", F32, "hyena_in")
    fparams = (p["hy_pos_w1"][j], p["hy_pos_b1"][j], p["hy_freq"][j], p["hy_pos_w2"][j], p["hy_pos_b2"][j],
               p["hy_pos_w3"][j])
    y = _hyena_core_poly(u, nb, fparams, p["hy_bias"][j], rows)
    if with_ctx:
        y = _hyena_core(u, CTX_LEN, nb, nb * MOD_ROWS, fparams, p["hy_bias"][j], "hy_ctx", dest=y)
    return _mm(y, p["hy_out_w"][j].astype(BF16), rows=rows, tn=D_MODEL, epi="gres", mod=mod, gslot=5,
               bias=p["hy_out_b"][j], res=h, name="hyena_out")


def _mamba_layer(h, mod, rows, nb, j, p):
    w_in = p["mb_in_w"][j].astype(BF16)
    w_zdt = jnp.concatenate([w_in[:, :SSM_INNER], w_in[:, SSM_INNER + SSM_CONV_DIM:],
                             jnp.zeros((D_MODEL, SSM_ZDT_PAD - SSM_INNER - 2 * SSM_HEADS), BF16)], axis=1)
    zx = _mm(h, w_zdt, rows=rows, tn=SSM_ZDT_PAD, pro="adaln", mod=mod, slot=3, name="mamba_in_zdt")
    xbc = _conv_proj(h, w_in[:, SSM_INNER:SSM_INNER + SSM_CONV_DIM], None, p["mb_conv_w"][j], p["mb_conv_b"][j], mod,
                     rows, nb, "conv_silu", BF16, "mamba_in_xbc")
    dtr = zx[:, SSM_INNER:SSM_INNER + 2 * SSM_HEADS].reshape(rows, 2, SSM_HEADS)
    dta = jnp.transpose(dtr, (1, 0, 2))
    dtt = jnp.transpose(dtr, (1, 2, 0))
    yf, yb = _ssd(xbc, dta, dtt, p["mb_dt_bias"][j], p["mb_A_log"][j], nb=nb)
    d_skip = jnp.repeat(p["mb_D"][j], SSM_HEAD_DIM)
    return _mm(yf, p["mb_out_w"][j].astype(BF16), rows=rows, tn=D_MODEL, tm=512, pro="ssd",
               ssd=(yb, xbc, zx, d_skip, p["mb_norm_w"][j]), epi="gres", mod=mod, gslot=5, res=h, name="mamba_out")


def _mla_proj_kernel(x_ref, mod_ref, wd_ref, qn_ref, kvn_ref, wq_ref, wkv_ref, q_ref, kv_ref, kr_ref):
    a = _rms(x_ref[...]) * (1.0 + mod_ref[0, 4:5, :]) + mod_ref[0, 3:4, :]
    dn = _dot(a.astype(BF16), wd_ref[...])
    cq = _rms(dn[:, :MLA_Q_RANK]) * qn_ref[...]
    ckv = _rms(dn[:, MLA_Q_RANK:MLA_Q_RANK + MLA_KV_RANK]) * kvn_ref[...]
    q_ref[...] = _dot(cq.astype(BF16), wq_ref[...]).astype(q_ref.dtype)
    kv_ref[...] = _dot(ckv.astype(BF16), wkv_ref[...]).astype(kv_ref.dtype)
    kr_ref[...] = dn[:, MLA_Q_RANK + MLA_KV_RANK:]


def _mla_proj(h, mod, wd, q_norm, kv_norm, wq, wkv, *, rows, tm=512):
    per = MOD_ROWS // tm
    nq, nkv = wq.shape[1], wkv.shape[1]
    nr = MLA_DOWN_PAD - MLA_Q_RANK - MLA_KV_RANK
    whole = lambda a: pl.BlockSpec(a.shape, lambda i: (0, 0))
    qn, kvn = q_norm.reshape(1, MLA_Q_RANK), kv_norm.reshape(1, MLA_KV_RANK)
    return pl.pallas_call(
        _mla_proj_kernel,
        grid=(rows // tm,),
        in_specs=[pl.BlockSpec((tm, D_MODEL), lambda i: (i, 0)),
                  pl.BlockSpec((1, N_MOD, D_MODEL), lambda i: (i // per, 0, 0)),
                  whole(wd), whole(qn), whole(kvn), whole(wq), whole(wkv)],
        out_specs=[pl.BlockSpec((tm, nq), lambda i: (i, 0)), pl.BlockSpec((tm, nkv), lambda i: (i, 0)),
                   pl.BlockSpec((tm, nr), lambda i: (i, 0))],
        out_shape=[jax.ShapeDtypeStruct((rows, nq), BF16), jax.ShapeDtypeStruct((rows, nkv), BF16),
                   jax.ShapeDtypeStruct((rows, nr), F32)],
        compiler_params=_params("parallel"),
        name="mla_proj",
    )(h, mod, wd, qn, kvn, wq, wkv)


def _mla_layer(h, mod, rows, nb, j, cs, p):
    wd = p["mla_w_down"][j]
    kpe_w = wd[:, MLA_Q_RANK + MLA_KV_RANK:]
    wd = jnp.concatenate([wd, _rot_cols(kpe_w),
                          jnp.zeros((D_MODEL, MLA_DOWN_PAD - wd.shape[1] - MLA_ROPE), F32)], axis=1).astype(BF16)
    wq = p["mla_w_uq"][j].reshape(MLA_Q_RANK, MLA_HEADS, MLA_QK)
    wq = jnp.concatenate([wq, _rot_cols(wq[..., MLA_NOPE:])], axis=-1).reshape(MLA_Q_RANK, -1).astype(BF16)
    q, kv, kr = _mla_proj(h, mod, wd, p["mla_q_norm"][j], p["mla_kv_norm"][j], wq, p["mla_w_ukv"][j].astype(BF16),
                          rows=rows)
    o = _attention(q, kv, kr, cs, nb=nb)
    return _mm(o, p["mla_w_o"][j].astype(BF16), rows=rows, tn=D_MODEL, epi="gres", mod=mod, gslot=5, res=h,
               name="mla_out")


def kernel(x, c, ctx, c_ctx, ada_w, ada_b, ffn_in, ffn_out, hy_in_w, hy_in_b, hy_conv_w, hy_conv_b, hy_pos_w1, hy_pos_b1, hy_freq, hy_pos_w2, hy_pos_b2, hy_pos_w3, hy_bias, hy_out_w, hy_out_b, mb_in_w, mb_conv_w, mb_conv_b, mb_dt_bias, mb_A_log, mb_D, mb_norm_w, mb_out_w, mla_w_down, mla_q_norm, mla_w_uq, mla_kv_norm, mla_w_ukv, mla_w_o, final_norm_w):
    p = dict(hy_in_w=hy_in_w, hy_in_b=hy_in_b, hy_conv_w=hy_conv_w, hy_conv_b=hy_conv_b, hy_pos_w1=hy_pos_w1,
             hy_pos_b1=hy_pos_b1, hy_freq=hy_freq, hy_pos_w2=hy_pos_w2, hy_pos_b2=hy_pos_b2, hy_pos_w3=hy_pos_w3,
             hy_bias=hy_bias, hy_out_w=hy_out_w, hy_out_b=hy_out_b, mb_in_w=mb_in_w, mb_conv_w=mb_conv_w,
             mb_conv_b=mb_conv_b, mb_dt_bias=mb_dt_bias, mb_A_log=mb_A_log, mb_D=mb_D, mb_norm_w=mb_norm_w,
             mb_out_w=mb_out_w, mla_w_down=mla_w_down, mla_q_norm=mla_q_norm, mla_w_uq=mla_w_uq,
             mla_kv_norm=mla_kv_norm, mla_w_ukv=mla_w_ukv, mla_w_o=mla_w_o)
    nb, n_lat, _ = x.shape
    assert n_lat == MOD_ROWS and nb * ctx.shape[1] == MOD_ROWS and ctx.shape[1] == CTX_LEN
    lat_rows = nb * n_lat
    all_rows = lat_rows + MOD_ROWS
    cs = _rope_table(n_lat)

    cc = jnp.concatenate([c, c_ctx[None], jnp.zeros((16 - nb - 1, D_MODEL), F32)], axis=0)
    mods = _mod_all(cc, ada_w, ada_b).reshape(DEPTH, 16, N_MOD, D_MODEL)
    w_ffn_in, w_ffn_out = ffn_in.astype(BF16), ffn_out.astype(BF16)
    h = None

    for i in range(DEPTH):
        kind, j, last = i % N_MIXERS, i // N_MIXERS, i == DEPTH - 1
        mod = mods[i]
        ctx_needed = not (last and kind == 0)
        ctx_out = not last
        rows = all_rows if ctx_needed else lat_rows
        if i == 0:
            h = _ffn(x.reshape(lat_rows, D_MODEL), mod, w_ffn_in, w_ffn_out, i, 0, rows=lat_rows, slot=0,
                     out_rows=rows)
            if ctx_needed:
                h = _ffn(ctx.reshape(MOD_ROWS, D_MODEL), mod, w_ffn_in, w_ffn_out, i, 0, rows=MOD_ROWS, slot=0,
                         out_row0=lat_rows, dest=h)
        else:
            h = _ffn(h, mod, w_ffn_in, w_ffn_out, i, 0, rows=rows, slot=0)
        if kind == 0:
            h = _hyena_layer(h, mod, rows, nb, j, ctx_out, p)
        elif kind == 1:
            h = _mamba_layer(h, mod, rows, nb, j, p)
        else:
            h = _mla_layer(h, mod, rows, nb, j, cs, p)
        rows = all_rows if ctx_out else lat_rows
        h = _ffn(h, mod, w_ffn_in, w_ffn_out, i, 1, rows=rows, slot=6, final_w=final_norm_w if last else None)
    return h[:lat_rows].reshape(nb, n_lat, D_MODEL)
```

```python
import functools
import math

import jax
import jax.numpy as jnp
from jax import lax
from jax.experimental import pallas as pl
from jax.experimental.pallas import tpu as pltpu

F32 = jnp.float32
BF16 = jnp.bfloat16
HIGHEST = lax.Precision.HIGHEST

D_MODEL = 1024
DEPTH = 4
GRID_W = 64
CTX_LEN = 256
N_MIXERS = 3
N_MOD = 9
FFN_DIM = 2816
EPS = 1e-6

HYENA_EMB = 33
HYENA_BANDS = (HYENA_EMB - 1) // 2
HYENA_FILTER_HIDDEN = 64
HYENA_FAST_DECAY = 0.3
HYENA_SLOW_DECAY = 1.5
HYENA_DECAY_TARGET = 1e-2

SSM_INNER = 2 * D_MODEL
SSM_HEAD_DIM = 64
SSM_HEADS = SSM_INNER // SSM_HEAD_DIM
SSM_GROUPS = 8
HEADS_PER_GROUP = SSM_HEADS // SSM_GROUPS
SSM_STATE = 128
SSM_CHUNK = 128
SSM_BC_DIM = SSM_GROUPS * SSM_STATE
SSM_CONV_DIM = SSM_INNER + 2 * SSM_BC_DIM
SSM_IN_DIM = SSM_INNER + SSM_CONV_DIM + 2 * SSM_HEADS
SSM_ZDT_PAD = SSM_INNER + 128

MLA_HEADS = 16
MLA_NOPE = 64
MLA_ROPE = 32
MLA_V = 64
MLA_Q_RANK = 768
MLA_KV_RANK = 256
MLA_QK = MLA_NOPE + MLA_ROPE
MLA_SCALE = MLA_QK ** -0.5
MLA_DOWN_PAD = 1152
ROPE_AXIS = MLA_ROPE // 2
ROPE_BASE = 10000.0

MOD_ROWS = 2048
VMEM_LIMIT_BYTES = 56 * 1024 * 1024


def _params(*sem):
    return pltpu.CompilerParams(dimension_semantics=sem, vmem_limit_bytes=VMEM_LIMIT_BYTES)


def _rms(x):
    return x * lax.rsqrt(jnp.mean(x * x, axis=-1, keepdims=True) + EPS)


def _silu(x):
    return x * jax.nn.sigmoid(x)


def _dot(a, b, **kw):
    return jnp.dot(a, b, preferred_element_type=F32, **kw)


def _dot_nt(a, b):
    return lax.dot_general(a, b, (((1,), (1,)), ((), ())), preferred_element_type=F32)


def _mod_kernel(x_ref, w_ref, b_ref, o_ref):
    a = _silu(x_ref[...]).astype(BF16)
    o_ref[...] = _dot(a, w_ref[...].astype(BF16)) + b_ref[...]


def _mod_all(cc, ada_w, ada_b):
    n = N_MOD * D_MODEL
    tn = 1024
    return pl.pallas_call(
        _mod_kernel,
        grid=(DEPTH, n // tn),
        in_specs=[pl.BlockSpec((16, D_MODEL), lambda l, j: (0, 0)),
                  pl.BlockSpec((None, D_MODEL, tn), lambda l, j: (l, 0, j)),
                  pl.BlockSpec((None, 1, tn), lambda l, j: (l, 0, j))],
        out_specs=pl.BlockSpec((None, 16, tn), lambda l, j: (l, 0, j)),
        out_shape=jax.ShapeDtypeStruct((DEPTH, 16, n), F32),
        compiler_params=_params("parallel", "parallel"),
        name="adaln_mod",
    )(cc, ada_w, ada_b.reshape(DEPTH, 1, n))


HALO = 16


def _mm_kernel(*refs, pro, epi, has_bias, slot, gslot, tile0, seq_len, has_dest):
    conv = epi in ("conv", "conv_silu")
    it = iter(refs)
    x_ref, w_ref = next(it), next(it)
    xp_ref, xn_ref = (next(it), next(it)) if conv else (None, None)
    mod_ref = next(it) if (pro == "adaln" or epi == "gres") else None
    ssd_refs = [next(it) for _ in range(5)] if pro == "ssd" else None
    b_ref = next(it) if has_bias else None
    res_ref = next(it) if epi == "gres" else None
    cw_ref, cb_ref = (next(it), next(it)) if conv else (None, None)
    if has_dest:
        next(it)
    o_ref, xa_ref = next(it), next(it)
    tm = x_ref.shape[0]

    def prologue(x):
        x = x.astype(F32)
        if pro == "adaln":
            x = _rms(x) * (1.0 + mod_ref[0, slot + 1:slot + 2, :]) + mod_ref[0, slot:slot + 1, :]
        elif pro == "ssd":
            yb_ref, xs_ref, z_ref, dsk_ref, gn_ref = ssd_refs
            y = (x + yb_ref[...].astype(F32) + dsk_ref[...] * xs_ref[...].astype(F32)) * _silu(z_ref[...])
            gw = SSM_INNER // SSM_GROUPS
            x = jnp.concatenate([_rms(y[:, g * gw:(g + 1) * gw]) * gn_ref[:, g * gw:(g + 1) * gw]
                                 for g in range(SSM_GROUPS)], axis=1)
        return x

    if conv:
        row0 = (tile0 + pl.program_id(0)) * tm
        keep_top = jnp.where((row0 & (seq_len - 1)) == 0, 0.0, 1.0)
        keep_bot = jnp.where(((row0 + tm) & (seq_len - 1)) == 0, 0.0, 1.0)

    @pl.when(pl.program_id(1) == 0)
    def _():
        if conv:
            xa_ref[0:HALO, :] = (prologue(xp_ref[...]) * keep_top).astype(BF16)
            xa_ref[HALO:HALO + tm, :] = prologue(x_ref[...]).astype(BF16)
            xa_ref[HALO + tm:2 * HALO + tm, :] = (prologue(xn_ref[...]) * keep_bot).astype(BF16)
        else:
            xa_ref[...] = prologue(x_ref[...]).astype(BF16)

    acc = _dot(xa_ref[...], w_ref[...])
    if conv:
        w0, w1, w2 = cw_ref[0:1, :], cw_ref[1:2, :], cw_ref[2:3, :]
        prev = pltpu.roll(acc, 1, 0)[HALO:HALO + tm]
        nxt = pltpu.roll(acc, tm + 2 * HALO - 1, 0)[HALO:HALO + tm]
        const = cb_ref[...] + (b_ref[...] * (w0 + w1 + w2) if has_bias else 0.0)
        y = w0 * prev + w1 * acc[HALO:HALO + tm] + w2 * nxt + const
        act = (lambda v: _silu(v.astype(o_ref.dtype))) if epi == "conv_silu" else (lambda v: v)
        o_ref[...] = act(y).astype(o_ref.dtype)
        if has_bias:
            o_ref[0:1, :] = act(y[0:1] - (1.0 - keep_top) * (b_ref[...] * w0)).astype(o_ref.dtype)
            o_ref[tm - 1:tm, :] = act(y[tm - 1:tm] - (1.0 - keep_bot) * (b_ref[...] * w2)).astype(o_ref.dtype)
        return
    if has_bias:
        acc = acc + b_ref[...]
    if epi == "gres":
        acc = res_ref[...] + mod_ref[0, gslot:gslot + 1, :] * acc
    o_ref[...] = acc.astype(o_ref.dtype)


def _mm(x, w, *, rows, tn, tm=1024, pro="none", epi="none", mod=None, slot=0, gslot=0,
        ssd=None, bias=None, res=None, conv_w=None, conv_b=None, seq_len=None, row0=0, out_rows=None,
        dest=None, out_dtype=F32, name="mm"):
    k, n = w.shape
    assert rows % tm == 0 and row0 % tm == 0 and n % tn == 0 and MOD_ROWS % tm == 0
    if epi == "gres":
        assert tn == n == D_MODEL
    conv = epi in ("conv", "conv_silu")
    per = MOD_ROWS // tm
    t0 = row0 // tm
    ins = [x, w]
    specs = [pl.BlockSpec((tm, k), lambda i, j: (t0 + i, 0)),
             pl.BlockSpec((k, tn), lambda i, j: (0, j))]
    if conv:
        assert seq_len % tm == 0 and (bias is None or out_dtype == F32)
        hb = tm // HALO
        last = (row0 + rows) // HALO - 1
        ins += [x, x]
        specs += [pl.BlockSpec((HALO, k), lambda i, j: (jnp.maximum((t0 + i) * hb - 1, 0), 0)),
                  pl.BlockSpec((HALO, k), lambda i, j: (jnp.minimum((t0 + i + 1) * hb, last), 0))]
    if pro == "adaln" or epi == "gres":
        ins.append(mod)
        specs.append(pl.BlockSpec((1, N_MOD, D_MODEL), lambda i, j: ((t0 + i) // per, 0, 0)))
    if pro == "ssd":
        yb, xs, z, d_skip, norm_w = ssd
        ins += [yb, xs, z, d_skip.reshape(1, k), norm_w.reshape(1, k)]
        specs += [pl.BlockSpec((tm, k), lambda i, j: (t0 + i, 0))] * 3 + [pl.BlockSpec((1, k), lambda i, j: (0, 0))] * 2
    if bias is not None:
        ins.append(bias.reshape(1, n))
        specs.append(pl.BlockSpec((1, tn), lambda i, j: (0, j)))
    if epi == "gres":
        ins.append(res)
        specs.append(pl.BlockSpec((tm, tn), lambda i, j: (t0 + i, j)))
    if conv:
        ins += [conv_w, conv_b.reshape(1, n)]
        specs += [pl.BlockSpec((3, tn), lambda i, j: (0, j)), pl.BlockSpec((1, tn), lambda i, j: (0, j))]
    aliases = {}
    out_shape = jax.ShapeDtypeStruct((out_rows or row0 + rows, n), out_dtype)
    if dest is not None:
        aliases = {len(ins): 0}
        ins.append(dest)
        specs.append(pl.BlockSpec(memory_space=pl.ANY))
        out_shape = jax.ShapeDtypeStruct(dest.shape, dest.dtype)
    return pl.pallas_call(
        functools.partial(_mm_kernel, pro=pro, epi=epi, has_bias=bias is not None, slot=slot, gslot=gslot,
                          tile0=t0, seq_len=seq_len, has_dest=dest is not None),
        grid=(rows // tm, n // tn),
        in_specs=specs,
        out_specs=pl.BlockSpec((tm, tn), lambda i, j: (t0 + i, j)),
        out_shape=out_shape,
        scratch_shapes=[pltpu.VMEM((tm + 2 * HALO if conv else tm, k), BF16)],
        input_output_aliases=aliases,
        compiler_params=_params("parallel", "arbitrary"),
        name=name,
    )(*ins)


def _ffn_kernel(*refs, slot, final, fc, has_dest):
    if has_dest:
        refs = refs[:-2] + refs[-1:]
    if final:
        x_ref, mod_ref, wi_ref, wo_ref, fw_ref, o_ref = refs
    else:
        x_ref, mod_ref, wi_ref, wo_ref, o_ref = refs
    x = x_ref[...]
    a = (_rms(x) * (1.0 + mod_ref[0, slot + 1:slot + 2, :]) + mod_ref[0, slot:slot + 1, :]).astype(BF16)
    acc = None
    for c in range(0, FFN_DIM, fc):
        g = _dot(a, wi_ref[:, c:c + fc])
        u = _dot(a, wi_ref[:, FFN_DIM + c:FFN_DIM + c + fc])
        t = _dot((_silu(g) * u).astype(BF16), wo_ref[c:c + fc, :])
        acc = t if acc is None else acc + t
    out = x + (0.5 * mod_ref[0, slot + 2:slot + 3, :]) * acc
    if final:
        out = _rms(out) * fw_ref[...]
    o_ref[...] = out


def _ffn(h, mod, w_in, w_out, layer, k, *, rows, slot, final_w=None, out_row0=0, out_rows=None, dest=None,
         tm=512, fc=FFN_DIM):
    per = MOD_ROWS // tm
    o0 = out_row0 // tm
    final = final_w is not None
    resident = pl.Buffered(1)
    ins = [h, mod, w_in, w_out]
    specs = [pl.BlockSpec((tm, D_MODEL), lambda i: (i, 0)),
             pl.BlockSpec((1, N_MOD, D_MODEL), lambda i: ((o0 + i) // per, 0, 0)),
             pl.BlockSpec((None, None, D_MODEL, 2 * FFN_DIM), lambda i: (layer, k, 0, 0), pipeline_mode=resident),
             pl.BlockSpec((None, None, FFN_DIM, D_MODEL), lambda i: (layer, k, 0, 0), pipeline_mode=resident)]
    if final:
        ins.append(final_w.reshape(1, D_MODEL))
        specs.append(pl.BlockSpec((1, D_MODEL), lambda i: (0, 0)))
    aliases = {}
    out_shape = jax.ShapeDtypeStruct((out_rows or rows, D_MODEL), F32)
    if dest is not None:
        aliases = {len(ins): 0}
        ins.append(dest)
        specs.append(pl.BlockSpec(memory_space=pl.ANY))
        out_shape = jax.ShapeDtypeStruct(dest.shape, dest.dtype)
    return pl.pallas_call(
        functools.partial(_ffn_kernel, slot=slot, final=final, fc=fc, has_dest=dest is not None),
        grid=(rows // tm,),
        in_specs=specs,
        out_specs=pl.BlockSpec((tm, D_MODEL), lambda i: (o0 + i, 0)),
        out_shape=out_shape,
        input_output_aliases=aliases,
        compiler_params=_params("parallel"),
        name="ffn",
    )(*ins)


FILTER_W = 128


def _filter_kernel(za_ref, zb_ref, w1_ref, b1_ref, fr_ref, w2_ref, b2_ref, w3f_ref, w3b_ref, dl_ref,
                   sum_ref, diff_ref):
    fr = fr_ref[...]

    def taps(z_ref, w3_ref):
        z = z_ref[...]
        hid = jnp.sin(fr * (_dot(z, w1_ref[...], precision=HIGHEST) + b1_ref[...]))
        hid = jnp.sin(fr * (_dot(hid, w2_ref[...], precision=HIGHEST) + b2_ref[...]))
        window = jnp.exp(-z[:, 0:1] * dl_ref[...]) * z[:, FILTER_W - 1:FILTER_W]
        h = _dot(hid.astype(BF16), w3_ref[...].astype(BF16))
        return jnp.concatenate([h[:, :D_MODEL] * window, h[:, D_MODEL:] * window], axis=1)

    hf = taps(za_ref, w3f_ref)
    hb = taps(zb_ref, w3b_ref)
    sum_ref[...] = (hf + hb).astype(BF16)
    diff_ref[...] = (hb - hf).astype(BF16)


def _hyena_filters(L, poly, w1, b1, freq, w2, b2, w3):
    S = L // poly
    nrho = 2 * poly - 1
    jj = jnp.arange(S, dtype=jnp.int32)
    rho = jnp.arange(-(poly - 1), poly, dtype=jnp.int32)
    t_tab = jnp.linspace(0.0, 1.0, L, dtype=F32)
    bands = jnp.linspace(1e-4, HYENA_BANDS - 1, HYENA_BANDS, dtype=F32)

    def feats(pos, lowest):
        valid = (pos >= lowest).astype(F32)
        pc = jnp.maximum(pos, 0)
        ang = (2.0 * math.pi / L) * pc.astype(F32)[..., None] * bands
        z = jnp.concatenate([jnp.take(t_tab, pc)[..., None], jnp.cos(ang), -jnp.sin(ang)], axis=-1)
        z = jnp.pad(z, ((0, 0), (0, 0), (0, FILTER_W - 1 - HYENA_EMB)))
        return jnp.concatenate([z, valid[..., None]], axis=-1).reshape(nrho * S, FILTER_W)

    za = feats(poly * jj[None, :] + rho[:, None], 0)
    zb = feats(poly * jj[None, :] - rho[:, None], 1)
    ph = FILTER_W - HYENA_FILTER_HIDDEN
    w1 = jnp.pad(w1, ((0, FILTER_W - HYENA_EMB), (0, ph)))
    w2 = jnp.pad(w2, ((0, ph), (0, ph)))
    w3 = jnp.pad(w3, ((0, ph), (0, 0))).reshape(FILTER_W, 2, 2, D_MODEL)
    w3f = w3[:, :, 0].reshape(FILTER_W, 2 * D_MODEL)
    w3b = w3[:, :, 1].reshape(FILTER_W, 2 * D_MODEL)
    b1, b2, freq = (jnp.pad(t, (0, ph)).reshape(1, FILTER_W) for t in (b1, b2, freq))
    max_decay = math.log(HYENA_DECAY_TARGET) / HYENA_FAST_DECAY
    min_decay = math.log(HYENA_DECAY_TARGET) / HYENA_SLOW_DECAY
    deltas = jnp.abs(jnp.linspace(min_decay, max_decay, D_MODEL, dtype=F32))
    deltas = deltas.reshape(1, D_MODEL)
    full = lambda shape: pl.BlockSpec(shape, lambda i: (0, 0))
    sq = full((FILTER_W, FILTER_W))
    vec = full((1, FILTER_W))
    return pl.pallas_call(
        _filter_kernel,
        grid=(nrho,),
        in_specs=[pl.BlockSpec((S, FILTER_W), lambda i: (i, 0)), pl.BlockSpec((S, FILTER_W), lambda i: (i, 0)),
                  sq, vec, vec, sq, vec, full((FILTER_W, 2 * D_MODEL)), full((FILTER_W, 2 * D_MODEL)),
                  full((1, D_MODEL))],
        out_specs=[pl.BlockSpec((S, 2 * D_MODEL), lambda i: (i, 0))] * 2,
        out_shape=[jax.ShapeDtypeStruct((nrho * S, 2 * D_MODEL), BF16)] * 2,
        compiler_params=_params("parallel"),
        name="hyena_filter",
    )(za, zb, w1, b1, freq, w2, b2, w3f, w3b, deltas)


def _lmm_kernel(*refs, epi, tf, scale, has_dest):
    it = iter(refs)
    a_ref, x_ref = next(it), next(it)
    if epi == "spec":
        k_ref = next(it)
    elif epi == "gate":
        g_ref, v_ref, bias_ref = next(it), next(it), next(it)
    if has_dest:
        next(it)
    o_ref, xb_ref = next(it), next(it)

    @pl.when(pl.program_id(2) == 0)
    def _():
        xb_ref[...] = x_ref[...].astype(BF16)

    acc = _dot(a_ref[...], xb_ref[...])
    if epi == "spec":
        xr, xs = acc[:tf], acc[tf:]
        kr, ki = k_ref[0:tf, :], k_ref[tf:2 * tf, :]
        o_ref[0:tf, :] = ((xr * kr + xs * ki) * scale).astype(o_ref.dtype)
        o_ref[tf:2 * tf, :] = ((xs * kr - xr * ki) * scale).astype(o_ref.dtype)
    elif epi == "gate":
        o_ref[...] = (g_ref[...] * (acc + v_ref[...].astype(F32) * bias_ref[...])).astype(o_ref.dtype)
    else:
        o_ref[...] = acc.astype(o_ref.dtype)


def _lmm(a, x, *, nb, ncb, tm, n, x_row0=0, x_cb0=0, epi="none", tf=0, scale=1.0, kspec=None, k_cb=0,
         g=None, g_cb=0, g_row0=0, v=None, v_cb=0, v_row0=0, bias=None, out_dtype=F32, dest=None, dest_row0=0,
         name="lmm"):
    mo, k = a.shape
    nm = mo // tm
    xr0 = x_row0 // k
    or0 = dest_row0 // tm
    ins = [a, x]
    specs = [pl.BlockSpec((tm, k), lambda b, c, m: (m, 0)),
             pl.BlockSpec((k, n), lambda b, c, m: (xr0 + b, x_cb0 + c))]
    if epi == "spec":
        ins.append(kspec)
        specs.append(pl.BlockSpec((tm, n), lambda b, c, m: (m, k_cb)))
    elif epi == "gate":
        gr0, vr0 = g_row0 // tm, v_row0 // tm
        ins += [g, v, bias.reshape(1, n)]
        specs += [pl.BlockSpec((tm, n), lambda b, c, m: (gr0 + b * nm + m, g_cb)),
                  pl.BlockSpec((tm, n), lambda b, c, m: (vr0 + b * nm + m, v_cb)),
                  pl.BlockSpec((1, n), lambda b, c, m: (0, 0))]
    aliases = {}
    out_shape = jax.ShapeDtypeStruct((nb * mo, ncb * n), out_dtype)
    if dest is not None:
        aliases = {len(ins): 0}
        ins.append(dest)
        specs.append(pl.BlockSpec(memory_space=pl.ANY))
        out_shape = jax.ShapeDtypeStruct(dest.shape, dest.dtype)
    return pl.pallas_call(
        functools.partial(_lmm_kernel, epi=epi, tf=tf, scale=scale, has_dest=dest is not None),
        grid=(nb, ncb, nm),
        in_specs=specs,
        out_specs=pl.BlockSpec((tm, n), lambda b, c, m: (or0 + b * nm + m, c)),
        out_shape=out_shape,
        scratch_shapes=[pltpu.VMEM((k, n), BF16)],
        input_output_aliases=aliases,
        compiler_params=_params("parallel", "parallel", "arbitrary"),
        name=name,
    )(*ins)


def _dft_mats(L, tf):
    idx = jnp.arange(L, dtype=jnp.int32)
    ph = ((2 * idx[:, None] + 1) * idx[None, :]) % (4 * L)
    ang = ph.astype(F32) * (2.0 * math.pi / (4 * L))
    c, s = jnp.cos(ang), jnp.sin(ang)
    nf = L // tf
    fwd = jnp.stack([c.reshape(nf, tf, L), s.reshape(nf, tf, L)], axis=1).reshape(2 * L, L)
    inv = jnp.stack([c.T.reshape(L, nf, tf), s.T.reshape(L, nf, tf)], axis=2).reshape(L, 2 * L)
    return c.astype(BF16), s.astype(BF16), fwd.astype(BF16), inv.astype(BF16)


def _interleave(kr, ki, tf):
    L, n = kr.shape
    return jnp.stack([kr.reshape(L // tf, tf, n), ki.reshape(L // tf, tf, n)], axis=1).reshape(2 * L, n)


def _hyena_core(u, L, nb, row0, fparams, hy_bias, name, dest=None):
    tf = min(L, 256)
    tmi = min(L, 256)
    cmat, smat, fwd, inv = _dft_mats(L, tf)
    hsum, hdiff = _hyena_filters(L, 1, *fparams)
    kr = _lmm(cmat, hsum, nb=1, ncb=2, tm=tf, n=D_MODEL, name=name + "_kr")
    ki = _lmm(smat, hdiff, nb=1, ncb=2, tm=tf, n=D_MODEL, name=name + "_ki")
    kspec = _interleave(kr, ki, tf)
    scale = 1.0 / L
    y1 = _lmm(fwd, u, nb=nb, ncb=1, tm=2 * tf, n=D_MODEL, x_row0=row0, x_cb0=2, epi="spec", tf=tf, scale=scale,
              kspec=kspec, k_cb=0, out_dtype=BF16, name=name + "_fwd1")
    z = _lmm(inv, y1, nb=nb, ncb=1, tm=tmi, n=D_MODEL, epi="gate", g=u, g_cb=0, g_row0=row0, v=u, v_cb=2,
             v_row0=row0, bias=hy_bias[0], out_dtype=F32, name=name + "_inv1")
    y2 = _lmm(fwd, z, nb=nb, ncb=1, tm=2 * tf, n=D_MODEL, epi="spec", tf=tf, scale=scale,
              kspec=kspec, k_cb=1, out_dtype=BF16, name=name + "_fwd2")
    return _lmm(inv, y2, nb=nb, ncb=1, tm=tmi, n=D_MODEL, epi="gate", g=u, g_cb=1, g_row0=row0, v=z, v_cb=0,
                bias=hy_bias[1], out_dtype=F32, dest=dest, dest_row0=row0, name=name + "_inv2")


POLY = 4
POLY_SEQS = 2


def _poly_conv_kernel(a_ref, ai_ref, u_ref, kr_ref, ki_ref, g_ref, bias_ref, o_ref, *, scale):
    S = a_ref.shape[1]
    L = S * POLY
    tc = u_ref.shape[1]
    for row0 in range(0, u_ref.shape[0], L):
        phases = [u_ref[pl.ds(row0 + r, S, stride=POLY), :] for r in range(POLY)]
        e = _dot(a_ref[...], jnp.concatenate([ph.astype(BF16) for ph in phases], axis=1))
        f = []
        for r in range(POLY):
            fre = fim = None
            for rp in range(POLY):
                kr = kr_ref[r - rp + POLY - 1]
                ki = ki_ref[r - rp + POLY - 1]
                er = e[0:S, rp * tc:(rp + 1) * tc]
                es = e[S:2 * S, rp * tc:(rp + 1) * tc]
                tre = kr * er + ki * es
                tim = kr * es - ki * er
                fre = tre if fre is None else fre + tre
                fim = tim if fim is None else fim + tim
            f.append(jnp.concatenate([fre, fim], axis=0).astype(BF16))
        y = _dot(ai_ref[...], jnp.concatenate(f, axis=1)) * scale
        for r in range(POLY):
            rows = pl.ds(row0 + r, S, stride=POLY)
            o_ref[rows, :] = g_ref[rows, :] * (y[:, r * tc:(r + 1) * tc] + phases[r] * bias_ref[...])


def _poly_conv(fwd, inv, x, x_cb, g, g_cb, kre, kim, order, bias, *, nb, out_rows=None, tc=128, name="hy_conv"):
    L = MOD_ROWS
    S = L // POLY
    nct = D_MODEL // tc
    nrho = 2 * POLY - 1
    assert nb % POLY_SEQS == 0
    rb = POLY_SEQS * L
    return pl.pallas_call(
        functools.partial(_poly_conv_kernel, scale=1.0 / S),
        grid=(nct, nb // POLY_SEQS),
        in_specs=[pl.BlockSpec((2 * S, S), lambda c, b: (0, 0)),
                  pl.BlockSpec((S, 2 * S), lambda c, b: (0, 0)),
                  pl.BlockSpec((rb, tc), lambda c, b: (b, x_cb * nct + c)),
                  pl.BlockSpec((nrho, S, tc), lambda c, b: (0, 0, order * nct + c)),
                  pl.BlockSpec((nrho, S, tc), lambda c, b: (0, 0, order * nct + c)),
                  pl.BlockSpec((rb, tc), lambda c, b: (b, g_cb * nct + c)),
                  pl.BlockSpec((1, tc), lambda c, b: (0, c))],
        out_specs=pl.BlockSpec((rb, tc), lambda c, b: (b, c)),
        out_shape=jax.ShapeDtypeStruct((out_rows or nb * L, D_MODEL), F32),
        compiler_params=_params("parallel", "parallel"),
        name=name,
    )(fwd, inv, x, kre, kim, g, bias.reshape(1, D_MODEL))


def _hyena_core_poly(u, nb, fparams, hy_bias, out_rows):
    S = MOD_ROWS // POLY
    nrho = 2 * POLY - 1
    cmat, smat, fwd, inv = _dft_mats(S, S)
    ksum, kdiff = _hyena_filters(MOD_ROWS, POLY, *fparams)
    kre = _lmm(cmat, ksum, nb=nrho, ncb=2, tm=S, n=D_MODEL, name="hy_lat_kr").reshape(nrho, S, 2 * D_MODEL)
    kim = _lmm(smat, kdiff, nb=nrho, ncb=2, tm=S, n=D_MODEL, name="hy_lat_ki").reshape(nrho, S, 2 * D_MODEL)
    z = _poly_conv(fwd, inv, u, 2, u, 0, kre, kim, 0, hy_bias[0], nb=nb, name="hy_lat_conv1")
    return _poly_conv(fwd, inv, z, 0, u, 1, kre, kim, 1, hy_bias[1], nb=nb, out_rows=out_rows, name="hy_lat_conv2")


def _softplus(x):
    return jnp.maximum(x, 0.0) + jnp.log(1.0 + jnp.exp(-jnp.abs(x)))


def _ssd_kernel(xbc_f, dt_f, dtt_f, xbc_b, dt_b, dtt_b, dtb_ref, dtbt_ref, al_ref, alt_ref, yf_ref, yb_ref, st_ref):
    @pl.when(pl.program_id(1) == 0)
    def _():
        st_ref[...] = jnp.zeros_like(st_ref)

    _ssd_chunk(0, xbc_f, dt_f, dtt_f, dtb_ref[0], dtbt_ref[0], al_ref[0], alt_ref[0], yf_ref, st_ref.at[0])
    _ssd_chunk(1, xbc_b, dt_b, dtt_b, dtb_ref[1], dtbt_ref[1], al_ref[1], alt_ref[1], yb_ref, st_ref.at[1])


def _ssd_chunk(d, xbc_ref, dt_ref, dtt_ref, dt_bias, dt_bias_t, a_log, a_log_t, y_ref, st_ref):
    T = SSM_CHUNK
    N = SSM_STATE
    ri = lax.broadcasted_iota(jnp.int32, (T, T), 0)
    ci = lax.broadcasted_iota(jnp.int32, (T, T), 1)
    mask = (ri >= ci) if d == 0 else (ri <= ci)
    tri_col = mask.astype(F32)
    tri_row = ((ci >= ri) if d == 0 else (ci <= ri)).astype(F32)
    low = lax.broadcasted_iota(jnp.int32, (T, 2 * SSM_HEAD_DIM), 1) < SSM_HEAD_DIM

    a = _softplus(dt_ref[...] + dt_bias) * -jnp.exp(a_log)
    dtt = _softplus(dtt_ref[...] + dt_bias_t)
    at = dtt * -jnp.exp(a_log_t)
    cs_col = _dot(tri_col, a, precision=HIGHEST)
    cs_row = _dot(at, tri_row, precision=HIGHEST)
    tot = jnp.sum(at, axis=1, keepdims=True)
    row_dt = cs_row - jnp.log(dtt)
    w_out = dtt * jnp.exp(tot - cs_row)
    e_tot = jnp.exp(tot)

    for g in range(SSM_GROUPS):
        b0 = SSM_INNER + g * N
        c0 = SSM_INNER + SSM_BC_DIM + g * N
        bt = jnp.transpose(xbc_ref[:, b0:b0 + N].astype(F32))
        cg = xbc_ref[:, c0:c0 + N].astype(F32)
        cb = _dot(cg.astype(BF16), bt.astype(BF16))
        for k in range(HEADS_PER_GROUP // 2):
            h0 = g * HEADS_PER_GROUP + 2 * k
            lanes = slice(h0 * SSM_HEAD_DIM, (h0 + 2) * SSM_HEAD_DIM)
            xs = xbc_ref[:, lanes].astype(BF16)
            st = st_ref[g, :, k * 128:(k + 1) * 128]
            rhs = jnp.concatenate([xs, st.astype(BF16)], axis=0)
            ys, upds = [], []
            for h in (h0, h0 + 1):
                col = jnp.broadcast_to(cs_col[:, h:h + 1], (T, T))
                m = jnp.where(mask, jnp.exp(col - row_dt[h:h + 1, :]), 0.0) * cb
                lhs = jnp.concatenate([m, jnp.exp(col) * cg], axis=1).astype(BF16)
                ys.append(_dot(lhs, rhs))
                upds.append(_dot((bt * w_out[h:h + 1, :]).astype(BF16), xs))
            y_ref[:, lanes] = jnp.where(low, ys[0], ys[1]).astype(y_ref.dtype)
            decay = jnp.where(low, e_tot[h0:h0 + 1, :], e_tot[h0 + 1:h0 + 2, :])
            st_ref[g, :, k * 128:(k + 1) * 128] = st * decay + jnp.where(low, upds[0], upds[1])


def _ssd(xbc, dta, dtt, dt_bias, a_log, *, nb):
    m = xbc.shape[0]
    T = SSM_CHUNK
    H = SSM_HEADS
    lat_c = MOD_ROWS // T
    ctx_c = CTX_LEN // T
    ctx0 = nb * lat_c
    steps = lat_c + ctx_c

    def rb(b, d, t):
        ctx = ctx0 + ctx_c * b + jnp.where(d == 0, t, ctx_c - 1 - t)
        lat = lat_c * b + jnp.where(d == 0, t - ctx_c, steps - 1 - t)
        return jnp.where(t < ctx_c, ctx, lat)

    def direction(d):
        return [pl.BlockSpec((T, SSM_CONV_DIM), lambda b, t: (rb(b, d, t), 0)),
                pl.BlockSpec((None, T, H), lambda b, t: (d, rb(b, d, t), 0)),
                pl.BlockSpec((None, H, T), lambda b, t: (d, 0, rb(b, d, t)))]

    whole = lambda shape: pl.BlockSpec(shape, lambda b, t: (0, 0, 0))
    return pl.pallas_call(
        _ssd_kernel,
        grid=(nb, steps),
        in_specs=direction(0) + direction(1) + [whole((2, 1, H)), whole((2, H, 1)), whole((2, 1, H)), whole((2, H, 1))],
        out_specs=[pl.BlockSpec((T, SSM_INNER), lambda b, t: (rb(b, 0, t), 0)),
                   pl.BlockSpec((T, SSM_INNER), lambda b, t: (rb(b, 1, t), 0))],
        out_shape=[jax.ShapeDtypeStruct((m, SSM_INNER), BF16)] * 2,
        scratch_shapes=[pltpu.VMEM((2, SSM_GROUPS, SSM_STATE, HEADS_PER_GROUP * SSM_HEAD_DIM), F32)],
        compiler_params=_params("parallel", "arbitrary"),
        name="ssd_scan",
    )(xbc, dta, dtt, xbc, dta, dtt, dt_bias.reshape(2, 1, H), dt_bias.reshape(2, H, 1),
      a_log.reshape(2, 1, H), a_log.reshape(2, H, 1))


HEAD_W = 2 * MLA_NOPE
ATTN_SUB_ROWS = 256


def _attn_kernel(*refs, has_lat):
    if has_lat:
        q_ref, kvc_ref, krc_ref, kv_ref, kr_ref, tq_ref, tk_ref, o_ref, kc_scr, vc_scr, k_scr, v_scr = refs
    else:
        q_ref, kvc_ref, krc_ref, _, o_ref, kc_scr, vc_scr = refs
    R = MLA_ROPE
    kscale = MLA_SCALE * math.log2(math.e)

    def build_keys():
        krc = krc_ref[...].astype(F32)
        lane_c = lax.broadcasted_iota(jnp.int32, krc.shape, 1)
        pe_c = jnp.where((lane_c >= MLA_NOPE) & (lane_c < MLA_NOPE + R), pltpu.roll(krc, MLA_NOPE, 1), 0.0)
        for hh in range(2):
            kvh = kvc_ref[:, hh * HEAD_W:(hh + 1) * HEAD_W].astype(F32)
            kc_scr[hh] = (jnp.where(lane_c < MLA_NOPE, kvh, pe_c) * kscale).astype(BF16)
            vc_scr[hh] = jnp.where(lane_c < MLA_NOPE, 1.0, kvh).astype(BF16)
        if has_lat:
            t = kr_ref[...].astype(F32) * tk_ref[...]
            lane = lax.broadcasted_iota(jnp.int32, t.shape, 1)
            krot = jnp.where(lane < R, t + pltpu.roll(t, HEAD_W - R, 1), 0.0)
            krr = pltpu.roll(krot, MLA_NOPE, 1) + pltpu.roll(krot, MLA_NOPE + R, 1)
            for hh in range(2):
                kvh = kv_ref[:, hh * HEAD_W:(hh + 1) * HEAD_W].astype(F32)
                k_scr[hh] = (jnp.where(lane < MLA_NOPE, kvh, krr) * kscale).astype(BF16)
                v_scr[hh] = jnp.where(lane < MLA_NOPE, 1.0, kvh).astype(BF16)

    if has_lat:
        pl.when(pl.program_id(2) == 0)(build_keys)
    else:
        build_keys()

    sub = min(q_ref.shape[0], ATTN_SUB_ROWS)
    for r0 in range(0, q_ref.shape[0], sub):
        rows = slice(r0, r0 + sub)
        res = []
        for hh in range(2):
            q = q_ref[rows, hh * HEAD_W:(hh + 1) * HEAD_W]
            s_c = _dot_nt(q, kc_scr[hh])
            mx = jnp.max(s_c, axis=-1, keepdims=True)
            if has_lat:
                ql = (q.astype(F32) * tq_ref[rows, :]).astype(BF16)
                s_l = _dot_nt(ql, k_scr[hh])
                mx = jnp.maximum(mx, jnp.max(s_l, axis=-1, keepdims=True))
                acc = _dot(jnp.exp2(s_l - mx).astype(BF16), v_scr[hh])
                acc = acc + _dot(jnp.exp2(s_c - mx).astype(BF16), vc_scr[hh])
            else:
                acc = _dot(jnp.exp2(s_c - mx).astype(BF16), vc_scr[hh])
            res.append(acc / pltpu.roll(acc, MLA_V, 1))
        lane_o = lax.broadcasted_iota(jnp.int32, res[0].shape, 1)
        o_ref[rows, :] = jnp.where(lane_o < MLA_V, pltpu.roll(res[0], MLA_V, 1), res[1]).astype(o_ref.dtype)


def _attention(q, kv, dn, cs, *, nb, tq=2048):
    L = MOD_ROWS
    hp = MLA_HEADS // 2
    w = 2 * HEAD_W
    nq = L // tq
    cb = nb * (L // CTX_LEN)
    kr_cb = 0
    tab_q = jnp.concatenate([jnp.ones((L, MLA_NOPE), F32), cs], axis=1)
    tab_k = jnp.concatenate([cs, jnp.zeros((L, HEAD_W - 2 * MLA_ROPE), F32)], axis=1)
    lat = pl.pallas_call(
        functools.partial(_attn_kernel, has_lat=True),
        grid=(nb, hp, nq),
        in_specs=[pl.BlockSpec((tq, w), lambda b, p, i: (b * nq + i, p)),
                  pl.BlockSpec((CTX_LEN, w), lambda b, p, i: (cb + b, p)),
                  pl.BlockSpec((CTX_LEN, HEAD_W), lambda b, p, i: (cb + b, kr_cb)),
                  pl.BlockSpec((L, w), lambda b, p, i: (b, p)),
                  pl.BlockSpec((L, HEAD_W), lambda b, p, i: (b, kr_cb)),
                  pl.BlockSpec((tq, HEAD_W), lambda b, p, i: (i, 0)),
                  pl.BlockSpec((L, HEAD_W), lambda b, p, i: (0, 0))],
        out_specs=pl.BlockSpec((tq, 2 * MLA_V), lambda b, p, i: (b * nq + i, p)),
        out_shape=jax.ShapeDtypeStruct((nb * (L + CTX_LEN), MLA_HEADS * MLA_V), BF16),
        scratch_shapes=[pltpu.VMEM((2, CTX_LEN, HEAD_W), BF16)] * 2 + [pltpu.VMEM((2, L, HEAD_W), BF16)] * 2,
        compiler_params=_params("parallel", "parallel", "arbitrary"),
        name="mla_attn",
    )(q, kv, dn, kv, dn, tab_q, tab_k)
    return pl.pallas_call(
        functools.partial(_attn_kernel, has_lat=False),
        grid=(nb, hp),
        in_specs=[pl.BlockSpec((CTX_LEN, w), lambda b, p: (cb + b, p)),
                  pl.BlockSpec((CTX_LEN, w), lambda b, p: (cb + b, p)),
                  pl.BlockSpec((CTX_LEN, HEAD_W), lambda b, p: (cb + b, kr_cb)),
                  pl.BlockSpec(memory_space=pl.ANY)],
        out_specs=pl.BlockSpec((CTX_LEN, 2 * MLA_V), lambda b, p: (cb + b, p)),
        out_shape=jax.ShapeDtypeStruct(lat.shape, lat.dtype),
        scratch_shapes=[pltpu.VMEM((2, CTX_LEN, HEAD_W), BF16)] * 2,
        input_output_aliases={3: 0},
        compiler_params=_params("parallel", "parallel"),
        name="mla_attn_ctx",
    )(q, kv, dn, lat)


def _rot_cols(w):
    wp = w.reshape(w.shape[:-1] + (2, 2, ROPE_AXIS // 2))
    return jnp.stack([-wp[..., 1, :], wp[..., 0, :]], axis=-2).reshape(w.shape)


def _rope_table(n_lat):
    rows = n_lat // GRID_W
    row = jnp.repeat(jnp.arange(rows), GRID_W)
    col = jnp.tile(jnp.arange(GRID_W), rows)
    inv = 1.0 / (ROPE_BASE ** (jnp.arange(0, ROPE_AXIS, 2, dtype=F32) / ROPE_AXIS))
    ar = row.astype(F32)[:, None] * inv[None, :]
    ac = col.astype(F32)[:, None] * inv[None, :]
    ang = jnp.concatenate([ar, ar, ac, ac], axis=-1)
    return jnp.concatenate([jnp.cos(ang), jnp.sin(ang)], axis=-1)


def _conv_proj(h, w, bias, conv_w, conv_b, mod, rows, nb, epi, out_dtype, name):
    lat_rows = nb * MOD_ROWS
    common = dict(pro="adaln", mod=mod, slot=3, bias=bias, epi=epi, conv_w=conv_w, conv_b=conv_b, out_dtype=out_dtype)
    out = _mm(h, w, rows=lat_rows, tn=1024, seq_len=MOD_ROWS, out_rows=rows, name=name, **common)
    if rows > lat_rows:
        out = _mm(h, w, rows=rows - lat_rows, tn=w.shape[1], tm=CTX_LEN, seq_len=CTX_LEN, row0=lat_rows, dest=out,
                  name=name + "_ctx", **common)
    return out


def _hyena_layer(h, mod, rows, nb, j, with_ctx, p):
    u = _conv_proj(h, p["hy_in_w"][j].astype(BF16), p["hy_in_b"][j], p["hy_conv_w"][j], p["hy_conv_b"][j], mod,
                   rows, nb, "conv", F32, "hyena_in")
    fparams = (p["hy_pos_w1"][j], p["hy_pos_b1"][j], p["hy_freq"][j], p["hy_pos_w2"][j], p["hy_pos_b2"][j],
               p["hy_pos_w3"][j])
    y = _hyena_core_poly(u, nb, fparams, p["hy_bias"][j], rows)
    if with_ctx:
        y = _hyena_core(u, CTX_LEN, nb, nb * MOD_ROWS, fparams, p["hy_bias"][j], "hy_ctx", dest=y)
    return _mm(y, p["hy_out_w"][j].astype(BF16), rows=rows, tn=D_MODEL, epi="gres", mod=mod, gslot=5,
               bias=p["hy_out_b"][j], res=h, name="hyena_out")


def _mamba_layer(h, mod, rows, nb, j, p):
    w_in = p["mb_in_w"][j].astype(BF16)
    w_zdt = jnp.concatenate([w_in[:, :SSM_INNER], w_in[:, SSM_INNER + SSM_CONV_DIM:],
                             jnp.zeros((D_MODEL, SSM_ZDT_PAD - SSM_INNER - 2 * SSM_HEADS), BF16)], axis=1)
    zx = _mm(h, w_zdt, rows=rows, tn=SSM_ZDT_PAD, pro="adaln", mod=mod, slot=3, name="mamba_in_zdt")
    xbc = _conv_proj(h, w_in[:, SSM_INNER:SSM_INNER + SSM_CONV_DIM], None, p["mb_conv_w"][j], p["mb_conv_b"][j], mod,
                     rows, nb, "conv_silu", BF16, "mamba_in_xbc")
    dtr = zx[:, SSM_INNER:SSM_INNER + 2 * SSM_HEADS].reshape(rows, 2, SSM_HEADS)
    dta = jnp.transpose(dtr, (1, 0, 2))
    dtt = jnp.transpose(dtr, (1, 2, 0))
    yf, yb = _ssd(xbc, dta, dtt, p["mb_dt_bias"][j], p["mb_A_log"][j], nb=nb)
    d_skip = jnp.repeat(p["mb_D"][j], SSM_HEAD_DIM)
    return _mm(yf, p["mb_out_w"][j].astype(BF16), rows=rows, tn=D_MODEL, tm=512, pro="ssd",
               ssd=(yb, xbc, zx, d_skip, p["mb_norm_w"][j]), epi="gres", mod=mod, gslot=5, res=h, name="mamba_out")


def _mla_proj_kernel(x_ref, mod_ref, wd_ref, qn_ref, kvn_ref, wq_ref, wkv_ref, q_ref, kv_ref, kr_ref):
    a = _rms(x_ref[...]) * (1.0 + mod_ref[0, 4:5, :]) + mod_ref[0, 3:4, :]
    dn = _dot(a.astype(BF16), wd_ref[...])
    cq = _rms(dn[:, :MLA_Q_RANK]) * qn_ref[...]
    ckv = _rms(dn[:, MLA_Q_RANK:MLA_Q_RANK + MLA_KV_RANK]) * kvn_ref[...]
    q_ref[...] = _dot(cq.astype(BF16), wq_ref[...]).astype(q_ref.dtype)
    kv_ref[...] = _dot(ckv.astype(BF16), wkv_ref[...]).astype(kv_ref.dtype)
    kr_ref[...] = dn[:, MLA_Q_RANK + MLA_KV_RANK:]


def _mla_proj(h, mod, wd, q_norm, kv_norm, wq, wkv, *, rows, tm=512):
    per = MOD_ROWS // tm
    nq, nkv = wq.shape[1], wkv.shape[1]
    nr = MLA_DOWN_PAD - MLA_Q_RANK - MLA_KV_RANK
    whole = lambda a: pl.BlockSpec(a.shape, lambda i: (0, 0))
    qn, kvn = q_norm.reshape(1, MLA_Q_RANK), kv_norm.reshape(1, MLA_KV_RANK)
    return pl.pallas_call(
        _mla_proj_kernel,
        grid=(rows // tm,),
        in_specs=[pl.BlockSpec((tm, D_MODEL), lambda i: (i, 0)),
                  pl.BlockSpec((1, N_MOD, D_MODEL), lambda i: (i // per, 0, 0)),
                  whole(wd), whole(qn), whole(kvn), whole(wq), whole(wkv)],
        out_specs=[pl.BlockSpec((tm, nq), lambda i: (i, 0)), pl.BlockSpec((tm, nkv), lambda i: (i, 0)),
                   pl.BlockSpec((tm, nr), lambda i: (i, 0))],
        out_shape=[jax.ShapeDtypeStruct((rows, nq), BF16), jax.ShapeDtypeStruct((rows, nkv), BF16),
                   jax.ShapeDtypeStruct((rows, nr), F32)],
        compiler_params=_params("parallel"),
        name="mla_proj",
    )(h, mod, wd, qn, kvn, wq, wkv)


def _mla_layer(h, mod, rows, nb, j, cs, p):
    wd = p["mla_w_down"][j]
    kpe_w = wd[:, MLA_Q_RANK + MLA_KV_RANK:]
    wd = jnp.concatenate([wd, _rot_cols(kpe_w),
                          jnp.zeros((D_MODEL, MLA_DOWN_PAD - wd.shape[1] - MLA_ROPE), F32)], axis=1).astype(BF16)
    wq = p["mla_w_uq"][j].reshape(MLA_Q_RANK, MLA_HEADS, MLA_QK)
    wq = jnp.concatenate([wq, _rot_cols(wq[..., MLA_NOPE:])], axis=-1).reshape(MLA_Q_RANK, -1).astype(BF16)
    q, kv, kr = _mla_proj(h, mod, wd, p["mla_q_norm"][j], p["mla_kv_norm"][j], wq, p["mla_w_ukv"][j].astype(BF16),
                          rows=rows)
    o = _attention(q, kv, kr, cs, nb=nb)
    return _mm(o, p["mla_w_o"][j].astype(BF16), rows=rows, tn=D_MODEL, epi="gres", mod=mod, gslot=5, res=h,
               name="mla_out")


def kernel(x, c, ctx, c_ctx, ada_w, ada_b, ffn_in, ffn_out, hy_in_w, hy_in_b, hy_conv_w, hy_conv_b, hy_pos_w1, hy_pos_b1, hy_freq, hy_pos_w2, hy_pos_b2, hy_pos_w3, hy_bias, hy_out_w, hy_out_b, mb_in_w, mb_conv_w, mb_conv_b, mb_dt_bias, mb_A_log, mb_D, mb_norm_w, mb_out_w, mla_w_down, mla_q_norm, mla_w_uq, mla_kv_norm, mla_w_ukv, mla_w_o, final_norm_w):
    p = dict(hy_in_w=hy_in_w, hy_in_b=hy_in_b, hy_conv_w=hy_conv_w, hy_conv_b=hy_conv_b, hy_pos_w1=hy_pos_w1,
             hy_pos_b1=hy_pos_b1, hy_freq=hy_freq, hy_pos_w2=hy_pos_w2, hy_pos_b2=hy_pos_b2, hy_pos_w3=hy_pos_w3,
             hy_bias=hy_bias, hy_out_w=hy_out_w, hy_out_b=hy_out_b, mb_in_w=mb_in_w, mb_conv_w=mb_conv_w,
             mb_conv_b=mb_conv_b, mb_dt_bias=mb_dt_bias, mb_A_log=mb_A_log, mb_D=mb_D, mb_norm_w=mb_norm_w,
             mb_out_w=mb_out_w, mla_w_down=mla_w_down, mla_q_norm=mla_q_norm, mla_w_uq=mla_w_uq,
             mla_kv_norm=mla_kv_norm, mla_w_ukv=mla_w_ukv, mla_w_o=mla_w_o)
    nb, n_lat, _ = x.shape
    assert n_lat == MOD_ROWS and nb * ctx.shape[1] == MOD_ROWS and ctx.shape[1] == CTX_LEN
    lat_rows = nb * n_lat
    all_rows = lat_rows + MOD_ROWS
    cs = _rope_table(n_lat)

    cc = jnp.concatenate([c, c_ctx[None], jnp.zeros((16 - nb - 1, D_MODEL), F32)], axis=0)
    mods = _mod_all(cc, ada_w, ada_b).reshape(DEPTH, 16, N_MOD, D_MODEL)
    w_ffn_in, w_ffn_out = ffn_in.astype(BF16), ffn_out.astype(BF16)
    h = None

    for i in range(DEPTH):
        kind, j, last = i % N_MIXERS, i // N_MIXERS, i == DEPTH - 1
        mod = mods[i]
        ctx_needed = not (last and kind == 0)
        ctx_out = not last
        rows = all_rows if ctx_needed else lat_rows
        if i == 0:
            h = _ffn(x.reshape(lat_rows, D_MODEL), mod, w_ffn_in, w_ffn_out, i, 0, rows=lat_rows, slot=0,
                     out_rows=rows)
            if ctx_needed:
                h = _ffn(ctx.reshape(MOD_ROWS, D_MODEL), mod, w_ffn_in, w_ffn_out, i, 0, rows=MOD_ROWS, slot=0,
                         out_row0=lat_rows, dest=h)
        else:
            h = _ffn(h, mod, w_ffn_in, w_ffn_out, i, 0, rows=rows, slot=0)
        if kind == 0:
            h = _hyena_layer(h, mod, rows, nb, j, ctx_out, p)
        elif kind == 1:
            h = _mamba_layer(h, mod, rows, nb, j, p)
        else:
            h = _mla_layer(h, mod, rows, nb, j, cs, p)
        rows = all_rows if ctx_out else lat_rows
        h = _ffn(h, mod, w_ffn_in, w_ffn_out, i, 1, rows=rows, slot=6, final_w=final_norm_w if last else None)
    return h[:lat_rows].reshape(nb, n_lat, D_MODEL)
```
